```python
import math
import jax, jax.numpy as jnp
from jax import lax
import numpy as np

D_MODEL = 1024
BATCH = 16
SEQ = 2048
DEPTH = 1

CHUNK = 64
N_META = 16
QBLOCK = 64
A_HEADS = 8
A_WIDTH = D_MODEL // 2
A_HEAD_DIM = A_WIDTH // A_HEADS
IDX_HEADS = 4
IDX_DIM = 64
TOPK_MAX = 256
B_HEADS = 4
B_WIDTH = D_MODEL - A_WIDTH
B_VAL_DIM = B_WIDTH // B_HEADS
B_KEY_DIM = B_VAL_DIM // 2
B_GATE_RANK = 16
B_GATE_TAU = 16.0
D_FF = 2816
CONV_W = 3
LN_EPS = 1e-5
RMS_EPS = 1e-6
ALPHA = (2.0 * DEPTH) ** 0.25
BETA = (8.0 * DEPTH) ** -0.25
IN_COLS = 3 * A_WIDTH + IDX_HEADS * IDX_DIM + IDX_DIM + IDX_HEADS + 2 * B_HEADS * B_KEY_DIM + 2 * B_WIDTH + B_GATE_RANK

kernel_name = "hymba_dsa_gla_deepnorm_convffn"


def _col_layout():
    return (('a_q', A_WIDTH, False), ('a_k', A_WIDTH, False), ('a_v', A_WIDTH, True),
            ('i_q', IDX_HEADS * IDX_DIM, False), ('i_k', IDX_DIM, False), ('i_w', IDX_HEADS, False),
            ('b_q', B_HEADS * B_KEY_DIM, False), ('b_k', B_HEADS * B_KEY_DIM, False), ('b_v', B_WIDTH, True),
            ('b_lr', B_GATE_RANK, False), ('b_og', B_WIDTH, False))


def _split_offsets():
    offs, acc = [], 0
    for _, n, _ in _col_layout()[:-1]:
        acc += n
        offs.append(acc)
    return offs


def layer_norm(x, g, b):
    xf = x.astype(jnp.float32)
    mu = jnp.mean(xf, axis=-1, keepdims=True)
    var = jnp.mean(jnp.square(xf - mu), axis=-1, keepdims=True)
    return ((xf - mu) * lax.rsqrt(var + LN_EPS) * g.astype(jnp.float32) + b.astype(jnp.float32)).astype(x.dtype)


def head_rms_norm(o, g):
    of = o.astype(jnp.float32)
    return (of * lax.rsqrt(jnp.mean(jnp.square(of), axis=-1, keepdims=True) + RMS_EPS) * g.astype(jnp.float32)).astype(o.dtype)


def chunk_ids(n):
    p = jnp.arange(n)
    return jnp.where(p < N_META, 0, 1 + (p - N_META) // CHUNK)


def alibi_slopes(n_heads):
    return jnp.asarray(2.0 ** (-8.0 * np.arange(1, n_heads + 1) / n_heads), dtype=jnp.float32)


def dsa_attention(q, k, v, qi, ki, wi, n_valid, topk):
    B, Tp, H, dh = q.shape
    nblk = Tp // QBLOCK
    pos = jnp.arange(Tp)
    cid = chunk_ids(Tp)
    key_ok = pos < n_valid
    slopes = alibi_slopes(H)
    kf = k.reshape(B, Tp, H * dh)
    vf = v.reshape(B, Tp, H * dh)
    kif = ki.astype(jnp.float32)

    def to_blocks(a):
        return a.reshape((B, nblk, QBLOCK) + a.shape[2:]).swapaxes(0, 1)

    def one_block(args):
        qb, qib, wib, tq = args
        cq = cid[tq]
        s = jnp.einsum('bqhd,bsd->bqhs', qib.astype(jnp.float32), kif)
        s = jnp.einsum('bqh,bqhs->bqs', wib.astype(jnp.float32), jax.nn.relu(s))
        allowed = (cid[None, :] <= cq[:, None]) & key_ok[None, :]
        s = jnp.where(allowed[None], s, -jnp.inf)
        _, idx = lax.top_k(s, topk)
        valid = (cid[idx] <= cq[None, :, None]) & (idx < n_valid)
        kg = jax.vmap(lambda a, i: a[i])(kf, idx).reshape(B, QBLOCK, topk, H, dh)
        vg = jax.vmap(lambda a, i: a[i])(vf, idx).reshape(B, QBLOCK, topk, H, dh)
        logits = jnp.einsum('bqhd,bqkhd->bqhk', qb, kg).astype(jnp.float32) * (dh ** -0.5)
        dist = jnp.abs(tq[None, :, None] - idx).astype(jnp.float32)
        logits = logits - slopes[None, None, :, None] * dist[:, :, None, :]
        logits = jnp.where(valid[:, :, None, :], logits, -jnp.inf)
        p = jax.nn.softmax(logits, axis=-1).astype(vg.dtype)
        return jnp.einsum('bqhk,bqkhd->bqhd', p, vg)

    out = lax.map(one_block, (to_blocks(q), to_blocks(qi), to_blocks(wi), pos.reshape(nblk, QBLOCK)))
    return out.swapaxes(0, 1).reshape(B, Tp, H, dh)


def gla_chunked(q, k, v, g):
    B, Tp, H, dk = q.shape
    dv = v.shape[-1]
    n = Tp // CHUNK

    def to_chunks(a):
        return a.astype(jnp.float32).reshape(B, n, CHUNK, H, a.shape[-1]).transpose(1, 0, 3, 2, 4)

    causal = jnp.tril(jnp.ones((CHUNK, CHUNK), dtype=bool))

    def step(S, inp):
        qc, kc, vc, gc = inp
        b = jnp.cumsum(gc, axis=2)
        o_inter = jnp.einsum('bhcd,bhde->bhce', qc * jnp.exp(b), S)
        diff = b[:, :, :, None, :] - b[:, :, None, :, :]
        decay = jnp.exp(jnp.where(causal[:, :, None], diff, -jnp.inf))
        A = jnp.einsum('bhid,bhjd,bhijd->bhij', qc, kc, decay)
        o_intra = jnp.einsum('bhij,bhje->bhie', A, vc)
        b_last = b[:, :, -1:, :]
        S = S * jnp.exp(b_last[:, :, 0, :])[..., None] + jnp.einsum('bhcd,bhce->bhde', kc * jnp.exp(b_last - b), vc)
        return S, o_inter + o_intra

    S0 = jnp.zeros((B, H, dk, dv), jnp.float32)
    _, o = lax.scan(step, S0, (to_chunks(q * (dk ** -0.5)), to_chunks(k), to_chunks(v), to_chunks(g)))
    return o.transpose(1, 0, 3, 2, 4).reshape(B, Tp, H, dv)


def conv_ffn(x, w_up, conv_w, conv_b, w_down):
    T = x.shape[1]
    a, gte = jnp.split(x @ w_up, 2, axis=-1)
    ap = jnp.pad(a, ((0, 0), (CONV_W - 1, 0), (0, 0)))
    a = sum(ap[:, j:j + T] * conv_w[j] for j in range(CONV_W)) + conv_b
    return (jax.nn.gelu(a) * gte) @ w_down


def setup_inputs(seed: int = 0) -> dict:
    key = jax.random.key(seed)
    ks = jax.random.split(key, 20)
    nrm = lambda k, shape: jax.random.normal(k, shape, jnp.float32)
    col_scale = jnp.concatenate([jnp.full((n,), BETA if is_v else 1.0, jnp.float32) for _, n, is_v in _col_layout()])
    return {
        'x': nrm(ks[0], (BATCH, SEQ, D_MODEL)),
        'meta': nrm(ks[1], (N_META, D_MODEL)),
        'ln_in_g': 1.0 + 0.01 * nrm(ks[2], (D_MODEL,)),
        'ln_in_b': 0.01 * nrm(ks[3], (D_MODEL,)),
        'w_in': nrm(ks[4], (DEPTH, D_MODEL, IN_COLS)) * (D_MODEL ** -0.5) * col_scale,
        'w_gate_b': nrm(ks[5], (DEPTH, B_GATE_RANK, B_HEADS * B_KEY_DIM)) * (B_GATE_RANK ** -0.5),
        'b_gate_b': 0.01 * nrm(ks[6], (DEPTH, B_HEADS * B_KEY_DIM)),
        'attn_norm_g': 1.0 + 0.01 * nrm(ks[7], (DEPTH, A_HEADS, A_HEAD_DIM)),
        'gla_norm_g': 1.0 + 0.01 * nrm(ks[8], (DEPTH, B_HEADS, B_VAL_DIM)),
        'w_out': nrm(ks[9], (DEPTH, D_MODEL, D_MODEL)) * (D_MODEL ** -0.5) * BETA,
        'ln1_g': 1.0 + 0.01 * nrm(ks[10], (DEPTH, D_MODEL)),
        'ln1_b': 0.01 * nrm(ks[11], (DEPTH, D_MODEL)),
        'w_up': nrm(ks[12], (DEPTH, D_MODEL, 2 * D_FF)) * (D_MODEL ** -0.5) * BETA,
        'conv_w': nrm(ks[13], (DEPTH, CONV_W, D_FF)) * (CONV_W ** -0.5),
        'conv_b': 0.01 * nrm(ks[14], (DEPTH, D_FF)),
        'w_down': nrm(ks[15], (DEPTH, D_FF, D_MODEL)) * (D_FF ** -0.5) * BETA,
        'ln2_g': 1.0 + 0.01 * nrm(ks[16], (DEPTH, D_MODEL)),
        'ln2_b': 0.01 * nrm(ks[17], (DEPTH, D_MODEL)),
    }


def reference(x, meta, ln_in_g, ln_in_b, w_in, w_gate_b, b_gate_b, attn_norm_g, gla_norm_g, w_out,
              ln1_g, ln1_b, w_up, conv_w, conv_b, w_down, ln2_g, ln2_b):
    B, S, D = x.shape
    T = N_META + S
    Tp = -(-T // QBLOCK) * QBLOCK
    topk = min(TOPK_MAX, S // 4)
    offs = _split_offsets()
    h = jnp.concatenate([jnp.broadcast_to(meta[None].astype(x.dtype), (B, N_META, D)), x], axis=1)
    h = layer_norm(h, ln_in_g, ln_in_b)
    for l in range(DEPTH):
        proj = jnp.pad(h @ w_in[l], ((0, 0), (0, Tp - T), (0, 0)))
        a_q, a_k, a_v, i_q, i_k, i_w, b_q, b_k, b_v, b_lr, b_og = jnp.split(proj, offs, axis=-1)
        a_out = dsa_attention(a_q.reshape(B, Tp, A_HEADS, A_HEAD_DIM),
                              a_k.reshape(B, Tp, A_HEADS, A_HEAD_DIM),
                              a_v.reshape(B, Tp, A_HEADS, A_HEAD_DIM),
                              i_q.reshape(B, Tp, IDX_HEADS, IDX_DIM), i_k,
                              i_w * (IDX_HEADS ** -0.5) * (IDX_DIM ** -0.5), T, topk)
        a_out = head_rms_norm(a_out, attn_norm_g[l]).reshape(B, Tp, A_WIDTH)
        log_g = jax.nn.log_sigmoid((b_lr @ w_gate_b[l] + b_gate_b[l]).astype(jnp.float32)) / B_GATE_TAU
        b_out = gla_chunked(b_q.reshape(B, Tp, B_HEADS, B_KEY_DIM), b_k.reshape(B, Tp, B_HEADS, B_KEY_DIM),
                            b_v.reshape(B, Tp, B_HEADS, B_VAL_DIM), log_g.reshape(B, Tp, B_HEADS, B_KEY_DIM))
        b_out = head_rms_norm(b_out.astype(x.dtype), gla_norm_g[l]).reshape(B, Tp, B_WIDTH) * jax.nn.silu(b_og)
        mixed = jnp.concatenate([a_out, b_out], axis=-1)[:, :T]
        h = layer_norm(ALPHA * h + mixed @ w_out[l], ln1_g[l], ln1_b[l])
        h = layer_norm(ALPHA * h + conv_ffn(h, w_up[l], conv_w[l], conv_b[l], w_down[l]), ln2_g[l], ln2_b[l])
    return h[:, N_META:]
```

```python
import functools

import jax
import jax.numpy as jnp
from jax import lax
from jax.experimental import pallas as pl
from jax.experimental.pallas import tpu as pltpu

F32 = jnp.float32
BF16 = jnp.bfloat16
I32 = jnp.int32

D_MODEL = 1024
CHUNK = 64
N_META = 16
A_HEADS = 8
A_WIDTH = 512
A_HEAD_DIM = 64
IDX_HEADS = 4
IDX_DIM = 64
TOPK_MAX = 256
B_HEADS = 4
B_WIDTH = 512
B_VAL_DIM = 128
B_KEY_DIM = 64
B_GATE_RANK = 16
B_GATE_TAU = 16.0
D_FF = 2816
CONV_W = 3
LN_EPS = 1e-5
RMS_EPS = 1e-6
DEPTH = 1
ALPHA = (2.0 * DEPTH) ** 0.25

LANES = 128
ROW_TILE = 512
QB = 256
KT = 256
MT = 64
META_ROWS = 256
GLA_BLOCK = 256
FF_TILE = 256
HALO = 16
INT_MIN = -(2 ** 31)
VMEM_LIMIT = 56 * 1024 * 1024

_COLS = {}
_off = 0
for _name, _w in (("aq", 512), ("ak", 512), ("av", 512), ("iq", 256), ("bq", 256), ("bk", 256),
                  ("bv", 512), ("bog", 512), ("ik2", 128), ("misc", 128)):
    _COLS[_name] = (_off, _w)
    _off += _w
PROJ_COLS = _off
MISC_LR = 0
MISC_IW = 16


def _layer_norm(x, g, b):
    mu = jnp.mean(x, axis=-1, keepdims=True)
    xc = x - mu
    var = jnp.mean(xc * xc, axis=-1, keepdims=True)
    return xc * lax.rsqrt(var + LN_EPS) * g + b


def _dot(a, b):
    return jnp.dot(a, b, preferred_element_type=F32)


def _dot_nt(a, b):
    return lax.dot_general(a, b, (((1,), (1,)), ((), ())), preferred_element_type=F32)


def _dot_tn(a, b):
    return lax.dot_general(a, b, (((0,), (0,)), ((), ())), preferred_element_type=F32)


def _inproj_kernel(x_ref, g_ref, b_ref, w_ref, aq_ref, ak_ref, avT_ref, iq_ref, bq_ref, bk_ref,
                   bv_ref, bog_ref, ik2_ref, misc_ref, miscT_ref):
    h = _layer_norm(x_ref[0], g_ref[...], b_ref[...])
    hb = h.astype(BF16)

    def proj(name):
        lo, w = _COLS[name]
        return _dot(hb, w_ref[:, lo:lo + w])

    aq_ref[0] = proj("aq").astype(BF16)
    ak_ref[0] = proj("ak").astype(BF16)
    avT = proj("av").T
    for i in range(avT_ref.shape[1]):
        avT_ref[0, i] = avT[:, i * KT:(i + 1) * KT].astype(BF16)
    iq_ref[0] = proj("iq").astype(BF16)
    bq_ref[0] = proj("bq").astype(BF16)
    bk_ref[0] = proj("bk").astype(BF16)
    bv_ref[0] = proj("bv").astype(BF16)
    bog_ref[0] = proj("bog")
    ik2_ref[0] = proj("ik2").astype(BF16)
    misc = proj("misc")
    misc_ref[0] = misc
    miscT_ref[0] = misc.T


def _inproj(x, ln_g, ln_b, w_perm, tm):
    nb, s, d = x.shape
    nt = s // tm
    row = lambda c: pl.BlockSpec((1, tm, c), lambda b, i: (b, i, 0))
    const = lambda shp: pl.BlockSpec(shp, lambda b, i: (0,) * len(shp))
    out_shape = (
        jax.ShapeDtypeStruct((nb, s, 512), BF16),
        jax.ShapeDtypeStruct((nb, s, 512), BF16),
        jax.ShapeDtypeStruct((nb, s // KT, 512, KT), BF16),
        jax.ShapeDtypeStruct((nb, s, 256), BF16),
        jax.ShapeDtypeStruct((nb, s, 256), BF16),
        jax.ShapeDtypeStruct((nb, s, 256), BF16),
        jax.ShapeDtypeStruct((nb, s, 512), BF16),
        jax.ShapeDtypeStruct((nb, s, 512), F32),
        jax.ShapeDtypeStruct((nb, s, 128), BF16),
        jax.ShapeDtypeStruct((nb, s, 128), F32),
        jax.ShapeDtypeStruct((nb, 128, s), F32),
    )
    out_specs = (
        row(512), row(512),
        pl.BlockSpec((1, tm // KT, 512, KT), lambda b, i: (b, i, 0, 0)),
        row(256), row(256), row(256), row(512), row(512), row(128), row(128),
        pl.BlockSpec((1, 128, tm), lambda b, i: (b, 0, i)),
    )
    return pl.pallas_call(
        _inproj_kernel,
        grid=(nb, nt),
        in_specs=[row(d), const((1, d)), const((1, d)), const((d, PROJ_COLS))],
        out_specs=out_specs,
        out_shape=out_shape,
        compiler_params=pltpu.CompilerParams(
            dimension_semantics=("arbitrary", "arbitrary"), vmem_limit_bytes=VMEM_LIMIT),
        name="inproj",
    )(x, ln_g, ln_b, w_perm)


def _dsa_kernel(aq_ref, iq_ref, miscT_ref, ak_ref, avT_ref, ik2_ref,
                maq_ref, miq_ref, mmiscT_ref, mak_ref, mavT_ref, mik2_ref, ng_ref,
                out_ref, mout_ref,
                keys_s, qm_s, iqm_s, iw_s, m_s, l_s, acc_s, *, topk):
    g = pl.program_id(1)
    is_meta = g == 0
    n_real = g

    lane_half = lax.broadcasted_iota(I32, (1, LANES), 1) // A_HEAD_DIM

    def stage(aq, iq, miscT):
        for h in range(A_HEADS):
            p = h // 2
            qp = aq[:, p * LANES:(p + 1) * LANES]
            qm_s[h] = jnp.where(lane_half == (h % 2), qp, jnp.zeros_like(qp))
        for h in range(IDX_HEADS):
            p = h // 2
            qp = iq[:, p * LANES:(p + 1) * LANES]
            iqm_s[h] = jnp.where(lane_half == (h % 2), qp, jnp.zeros_like(qp))
        iw_s[...] = miscT[MISC_IW:MISC_IW + 8, :]

    @pl.when(is_meta)
    def _():
        stage(maq_ref[...], miq_ref[...], mmiscT_ref[...])

    @pl.when(g > 0)
    def _():
        stage(aq_ref[0], iq_ref[0], miscT_ref[0])

    lane = lax.broadcasted_iota(I32, (1, QB), 1)
    t_real = (g - 1) * QB + lane
    qpos = jnp.where(is_meta, lane, N_META + t_real)
    qcid = jnp.where(is_meta, 0, 1 + (t_real >> 6))

    def keys_for_tile(ik2_t, allowed):
        s = None
        for h in range(IDX_HEADS):
            sh = _dot_nt(ik2_t, iqm_s[h])
            term = iw_s[h:h + 1, :] * jnp.maximum(sh, 0.0)
            s = term if s is None else s + term
        s = jnp.where(s == 0.0, 0.0, s)
        bits = lax.bitcast_convert_type(s, I32)
        key = bits ^ ((bits >> 31) & 0x7FFFFFFF)
        return jnp.where(allowed, key, INT_MIN)

    mrow = lax.broadcasted_iota(I32, (MT, QB), 0)
    keys_s[0, 0:MT, :] = keys_for_tile(mik2_ref[0:MT, :], mrow < N_META)

    krow = lax.broadcasted_iota(I32, (KT, QB), 0)

    def fill_body(kt, c):
        kcid = 1 + ((kt * KT + krow) >> 6)
        keys_s[kt + 1] = keys_for_tile(ik2_ref[0, kt], kcid <= qcid)
        return c

    lax.fori_loop(0, n_real, fill_body, 0)

    def count(pred):
        def red(tile):
            x = jnp.where(pred(tile), 1, 0).astype(I32)
            return jnp.sum(x.reshape(tile.shape[0] // 8, 8, QB), axis=0)

        acc = red(keys_s[0, 0:MT, :])
        acc = lax.fori_loop(0, n_real, lambda kt, a: a + red(keys_s[kt + 1]), acc)
        return jnp.sum(acc, axis=0, keepdims=True)

    def bit_body(i, ut):
        cand_u = ut | jnp.left_shift(jnp.int32(1), 31 - i)
        cand = cand_u ^ INT_MIN
        cnt = count(lambda tile: tile >= cand)
        return jnp.where(cnt >= topk, cand_u, ut)

    ut = lax.fori_loop(0, 32, bit_body, jnp.zeros((1, QB), I32))
    thr = ut ^ INT_MIN
    n_gt = count(lambda tile: tile > thr)
    need = jnp.where(thr == INT_MIN, 0, topk - n_gt).astype(F32)

    m_s[...] = jnp.full(m_s.shape, -jnp.inf, F32)
    l_s[...] = jnp.zeros(l_s.shape, F32)
    acc_s[...] = jnp.zeros(acc_s.shape, F32)

    def attend(keys_t, k_t, vT_t, kpos, n_eq_before):
        r = keys_t.shape[0]
        ri = lax.broadcasted_iota(I32, (r, r), 0)
        ci = lax.broadcasted_iota(I32, (r, r), 1)
        lower = jnp.where(ri > ci, 1.0, 0.0).astype(BF16)
        eq = keys_t == thr
        eqf = jnp.where(eq, 1.0, 0.0)
        rank = _dot(lower, eqf.astype(BF16)) + n_eq_before
        ninf = jnp.float32(-jnp.inf)
        bias = jnp.where(keys_t > thr, 0.0, jnp.where(eq, jnp.where(rank < need, 0.0, ninf), ninf))
        dist = jnp.abs(qpos - kpos).astype(F32)
        for h in range(A_HEADS):
            p = h // 2
            slope = 2.0 ** (-8.0 * (h + 1) / A_HEADS)
            lg = _dot_nt(k_t[:, p * LANES:(p + 1) * LANES], qm_s[h]) - slope * dist + bias
            m_old = m_s[h]
            m_new = jnp.maximum(m_old, jnp.max(lg, axis=0, keepdims=True))
            m_safe = jnp.where(m_new == ninf, 0.0, m_new)
            pr = jnp.exp(lg - m_safe)
            corr = jnp.exp(m_old - m_safe)
            l_s[h] = l_s[h] * corr + jnp.sum(pr, axis=0, keepdims=True)
            acc_s[h] = acc_s[h] * corr + _dot(vT_t[h * A_HEAD_DIM:(h + 1) * A_HEAD_DIM, :], pr.astype(BF16))
            m_s[h] = m_new
        return n_eq_before + jnp.sum(eqf, axis=0, keepdims=True)

    n_eq = attend(keys_s[0, 0:MT, :], mak_ref[0:MT, :], mavT_ref[0, :, 0:MT], mrow,
                  jnp.zeros((1, QB), F32))

    def attend_body(kt, n_eq_c):
        kpos = N_META + kt * KT + krow
        return attend(keys_s[kt + 1], ak_ref[0, kt], avT_ref[0, kt], kpos, n_eq_c)

    lax.fori_loop(0, n_real, attend_body, n_eq)

    outs = []
    for h in range(A_HEADS):
        o = acc_s[h] / l_s[h]
        ms = jnp.mean(o * o, axis=0, keepdims=True)
        outs.append(o * lax.rsqrt(ms + RMS_EPS))
    res = (jnp.concatenate(outs, axis=0).T * ng_ref[...]).astype(BF16)

    @pl.when(is_meta)
    def _():
        mout_ref[...] = res

    @pl.when(g > 0)
    def _():
        out_ref[0] = res


def _dsa(aq, iq, miscT, ak, avT, ik2, m, ng, topk):
    nb, s, _ = aq.shape
    nq = s // QB
    nkt = s // KT
    qidx = lambda b, g: (b, jnp.maximum(g - 1, 0), 0)
    const = lambda shp: pl.BlockSpec(shp, lambda b, g: (0,) * len(shp))
    in_specs = [
        pl.BlockSpec((1, QB, 512), qidx),
        pl.BlockSpec((1, QB, 256), qidx),
        pl.BlockSpec((1, 128, QB), lambda b, g: (b, 0, jnp.maximum(g - 1, 0))),
        pl.BlockSpec((1, nkt, KT, 512), lambda b, g: (b, 0, 0, 0)),
        pl.BlockSpec((1, nkt, 512, KT), lambda b, g: (b, 0, 0, 0)),
        pl.BlockSpec((1, nkt, KT, 128), lambda b, g: (b, 0, 0, 0)),
        const((META_ROWS, 512)), const((META_ROWS, 256)), const((128, META_ROWS)),
        const((META_ROWS, 512)), const((1, 512, KT)), const((META_ROWS, 128)),
        const((1, 512)),
    ]
    out_shape = (jax.ShapeDtypeStruct((nb, s, 512), BF16), jax.ShapeDtypeStruct((META_ROWS, 512), BF16))
    out_specs = (pl.BlockSpec((1, QB, 512), qidx), const((META_ROWS, 512)))
    scratch = [
        pltpu.VMEM((nkt + 1, KT, QB), I32),
        pltpu.VMEM((A_HEADS, QB, LANES), BF16),
        pltpu.VMEM((IDX_HEADS, QB, LANES), BF16),
        pltpu.VMEM((8, QB), F32),
        pltpu.VMEM((A_HEADS, 1, QB), F32),
        pltpu.VMEM((A_HEADS, 1, QB), F32),
        pltpu.VMEM((A_HEADS, A_HEAD_DIM, QB), F32),
    ]
    return pl.pallas_call(
        functools.partial(_dsa_kernel, topk=topk),
        grid=(nb, nq + 1),
        in_specs=in_specs,
        out_specs=out_specs,
        out_shape=out_shape,
        scratch_shapes=scratch,
        compiler_params=pltpu.CompilerParams(
            dimension_semantics=("arbitrary", "arbitrary"), vmem_limit_bytes=VMEM_LIMIT),
        name="dsa",
    )(aq, iq, miscT, ak.reshape(nb, nkt, KT, 512), avT, ik2.reshape(nb, nkt, KT, 128),
      m["aq"], m["iq"], m["miscT"], m["ak"], m["avT"], m["ik2"], ng)


def _cumsum_rows(x):
    n = x.shape[0]
    row = lax.broadcasted_iota(I32, x.shape, 0)
    sh = 1
    while sh < n:
        x = x + jnp.where(row >= sh, pltpu.roll(x, sh, 0), 0.0)
        sh *= 2
    return x


def _gla_chunk(q, k, v, lr, og, wg, bg, ng, st_s, valid_rows):
    z = _dot(lr.astype(BF16), wg) + bg
    logg = (jnp.minimum(z, 0.0) - jnp.log1p(jnp.exp(-jnp.abs(z)))) / B_GATE_TAU
    if valid_rows is not None:
        rows = lax.broadcasted_iota(I32, logg.shape, 0)
        logg = jnp.where(rows < valid_rows, logg, 0.0)
    b = _cumsum_rows(logg)
    b_mid = b[CHUNK // 2 - 1:CHUNK // 2, :]
    b_last = b[CHUNK - 1:CHUNK, :]
    q_in = q * jnp.exp(b)
    q_ic = q * jnp.exp(b - b_mid)
    k_ic = k * jnp.exp(b_mid - b)
    k_st = k * jnp.exp(b_last - b)
    decay = jnp.exp(b_last)

    lane_half = lax.broadcasted_iota(I32, (1, LANES), 1) // B_KEY_DIM
    ri = lax.broadcasted_iota(I32, (CHUNK, CHUNK), 0)
    ci = lax.broadcasted_iota(I32, (CHUNK, CHUNK), 1)
    causal = ci <= ri
    outs = []
    for p in range(B_HEADS // 2):
        sl = slice(p * LANES, (p + 1) * LANES)
        st_old = st_s[p]
        st_new = st_old * decay[:, sl]
        k_ic_p = k_ic[:, sl].astype(BF16)
        for hh in range(2):
            h = 2 * p + hh
            hm = lane_half == hh
            v_h = v[:, h * B_VAL_DIM:(h + 1) * B_VAL_DIM]
            a = _dot_nt(jnp.where(hm, q_ic[:, sl], 0.0).astype(BF16), k_ic_p)
            a = jnp.where(causal, a, 0.0)
            o = _dot_nt(jnp.where(hm, q_in[:, sl], 0.0).astype(BF16), st_old.astype(BF16))
            o = o + _dot(a.astype(BF16), v_h)
            st_new = st_new + _dot_tn(v_h, jnp.where(hm, k_st[:, sl], 0.0).astype(BF16))
            ms = jnp.mean(o * o, axis=-1, keepdims=True)
            on = o * lax.rsqrt(ms + RMS_EPS) * ng[:, h * B_VAL_DIM:(h + 1) * B_VAL_DIM]
            og_h = og[:, h * B_VAL_DIM:(h + 1) * B_VAL_DIM]
            outs.append(on * (og_h * jax.nn.sigmoid(og_h)))
        st_s[p] = st_new
    return jnp.concatenate(outs, axis=-1)


def _gla_kernel(bq_ref, bk_ref, bv_ref, misc_ref, bog_ref,
                mbq_ref, mbk_ref, mbv_ref, mmisc_ref, mbog_ref,
                wg_ref, bg_ref, ng_ref, out_ref, mout_ref, st_s):
    j = pl.program_id(1)
    wg = wg_ref[...]
    bg = bg_ref[...]
    ng = ng_ref[...]

    @pl.when(j == 0)
    def _():
        st_s[...] = jnp.zeros(st_s.shape, F32)
        o = _gla_chunk(mbq_ref[0:CHUNK, :].astype(F32), mbk_ref[0:CHUNK, :].astype(F32),
                       mbv_ref[0:CHUNK, :], mmisc_ref[0:CHUNK, MISC_LR:MISC_LR + B_GATE_RANK],
                       mbog_ref[0:CHUNK, :], wg, bg, ng, st_s, N_META)
        mout_ref[...] = o.astype(BF16)

    for c in range(GLA_BLOCK // CHUNK):
        rs = slice(c * CHUNK, (c + 1) * CHUNK)
        o = _gla_chunk(bq_ref[0, rs, :].astype(F32), bk_ref[0, rs, :].astype(F32), bv_ref[0, rs, :],
                       misc_ref[0, rs, MISC_LR:MISC_LR + B_GATE_RANK], bog_ref[0, rs, :],
                       wg, bg, ng, st_s, None)
        out_ref[0, rs, :] = o.astype(BF16)


def _gla(bq, bk, bv, misc, bog, m, wg, bg, ng):
    nb, s, _ = bq.shape
    row = lambda c: pl.BlockSpec((1, GLA_BLOCK, c), lambda b, j: (b, j, 0))
    const = lambda shp: pl.BlockSpec(shp, lambda b, j: (0,) * len(shp))
    return pl.pallas_call(
        _gla_kernel,
        grid=(nb, s // GLA_BLOCK),
        in_specs=[row(256), row(256), row(512), row(128), row(512),
                  const((META_ROWS, 256)), const((META_ROWS, 256)), const((META_ROWS, 512)),
                  const((META_ROWS, 128)), const((META_ROWS, 512)),
                  const((B_GATE_RANK, 256)), const((1, 256)), const((1, 512))],
        out_specs=(row(512), const((CHUNK, 512))),
        out_shape=(jax.ShapeDtypeStruct((nb, s, 512), BF16), jax.ShapeDtypeStruct((CHUNK, 512), BF16)),
        scratch_shapes=[pltpu.VMEM((B_HEADS // 2, LANES, LANES), F32)],
        compiler_params=pltpu.CompilerParams(
            dimension_semantics=("arbitrary", "arbitrary"), vmem_limit_bytes=VMEM_LIMIT),
        name="gla",
    )(bq, bk, bv, misc, bog, m["bq"], m["bk"], m["bv"], m["misc"], m["bog"], wg, bg, ng)


def _outproj_kernel(x_ref, a_ref, b_ref, lng_ref, lnb_ref, wo_ref, g1_ref, b1_ref, out_ref):
    h = _layer_norm(x_ref[0], lng_ref[...], lnb_ref[...])
    mixed = _dot(a_ref[0], wo_ref[0:A_WIDTH, :]) + _dot(b_ref[0], wo_ref[A_WIDTH:, :])
    out_ref[0] = _layer_norm(ALPHA * h + mixed, g1_ref[...], b1_ref[...])


def _outproj(x, a, b, lng, lnb, wo, g1, b1, tm):
    nb, s, d = x.shape
    row = lambda c: pl.BlockSpec((1, tm, c), lambda bb, i: (bb, i, 0))
    const = lambda shp: pl.BlockSpec(shp, lambda bb, i: (0,) * len(shp))
    return pl.pallas_call(
        _outproj_kernel,
        grid=(nb, s // tm),
        in_specs=[row(d), row(512), row(512), const((1, d)), const((1, d)), const((d, d)),
                  const((1, d)), const((1, d))],
        out_specs=row(d),
        out_shape=jax.ShapeDtypeStruct((nb, s, d), F32),
        compiler_params=pltpu.CompilerParams(
            dimension_semantics=("arbitrary", "arbitrary"), vmem_limit_bytes=VMEM_LIMIT),
        name="outproj",
    )(x, a, b, lng, lnb, wo, g1, b1)


def _gelu_tanh(x):
    c = 0.7978845608028654
    return 0.5 * x * (1.0 + jnp.tanh(c * (x + 0.044715 * (x * x * x))))


def _ffn_kernel(h_ref, halo_ref, hm_ref, wup_ref, cw_ref, cb_ref, wdn_ref, g2_ref, b2_ref, out_ref, y_s):
    j = pl.program_id(1)
    h = h_ref[0]
    tm = h.shape[0]
    halo = jnp.where(j == 0, hm_ref[...], halo_ref[0])
    hb = jnp.concatenate([halo, h], axis=0).astype(BF16)
    for f in range(D_FF // FF_TILE):
        fs = slice(f * FF_TILE, (f + 1) * FF_TILE)
        a = _dot(hb, wup_ref[:, fs])
        gate = _dot(hb[HALO:, :], wup_ref[:, D_FF + f * FF_TILE:D_FF + (f + 1) * FF_TILE])
        cw = cw_ref[:, fs]
        conv = cb_ref[:, fs] + cw[CONV_W - 1:CONV_W, :] * a[HALO:, :]
        for back in range(1, CONV_W):
            tap = cw[CONV_W - 1 - back:CONV_W - back, :]
            conv = conv + tap * pltpu.roll(a, back, 0)[HALO:, :]
        y_s[:, fs] = (_gelu_tanh(conv) * gate).astype(BF16)
    ffn = _dot(y_s[...], wdn_ref[...])
    out_ref[0] = _layer_norm(ALPHA * h + ffn, g2_ref[...], b2_ref[...])


def _ffn(h1, h1_meta, wup, cw, cb, wdn, g2, b2, tm):
    nb, s, d = h1.shape
    per = tm // HALO
    const = lambda shp: pl.BlockSpec(shp, lambda b, j: (0,) * len(shp))
    return pl.pallas_call(
        _ffn_kernel,
        grid=(nb, s // tm),
        in_specs=[pl.BlockSpec((1, tm, d), lambda b, j: (b, j, 0)),
                  pl.BlockSpec((1, HALO, d), lambda b, j: (b, jnp.maximum(j * per - 1, 0), 0)),
                  const((HALO, d)), const((d, 2 * D_FF)), const((CONV_W, D_FF)), const((1, D_FF)),
                  const((D_FF, d)), const((1, d)), const((1, d))],
        out_specs=pl.BlockSpec((1, tm, d), lambda b, j: (b, j, 0)),
        out_shape=jax.ShapeDtypeStruct((nb, s, d), F32),
        scratch_shapes=[pltpu.VMEM((tm, D_FF), BF16)],
        compiler_params=pltpu.CompilerParams(
            dimension_semantics=("arbitrary", "arbitrary"), vmem_limit_bytes=VMEM_LIMIT),
        name="ffn",
    )(h1, h1, h1_meta, wup, cw, cb, wdn, g2, b2)


def _permute_w_in(w):
    o = 0
    parts = {}
    for name, n in (("a_q", 512), ("a_k", 512), ("a_v", 512), ("i_q", 256), ("i_k", 64), ("i_w", 4),
                    ("b_q", 256), ("b_k", 256), ("b_v", 512), ("b_lr", 16), ("b_og", 512)):
        parts[name] = w[:, o:o + n]
        o += n
    zeros = jnp.zeros((w.shape[0], 128 - B_GATE_RANK - IDX_HEADS), w.dtype)
    cols = [parts["a_q"] * (A_HEAD_DIM ** -0.5), parts["a_k"], parts["a_v"], parts["i_q"],
            parts["b_q"] * (B_KEY_DIM ** -0.5), parts["b_k"], parts["b_v"], parts["b_og"],
            parts["i_k"], parts["i_k"],
            parts["b_lr"], parts["i_w"] * (IDX_HEADS ** -0.5) * (IDX_DIM ** -0.5), zeros]
    return jnp.concatenate(cols, axis=1).astype(BF16)


def kernel(x, meta, ln_in_g, ln_in_b, w_in, w_gate_b, b_gate_b, attn_norm_g, gla_norm_g, w_out,
           ln1_g, ln1_b, w_up, conv_w, conv_b, w_down, ln2_g, ln2_b):
    nb, s, d = x.shape
    assert d == D_MODEL and w_in.shape[0] == DEPTH == 1
    assert s % ROW_TILE == 0 and s % KT == 0 and s % GLA_BLOCK == 0
    topk = min(TOPK_MAX, s // 4)
    l = 0
    r2 = lambda v: v.reshape(1, -1)
    lng, lnb = r2(ln_in_g), r2(ln_in_b)
    w_perm = _permute_w_in(w_in[l])
    wo = w_out[l].astype(BF16)
    wup = w_up[l].astype(BF16)
    wdn = w_down[l].astype(BF16)
    wg = w_gate_b[l].astype(BF16)
    bg = r2(b_gate_b[l])
    ng_a = r2(attn_norm_g[l])
    ng_b = r2(gla_norm_g[l])

    x_meta = jnp.zeros((1, META_ROWS, d), x.dtype).at[0, :N_META].set(meta.astype(x.dtype))
    names = ("aq", "ak", "avT", "iq", "bq", "bk", "bv", "bog", "ik2", "misc", "miscT")
    mo = dict(zip(names, _inproj(x_meta, lng, lnb, w_perm, META_ROWS)))
    keep = (jnp.arange(META_ROWS) < N_META)
    m = {}
    for n in names:
        v = mo[n][0]
        if n in ("avT", "miscT"):
            m[n] = jnp.where(keep[None, :], v, jnp.zeros_like(v)).reshape((1,) * (n == "avT") + v.shape[-2:])
        else:
            m[n] = jnp.where(keep[:, None], v, jnp.zeros_like(v))

    aq, ak, avT, iq, bq, bk, bv, bog, ik2, misc, miscT = _inproj(x, lng, lnb, w_perm, ROW_TILE)
    a_out, a_out_m = _dsa(aq, iq, miscT, ak, avT, ik2, m, ng_a, topk)
    b_out, b_out_m = _gla(bq, bk, bv, misc, bog, m, wg, bg, ng_b)

    h1 = _outproj(x, a_out, b_out, lng, lnb, wo, r2(ln1_g[l]), r2(ln1_b[l]), ROW_TILE)
    b_out_m = jnp.zeros((META_ROWS, B_WIDTH), BF16).at[:CHUNK].set(b_out_m)
    h1_m = _outproj(x_meta, a_out_m[None], b_out_m[None], lng, lnb, wo, r2(ln1_g[l]), r2(ln1_b[l]), META_ROWS)
    return _ffn(h1, h1_m[0, :HALO], wup, conv_w[l], r2(conv_b[l]), wdn, r2(ln2_g[l]), r2(ln2_b[l]), ROW_TILE)
```

```python
import functools

import jax
import jax.numpy as jnp
from jax import lax
from jax.experimental import pallas as pl
from jax.experimental.pallas import tpu as pltpu

F32 = jnp.float32
BF16 = jnp.bfloat16
I32 = jnp.int32

D_MODEL = 1024
CHUNK = 64
N_META = 16
A_HEADS = 8
A_WIDTH = 512
A_HEAD_DIM = 64
IDX_HEADS = 4
IDX_DIM = 64
TOPK_MAX = 256
B_HEADS = 4
B_WIDTH = 512
B_VAL_DIM = 128
B_KEY_DIM = 64
B_GATE_RANK = 16
B_GATE_TAU = 16.0
D_FF = 2816
CONV_W = 3
LN_EPS = 1e-5
RMS_EPS = 1e-6
DEPTH = 1
ALPHA = (2.0 * DEPTH) ** 0.25

LANES = 128
ROW_TILE = 512
QB = 256
KT = 256
MT = 64
META_ROWS = 256
GLA_BLOCK = 256
FF_TILE = 256
HALO = 16
INT_MIN = -(2 ** 31)
I16_MIN = -(2 ** 15)
N_SLOPE_PARTS = 3
VT_ROWS = 80
KX_COLS = 2 * A_WIDTH
LOG2E = 1.4426950408889634
ALIBI_C = tuple(2.0 ** (-8.0 * (h + 1) / A_HEADS) * LOG2E for h in range(A_HEADS))
VMEM_LIMIT = 56 * 1024 * 1024

_COLS = {}
_off = 0
for _name, _w in (("aq", 512), ("ak", 512), ("av", 512), ("iq", 256), ("bq", 256), ("bk", 256),
                  ("bv", 512), ("bog", 512), ("ik2", 128), ("misc", 128)):
    _COLS[_name] = (_off, _w)
    _off += _w
PROJ_COLS = _off
MISC_LR = 0
MISC_IW = 16


def _layer_norm(x, g, b):
    mu = jnp.mean(x, axis=-1, keepdims=True)
    xc = x - mu
    var = jnp.mean(xc * xc, axis=-1, keepdims=True)
    return xc * lax.rsqrt(var + LN_EPS) * g + b


def _dot(a, b):
    return jnp.dot(a, b, preferred_element_type=F32)


def _dot_nt(a, b):
    return lax.dot_general(a, b, (((1,), (1,)), ((), ())), preferred_element_type=F32)


def _dot_tn(a, b):
    return lax.dot_general(a, b, (((0,), (0,)), ((), ())), preferred_element_type=F32)


def _inproj_kernel(x_ref, g_ref, b_ref, w_ref, aq_ref, akx_ref, avT_ref, iq_ref, bq_ref, bk_ref,
                   bv_ref, bog_ref, ik2_ref, misc_ref, miscT_ref, *, pos0):
    h = _layer_norm(x_ref[0], g_ref[...], b_ref[...])
    hb = h.astype(BF16)
    tm = hb.shape[0]

    def proj(name):
        lo, w = _COLS[name]
        return _dot(hb, w_ref[:, lo:lo + w])

    aq_ref[0] = proj("aq").astype(BF16)
    ak = proj("ak")
    rows = lax.broadcasted_iota(I32, (tm, LANES), 0)
    lanes = lax.broadcasted_iota(I32, (tm, LANES), 1)
    pos = pos0 + pl.program_id(1) * tm + rows
    feat = jnp.where(lanes < 2 * N_SLOPE_PARTS, jnp.where(lanes % 2 == 0, pos >> 6, pos & 63), 0)
    feat = feat.astype(F32).astype(BF16)
    for p in range(A_HEADS // 2):
        akx_ref[0, :, 2 * p * LANES:(2 * p + 1) * LANES] = ak[:, p * LANES:(p + 1) * LANES].astype(BF16)
        akx_ref[0, :, (2 * p + 1) * LANES:(2 * p + 2) * LANES] = feat
    avT = proj("av").T
    ones = jnp.ones((VT_ROWS - A_HEAD_DIM, KT), BF16)
    for i in range(avT_ref.shape[1]):
        for hd in range(A_HEADS):
            avT_ref[0, i, hd * VT_ROWS:hd * VT_ROWS + A_HEAD_DIM, :] = (
                avT[hd * A_HEAD_DIM:(hd + 1) * A_HEAD_DIM, i * KT:(i + 1) * KT].astype(BF16))
            avT_ref[0, i, hd * VT_ROWS + A_HEAD_DIM:(hd + 1) * VT_ROWS, :] = ones
    iq_ref[0] = proj("iq").astype(BF16)
    bq_ref[0] = proj("bq").astype(BF16)
    bk_ref[0] = proj("bk").astype(BF16)
    bv_ref[0] = proj("bv").astype(BF16)
    bog_ref[0] = proj("bog")
    ik2_ref[0] = proj("ik2").astype(BF16)
    misc = proj("misc")
    misc_ref[0] = misc
    miscT_ref[0] = misc.T


def _inproj(x, ln_g, ln_b, w_perm, tm, pos0):
    nb, s, d = x.shape
    nt = s // tm
    row = lambda c: pl.BlockSpec((1, tm, c), lambda b, i: (b, i, 0))
    const = lambda shp: pl.BlockSpec(shp, lambda b, i: (0,) * len(shp))
    out_shape = (
        jax.ShapeDtypeStruct((nb, s, 512), BF16),
        jax.ShapeDtypeStruct((nb, s, KX_COLS), BF16),
        jax.ShapeDtypeStruct((nb, s // KT, A_HEADS * VT_ROWS, KT), BF16),
        jax.ShapeDtypeStruct((nb, s, 256), BF16),
        jax.ShapeDtypeStruct((nb, s, 256), BF16),
        jax.ShapeDtypeStruct((nb, s, 256), BF16),
        jax.ShapeDtypeStruct((nb, s, 512), BF16),
        jax.ShapeDtypeStruct((nb, s, 512), F32),
        jax.ShapeDtypeStruct((nb, s, 128), BF16),
        jax.ShapeDtypeStruct((nb, s, 128), F32),
        jax.ShapeDtypeStruct((nb, 128, s), F32),
    )
    out_specs = (
        row(512), row(KX_COLS),
        pl.BlockSpec((1, tm // KT, A_HEADS * VT_ROWS, KT), lambda b, i: (b, i, 0, 0)),
        row(256), row(256), row(256), row(512), row(512), row(128), row(128),
        pl.BlockSpec((1, 128, tm), lambda b, i: (b, 0, i)),
    )
    return pl.pallas_call(
        functools.partial(_inproj_kernel, pos0=pos0),
        grid=(nb, nt),
        in_specs=[row(d), const((1, d)), const((1, d)), const((d, PROJ_COLS))],
        out_specs=out_specs,
        out_shape=out_shape,
        compiler_params=pltpu.CompilerParams(
            dimension_semantics=("arbitrary", "arbitrary"), vmem_limit_bytes=VMEM_LIMIT),
        name="inproj",
    )(x, ln_g, ln_b, w_perm)


def _dsa_kernel(aq_ref, iq_ref, miscT_ref, akx_ref, avT_ref, ik2_ref,
                maq_ref, miq_ref, mmiscT_ref, makx_ref, mavT_ref, mik2_ref, qfeat_ref, ng_ref,
                out_ref, mout_ref,
                keys_s, khi_s, klo_s, qm_s, iqm_s, iw_s, m_s, acc_s, bias_s, lg_s, p_s, *, topk):
    g = pl.program_id(1)
    is_meta = g == 0
    n_past = jnp.maximum(g - 1, 0)

    lane_half = lax.broadcasted_iota(I32, (1, LANES), 1) // A_HEAD_DIM

    def stage(aq, iq, miscT):
        for h in range(A_HEADS):
            p = h // 2
            qp = aq[:, p * LANES:(p + 1) * LANES]
            qm_s[h, :, 0:LANES] = jnp.where(lane_half == (h % 2), qp, jnp.zeros_like(qp))
            qm_s[h, :, LANES:2 * LANES] = jnp.broadcast_to(qfeat_ref[h:h + 1, :], (QB, LANES)).astype(BF16)
        for h in range(IDX_HEADS):
            p = h // 2
            qp = iq[:, p * LANES:(p + 1) * LANES]
            iqm_s[h] = jnp.where(lane_half == (h % 2), qp, jnp.zeros_like(qp))
        iw_s[...] = miscT[MISC_IW:MISC_IW + 8, :]

    @pl.when(is_meta)
    def _():
        stage(maq_ref[...], miq_ref[...], mmiscT_ref[...])

    @pl.when(g > 0)
    def _():
        stage(aq_ref[0], iq_ref[0], miscT_ref[0])

    lane = lax.broadcasted_iota(I32, (1, QB), 1)
    t_real = (g - 1) * QB + lane
    qpos = jnp.where(is_meta, lane, N_META + t_real)

    def keys_for_tile(ik2_t, allowed):
        s = None
        for h in range(IDX_HEADS):
            sh = _dot_nt(ik2_t, iqm_s[h])
            term = iw_s[h:h + 1, :] * jnp.maximum(sh, 0.0)
            s = term if s is None else s + term
        s = jnp.where(s == 0.0, 0.0, s)
        bits = lax.bitcast_convert_type(s, I32)
        key = bits ^ ((bits >> 31) & 0x7FFFFFFF)
        return key if allowed is None else jnp.where(allowed, key, INT_MIN)

    def split16(slot, r, key):
        keys_s[slot, 0:r, :] = key
        khi_s[slot, 0:r, :] = (key >> 16).astype(jnp.int16)
        klo_s[slot, 0:r, :] = ((key & 0xFFFF) + I16_MIN).astype(jnp.int16)

    mrow = lax.broadcasted_iota(I32, (MT, QB), 0)
    krow = lax.broadcasted_iota(I32, (KT, QB), 0)
    split16(0, MT, keys_for_tile(mik2_ref[0:MT, :], mrow < N_META))

    def fill_body(kt, c):
        split16(kt + 1, KT, keys_for_tile(ik2_ref[0, kt], None))
        return c

    lax.fori_loop(0, n_past, fill_body, 0)

    @pl.when(g > 0)
    def _():
        split16(g, KT, keys_for_tile(ik2_ref[0, g - 1], (krow >> 6) <= (lane >> 6)))

    def count(arr_s, pred):
        def cnt(tile):
            x = jnp.where(pred(tile), jnp.int16(1), jnp.int16(0))
            x = x.reshape(tile.shape[0] // 16, 16, QB)
            parts = [x[j] for j in range(x.shape[0])]
            while len(parts) > 1:
                parts = [parts[j] + parts[j + 1] for j in range(0, len(parts), 2)]
            return parts[0]

        acc = cnt(arr_s[0, 0:MT, :])
        acc = lax.fori_loop(0, g, lambda kt, a: a + cnt(arr_s[kt + 1]), acc)
        return jnp.sum(acc.astype(I32), axis=0, keepdims=True)

    def bisect16(arr_s, base):
        def body(i, ut):
            cand_u = ut | jnp.left_shift(jnp.int32(1), 15 - i)
            cand = (cand_u + I16_MIN).astype(jnp.int16)
            cnt = base + count(arr_s, lambda tile: tile >= cand)
            return jnp.where(cnt >= topk, cand_u, ut)

        return lax.fori_loop(0, 16, body, jnp.zeros((1, QB), I32))

    thi32 = bisect16(khi_s, 0) + I16_MIN
    thi = thi32.astype(jnp.int16)
    n_hi_gt = count(khi_s, lambda tile: tile > thi)

    def lo_mask(slot, r):
        klo_s[slot, 0:r, :] = jnp.where(khi_s[slot, 0:r, :] == thi, klo_s[slot, 0:r, :], jnp.int16(I16_MIN))

    lo_mask(0, MT)

    def lo_mask_body(kt, c):
        lo_mask(kt + 1, KT)
        return c

    lax.fori_loop(0, g, lo_mask_body, 0)
    tlo_u = bisect16(klo_s, n_hi_gt)
    tlo = (tlo_u + I16_MIN).astype(jnp.int16)
    thr = jnp.left_shift(thi32, 16) | tlo_u
    n_gt = n_hi_gt + count(klo_s, lambda tile: tile > tlo)
    need = jnp.where(thr == INT_MIN, 0, topk - n_gt).astype(F32)

    ninf = jnp.float32(-jnp.inf)
    m_s[...] = jnp.full(m_s.shape, -jnp.inf, F32)
    acc_s[...] = jnp.zeros(acc_s.shape, F32)

    def attend(keys_t, kx_fn, vT_fn, after, n_eq_before):
        r = keys_t.shape[0]
        ri = lax.broadcasted_iota(I32, (r, r), 0)
        ci = lax.broadcasted_iota(I32, (r, r), 1)
        lower = jnp.where(ri > ci, 1.0, 0.0).astype(BF16)
        eq = keys_t == thr
        rank = _dot(lower, jnp.where(eq, 1.0, 0.0).astype(BF16)) + n_eq_before
        bias_s[0:r, :] = jnp.where(keys_t > thr, 0.0, jnp.where(eq, jnp.where(rank < need, 0.0, ninf), ninf))
        n_eq_after = rank[r - 1:r, :] + jnp.where(eq[r - 1:r, :], 1.0, 0.0)
        m_safe, corr = [], []
        for h in range(A_HEADS):
            p = h // 2
            lg = _dot_nt(kx_fn(p), qm_s[h]) + bias_s[0:r, :]
            if after is not None:
                lg = lg + ALIBI_C[h] * after
            lg_s[h, 0:r, :] = lg
            m_old = m_s[h]
            m_new = jnp.maximum(m_old, jnp.max(lg, axis=0, keepdims=True))
            m_s[h] = m_new
            m_safe.append(jnp.where(m_new == ninf, 0.0, m_new))
            corr.append(jnp.exp2(m_old - m_safe[h]))
        for h in range(A_HEADS):
            p_s[h, 0:r, :] = jnp.exp2(lg_s[h, 0:r, :] - m_safe[h]).astype(BF16)
        for h in range(A_HEADS):
            acc_s[h] = acc_s[h] * corr[h] + _dot(vT_fn(h), p_s[h, 0:r, :])
        return n_eq_after

    def after_term(kpos):
        return jnp.minimum(2 * (qpos - kpos), 0).astype(F32)

    n_eq = attend(keys_s[0, 0:MT, :],
                  lambda p: makx_ref[0:MT, 2 * p * LANES:(2 * p + 2) * LANES],
                  lambda h: mavT_ref[0, h * VT_ROWS:(h + 1) * VT_ROWS, 0:MT],
                  after_term(mrow), jnp.zeros((1, QB), F32))

    def real_tile(kt, after, n_eq_c):
        return attend(keys_s[kt + 1],
                      lambda p: akx_ref[0, kt, :, 2 * p * LANES:(2 * p + 2) * LANES],
                      lambda h: avT_ref[0, kt, h * VT_ROWS:(h + 1) * VT_ROWS, :],
                      after, n_eq_c)

    n_eq = lax.fori_loop(0, n_past, lambda kt, c: real_tile(kt, None, c), n_eq)

    @pl.when(g > 0)
    def _():
        real_tile(g - 1, after_term(N_META + (g - 1) * KT + krow), n_eq)

    outs = []
    for h in range(A_HEADS):
        o = acc_s[h, 0:A_HEAD_DIM, :] / acc_s[h, A_HEAD_DIM:A_HEAD_DIM + 1, :]
        ms = jnp.mean(o * o, axis=0, keepdims=True)
        outs.append(o * lax.rsqrt(ms + RMS_EPS))
    res = (jnp.concatenate(outs, axis=0).T * ng_ref[...]).astype(BF16)

    @pl.when(is_meta)
    def _():
        mout_ref[...] = res

    @pl.when(g > 0)
    def _():
        out_ref[0] = res


def _alibi_query_features():
    rows = []
    for c in ALIBI_C:
        rest = jnp.float32(c)
        lanes = []
        for _ in range(N_SLOPE_PARTS):
            part = rest.astype(BF16).astype(F32)
            lanes += [part * CHUNK, part]
            rest = rest - part
        rows.append(jnp.stack(lanes + [jnp.float32(0.0)] * (LANES - len(lanes))))
    return jnp.stack(rows)


def _dsa(aq, iq, miscT, akx, avT, ik2, m, ng, topk):
    nb, s, _ = aq.shape
    nq = s // QB
    nkt = s // KT
    vt = A_HEADS * VT_ROWS
    qidx = lambda b, g: (b, jnp.maximum(g - 1, 0), 0)
    const = lambda shp: pl.BlockSpec(shp, lambda b, g: (0,) * len(shp))
    in_specs = [
        pl.BlockSpec((1, QB, 512), qidx),
        pl.BlockSpec((1, QB, 256), qidx),
        pl.BlockSpec((1, 128, QB), lambda b, g: (b, 0, jnp.maximum(g - 1, 0))),
        pl.BlockSpec((1, nkt, KT, KX_COLS), lambda b, g: (b, 0, 0, 0)),
        pl.BlockSpec((1, nkt, vt, KT), lambda b, g: (b, 0, 0, 0)),
        pl.BlockSpec((1, nkt, KT, 128), lambda b, g: (b, 0, 0, 0)),
        const((META_ROWS, 512)), const((META_ROWS, 256)), const((128, META_ROWS)),
        const((META_ROWS, KX_COLS)), const((1, vt, KT)), const((META_ROWS, 128)),
        const((A_HEADS, LANES)), const((1, 512)),
    ]
    out_shape = (jax.ShapeDtypeStruct((nb, s, 512), BF16), jax.ShapeDtypeStruct((META_ROWS, 512), BF16))
    out_specs = (pl.BlockSpec((1, QB, 512), qidx), const((META_ROWS, 512)))
    scratch = [
        pltpu.VMEM((nkt + 1, KT, QB), I32),
        pltpu.VMEM((nkt + 1, KT, QB), jnp.int16),
        pltpu.VMEM((nkt + 1, KT, QB), jnp.int16),
        pltpu.VMEM((A_HEADS, QB, 2 * LANES), BF16),
        pltpu.VMEM((IDX_HEADS, QB, LANES), BF16),
        pltpu.VMEM((8, QB), F32),
        pltpu.VMEM((A_HEADS, 1, QB), F32),
        pltpu.VMEM((A_HEADS, VT_ROWS, QB), F32),
        pltpu.VMEM((KT, QB), F32),
        pltpu.VMEM((A_HEADS, KT, QB), F32),
        pltpu.VMEM((A_HEADS, KT, QB), BF16),
    ]
    return pl.pallas_call(
        functools.partial(_dsa_kernel, topk=topk),
        grid=(nb, nq + 1),
        in_specs=in_specs,
        out_specs=out_specs,
        out_shape=out_shape,
        scratch_shapes=scratch,
        compiler_params=pltpu.CompilerParams(
            dimension_semantics=("arbitrary", "arbitrary"), vmem_limit_bytes=VMEM_LIMIT),
        name="dsa",
    )(aq, iq, miscT, akx.reshape(nb, nkt, KT, KX_COLS), avT, ik2.reshape(nb, nkt, KT, 128),
      m["aq"], m["iq"], m["miscT"], m["akx"], m["avT"], m["ik2"], _alibi_query_features(), ng)


def _cumsum_rows(x):
    n = x.shape[0]
    row = lax.broadcasted_iota(I32, x.shape, 0)
    sh = 1
    while sh < n:
        x = x + jnp.where(row >= sh, pltpu.roll(x, sh, 0), 0.0)
        sh *= 2
    return x


def _gla_chunk(q, k, v, lr, og, wg, bg, ng, st_s, valid_rows):
    z = _dot(lr.astype(BF16), wg) + bg
    logg = (jnp.minimum(z, 0.0) - jnp.log1p(jnp.exp(-jnp.abs(z)))) / B_GATE_TAU
    if valid_rows is not None:
        rows = lax.broadcasted_iota(I32, logg.shape, 0)
        logg = jnp.where(rows < valid_rows, logg, 0.0)
    b = _cumsum_rows(logg)
    b_mid = b[CHUNK // 2 - 1:CHUNK // 2, :]
    b_last = b[CHUNK - 1:CHUNK, :]
    q_in = q * jnp.exp(b)
    q_ic = q * jnp.exp(b - b_mid)
    k_ic = k * jnp.exp(b_mid - b)
    k_st = k * jnp.exp(b_last - b)
    decay = jnp.exp(b_last)

    lane_half = lax.broadcasted_iota(I32, (1, LANES), 1) // B_KEY_DIM
    ri = lax.broadcasted_iota(I32, (CHUNK, CHUNK), 0)
    ci = lax.broadcasted_iota(I32, (CHUNK, CHUNK), 1)
    causal = ci <= ri
    outs = []
    for p in range(B_HEADS // 2):
        sl = slice(p * LANES, (p + 1) * LANES)
        st_old = st_s[p]
        st_new = st_old * decay[:, sl]
        k_ic_p = k_ic[:, sl].astype(BF16)
        for hh in range(2):
            h = 2 * p + hh
            hm = lane_half == hh
            v_h = v[:, h * B_VAL_DIM:(h + 1) * B_VAL_DIM]
            a = _dot_nt(jnp.where(hm, q_ic[:, sl], 0.0).astype(BF16), k_ic_p)
            a = jnp.where(causal, a, 0.0)
            o = _dot_nt(jnp.where(hm, q_in[:, sl], 0.0).astype(BF16), st_old.astype(BF16))
            o = o + _dot(a.astype(BF16), v_h)
            st_new = st_new + _dot_tn(v_h, jnp.where(hm, k_st[:, sl], 0.0).astype(BF16))
            ms = jnp.mean(o * o, axis=-1, keepdims=True)
            on = o * lax.rsqrt(ms + RMS_EPS) * ng[:, h * B_VAL_DIM:(h + 1) * B_VAL_DIM]
            og_h = og[:, h * B_VAL_DIM:(h + 1) * B_VAL_DIM]
            outs.append(on * (og_h * jax.nn.sigmoid(og_h)))
        st_s[p] = st_new
    return jnp.concatenate(outs, axis=-1)


def _gla_kernel(bq_ref, bk_ref, bv_ref, misc_ref, bog_ref,
                mbq_ref, mbk_ref, mbv_ref, mmisc_ref, mbog_ref,
                wg_ref, bg_ref, ng_ref, out_ref, mout_ref, st_s):
    j = pl.program_id(1)
    wg = wg_ref[...]
    bg = bg_ref[...]
    ng = ng_ref[...]

    @pl.when(j == 0)
    def _():
        st_s[...] = jnp.zeros(st_s.shape, F32)
        o = _gla_chunk(mbq_ref[0:CHUNK, :].astype(F32), mbk_ref[0:CHUNK, :].astype(F32),
                       mbv_ref[0:CHUNK, :], mmisc_ref[0:CHUNK, MISC_LR:MISC_LR + B_GATE_RANK],
                       mbog_ref[0:CHUNK, :], wg, bg, ng, st_s, N_META)
        mout_ref[...] = o.astype(BF16)

    for c in range(GLA_BLOCK // CHUNK):
        rs = slice(c * CHUNK, (c + 1) * CHUNK)
        o = _gla_chunk(bq_ref[0, rs, :].astype(F32), bk_ref[0, rs, :].astype(F32), bv_ref[0, rs, :],
                       misc_ref[0, rs, MISC_LR:MISC_LR + B_GATE_RANK], bog_ref[0, rs, :],
                       wg, bg, ng, st_s, None)
        out_ref[0, rs, :] = o.astype(BF16)


def _gla(bq, bk, bv, misc, bog, m, wg, bg, ng):
    nb, s, _ = bq.shape
    row = lambda c: pl.BlockSpec((1, GLA_BLOCK, c), lambda b, j: (b, j, 0))
    const = lambda shp: pl.BlockSpec(shp, lambda b, j: (0,) * len(shp))
    return pl.pallas_call(
        _gla_kernel,
        grid=(nb, s // GLA_BLOCK),
        in_specs=[row(256), row(256), row(512), row(128), row(512),
                  const((META_ROWS, 256)), const((META_ROWS, 256)), const((META_ROWS, 512)),
                  const((META_ROWS, 128)), const((META_ROWS, 512)),
                  const((B_GATE_RANK, 256)), const((1, 256)), const((1, 512))],
        out_specs=(row(512), const((CHUNK, 512))),
        out_shape=(jax.ShapeDtypeStruct((nb, s, 512), BF16), jax.ShapeDtypeStruct((CHUNK, 512), BF16)),
        scratch_shapes=[pltpu.VMEM((B_HEADS // 2, LANES, LANES), F32)],
        compiler_params=pltpu.CompilerParams(
            dimension_semantics=("arbitrary", "arbitrary"), vmem_limit_bytes=VMEM_LIMIT),
        name="gla",
    )(bq, bk, bv, misc, bog, m["bq"], m["bk"], m["bv"], m["misc"], m["bog"], wg, bg, ng)


def _outproj_kernel(x_ref, a_ref, b_ref, lng_ref, lnb_ref, wo_ref, g1_ref, b1_ref, out_ref):
    h = _layer_norm(x_ref[0], lng_ref[...], lnb_ref[...])
    mixed = _dot(a_ref[0], wo_ref[0:A_WIDTH, :]) + _dot(b_ref[0], wo_ref[A_WIDTH:, :])
    out_ref[0] = _layer_norm(ALPHA * h + mixed, g1_ref[...], b1_ref[...])


def _outproj(x, a, b, lng, lnb, wo, g1, b1, tm):
    nb, s, d = x.shape
    row = lambda c: pl.BlockSpec((1, tm, c), lambda bb, i: (bb, i, 0))
    const = lambda shp: pl.BlockSpec(shp, lambda bb, i: (0,) * len(shp))
    return pl.pallas_call(
        _outproj_kernel,
        grid=(nb, s // tm),
        in_specs=[row(d), row(512), row(512), const((1, d)), const((1, d)), const((d, d)),
                  const((1, d)), const((1, d))],
        out_specs=row(d),
        out_shape=jax.ShapeDtypeStruct((nb, s, d), F32),
        compiler_params=pltpu.CompilerParams(
            dimension_semantics=("arbitrary", "arbitrary"), vmem_limit_bytes=VMEM_LIMIT),
        name="outproj",
    )(x, a, b, lng, lnb, wo, g1, b1)


def _gelu_tanh(x):
    c = 0.7978845608028654
    return 0.5 * x * (1.0 + jnp.tanh(c * (x + 0.044715 * (x * x * x))))


def _ffn_kernel(h_ref, halo_ref, hm_ref, wup_ref, cw_ref, cb_ref, wdn_ref, g2_ref, b2_ref, out_ref, y_s):
    j = pl.program_id(1)
    h = h_ref[0]
    tm = h.shape[0]
    halo = jnp.where(j == 0, hm_ref[...], halo_ref[0])
    hb = jnp.concatenate([halo, h], axis=0).astype(BF16)
    for f in range(D_FF // FF_TILE):
        fs = slice(f * FF_TILE, (f + 1) * FF_TILE)
        a = _dot(hb, wup_ref[:, fs])
        gate = _dot(hb[HALO:, :], wup_ref[:, D_FF + f * FF_TILE:D_FF + (f + 1) * FF_TILE])
        cw = cw_ref[:, fs]
        conv = cb_ref[:, fs] + cw[CONV_W - 1:CONV_W, :] * a[HALO:, :]
        for back in range(1, CONV_W):
            tap = cw[CONV_W - 1 - back:CONV_W - back, :]
            conv = conv + tap * pltpu.roll(a, back, 0)[HALO:, :]
        y_s[:, fs] = (_gelu_tanh(conv) * gate).astype(BF16)
    ffn = _dot(y_s[...], wdn_ref[...])
    out_ref[0] = _layer_norm(ALPHA * h + ffn, g2_ref[...], b2_ref[...])


def _ffn(h1, h1_meta, wup, cw, cb, wdn, g2, b2, tm):
    nb, s, d = h1.shape
    per = tm // HALO
    const = lambda shp: pl.BlockSpec(shp, lambda b, j: (0,) * len(shp))
    return pl.pallas_call(
        _ffn_kernel,
        grid=(nb, s // tm),
        in_specs=[pl.BlockSpec((1, tm, d), lambda b, j: (b, j, 0)),
                  pl.BlockSpec((1, HALO, d), lambda b, j: (b, jnp.maximum(j * per - 1, 0), 0)),
                  const((HALO, d)), const((d, 2 * D_FF)), const((CONV_W, D_FF)), const((1, D_FF)),
                  const((D_FF, d)), const((1, d)), const((1, d))],
        out_specs=pl.BlockSpec((1, tm, d), lambda b, j: (b, j, 0)),
        out_shape=jax.ShapeDtypeStruct((nb, s, d), F32),
        scratch_shapes=[pltpu.VMEM((tm, D_FF), BF16)],
        compiler_params=pltpu.CompilerParams(
            dimension_semantics=("arbitrary", "arbitrary"), vmem_limit_bytes=VMEM_LIMIT),
        name="ffn",
    )(h1, h1, h1_meta, wup, cw, cb, wdn, g2, b2)


def _permute_w_in(w):
    o = 0
    parts = {}
    for name, n in (("a_q", 512), ("a_k", 512), ("a_v", 512), ("i_q", 256), ("i_k", 64), ("i_w", 4),
                    ("b_q", 256), ("b_k", 256), ("b_v", 512), ("b_lr", 16), ("b_og", 512)):
        parts[name] = w[:, o:o + n]
        o += n
    zeros = jnp.zeros((w.shape[0], 128 - B_GATE_RANK - IDX_HEADS), w.dtype)
    cols = [parts["a_q"] * (A_HEAD_DIM ** -0.5 * LOG2E), parts["a_k"], parts["a_v"], parts["i_q"],
            parts["b_q"] * (B_KEY_DIM ** -0.5), parts["b_k"], parts["b_v"], parts["b_og"],
            parts["i_k"], parts["i_k"],
            parts["b_lr"], parts["i_w"] * (IDX_HEADS ** -0.5) * (IDX_DIM ** -0.5), zeros]
    return jnp.concatenate(cols, axis=1).astype(BF16)


def kernel(x, meta, ln_in_g, ln_in_b, w_in, w_gate_b, b_gate_b, attn_norm_g, gla_norm_g, w_out,
           ln1_g, ln1_b, w_up, conv_w, conv_b, w_down, ln2_g, ln2_b):
    nb, s, d = x.shape
    assert d == D_MODEL and w_in.shape[0] == DEPTH == 1
    assert s % ROW_TILE == 0 and s % KT == 0 and s % GLA_BLOCK == 0
    topk = min(TOPK_MAX, s // 4)
    l = 0
    r2 = lambda v: v.reshape(1, -1)
    lng, lnb = r2(ln_in_g), r2(ln_in_b)
    w_perm = _permute_w_in(w_in[l])
    wo = w_out[l].astype(BF16)
    wup = w_up[l].astype(BF16)
    wdn = w_down[l].astype(BF16)
    wg = w_gate_b[l].astype(BF16)
    bg = r2(b_gate_b[l])
    ng_a = r2(attn_norm_g[l])
    ng_b = r2(gla_norm_g[l])

    x_meta = jnp.zeros((1, META_ROWS, d), x.dtype).at[0, :N_META].set(meta.astype(x.dtype))
    names = ("aq", "akx", "avT", "iq", "bq", "bk", "bv", "bog", "ik2", "misc", "miscT")
    mo = dict(zip(names, _inproj(x_meta, lng, lnb, w_perm, META_ROWS, 0)))
    keep = (jnp.arange(META_ROWS) < N_META)
    m = {}
    for n in names:
        v = mo[n][0]
        if n in ("avT", "miscT"):
            m[n] = jnp.where(keep[None, :], v, jnp.zeros_like(v)).reshape((1,) * (n == "avT") + v.shape[-2:])
        else:
            m[n] = jnp.where(keep[:, None], v, jnp.zeros_like(v))

    aq, akx, avT, iq, bq, bk, bv, bog, ik2, misc, miscT = _inproj(x, lng, lnb, w_perm, ROW_TILE, N_META)
    a_out, a_out_m = _dsa(aq, iq, miscT, akx, avT, ik2, m, ng_a, topk)
    b_out, b_out_m = _gla(bq, bk, bv, misc, bog, m, wg, bg, ng_b)

    h1 = _outproj(x, a_out, b_out, lng, lnb, wo, r2(ln1_g[l]), r2(ln1_b[l]), ROW_TILE)
    b_out_m = jnp.zeros((META_ROWS, B_WIDTH), BF16).at[:CHUNK].set(b_out_m)
    h1_m = _outproj(x_meta, a_out_m[None], b_out_m[None], lng, lnb, wo, r2(ln1_g[l]), r2(ln1_b[l]), META_ROWS)
    return _ffn(h1, h1_m[0, N_META - HALO:N_META], wup, conv_w[l], r2(conv_b[l]), wdn, r2(ln2_g[l]), r2(ln2_b[l]), ROW_TILE)
```

```python
import functools

import jax
import jax.numpy as jnp
from jax import lax
from jax.experimental import pallas as pl
from jax.experimental.pallas import tpu as pltpu

F32 = jnp.float32
BF16 = jnp.bfloat16
I32 = jnp.int32

D_MODEL = 1024
CHUNK = 64
N_META = 16
A_HEADS = 8
A_WIDTH = 512
A_HEAD_DIM = 64
IDX_HEADS = 4
IDX_DIM = 64
TOPK_MAX = 256
B_HEADS = 4
B_WIDTH = 512
B_VAL_DIM = 128
B_KEY_DIM = 64
B_GATE_RANK = 16
B_GATE_TAU = 16.0
D_FF = 2816
CONV_W = 3
LN_EPS = 1e-5
RMS_EPS = 1e-6
DEPTH = 1
ALPHA = (2.0 * DEPTH) ** 0.25

LANES = 128
ROW_TILE = 512
QB = 256
KT = 256
MT = 64
META_ROWS = 256
GLA_BLOCK = 256
FF_TILE = 256
HALO = 16
INT_MIN = -(2 ** 31)
I16_MIN = -(2 ** 15)
N_SLOPE_PARTS = 3
VT_ROWS = 80
KX_COLS = 2 * A_WIDTH
LOG2E = 1.4426950408889634
ALIBI_C = tuple(2.0 ** (-8.0 * (h + 1) / A_HEADS) * LOG2E for h in range(A_HEADS))
VMEM_LIMIT = 56 * 1024 * 1024

_COLS = {}
_off = 0
for _name, _w in (("aq", 512), ("ak", 512), ("av", 512), ("iq", 256), ("bq", 256), ("bk", 256),
                  ("bv", 512), ("bog", 512), ("ik2", 128), ("misc", 128)):
    _COLS[_name] = (_off, _w)
    _off += _w
PROJ_COLS = _off
MISC_LR = 0
MISC_IW = 16


def _layer_norm(x, g, b):
    mu = jnp.mean(x, axis=-1, keepdims=True)
    xc = x - mu
    var = jnp.mean(xc * xc, axis=-1, keepdims=True)
    return xc * lax.rsqrt(var + LN_EPS) * g + b


def _dot(a, b):
    return jnp.dot(a, b, preferred_element_type=F32)


def _dot_nt(a, b):
    return lax.dot_general(a, b, (((1,), (1,)), ((), ())), preferred_element_type=F32)


def _dot_tn(a, b):
    return lax.dot_general(a, b, (((0,), (0,)), ((), ())), preferred_element_type=F32)


def _inproj_kernel(x_ref, g_ref, b_ref, w_ref, aq_ref, akx_ref, avT_ref, iq_ref, bq_ref, bk_ref,
                   bv_ref, bog_ref, ik2_ref, misc_ref, miscT_ref, *, pos0):
    h = _layer_norm(x_ref[0], g_ref[...], b_ref[...])
    hb = h.astype(BF16)
    tm = hb.shape[0]

    def proj(name):
        lo, w = _COLS[name]
        return _dot(hb, w_ref[:, lo:lo + w])

    aq_ref[0] = proj("aq").astype(BF16)
    ak = proj("ak")
    rows = lax.broadcasted_iota(I32, (tm, LANES), 0)
    lanes = lax.broadcasted_iota(I32, (tm, LANES), 1)
    pos = pos0 + pl.program_id(1) * tm + rows
    feat = jnp.where(lanes < 2 * N_SLOPE_PARTS, jnp.where(lanes % 2 == 0, pos >> 6, pos & 63), 0)
    feat = feat.astype(F32).astype(BF16)
    for p in range(A_HEADS // 2):
        akx_ref[0, :, 2 * p * LANES:(2 * p + 1) * LANES] = ak[:, p * LANES:(p + 1) * LANES].astype(BF16)
        akx_ref[0, :, (2 * p + 1) * LANES:(2 * p + 2) * LANES] = feat
    avT = proj("av").T
    ones = jnp.ones((VT_ROWS - A_HEAD_DIM, KT), BF16)
    for i in range(avT_ref.shape[1]):
        for hd in range(A_HEADS):
            avT_ref[0, i, hd * VT_ROWS:hd * VT_ROWS + A_HEAD_DIM, :] = (
                avT[hd * A_HEAD_DIM:(hd + 1) * A_HEAD_DIM, i * KT:(i + 1) * KT].astype(BF16))
            avT_ref[0, i, hd * VT_ROWS + A_HEAD_DIM:(hd + 1) * VT_ROWS, :] = ones
    iq_ref[0] = proj("iq").astype(BF16)
    bq_ref[0] = proj("bq").astype(BF16)
    bk_ref[0] = proj("bk").astype(BF16)
    bv_ref[0] = proj("bv").astype(BF16)
    bog_ref[0] = proj("bog")
    ik2_ref[0] = proj("ik2").astype(BF16)
    misc = proj("misc")
    misc_ref[0] = misc
    miscT_ref[0] = misc.T


def _inproj(x, ln_g, ln_b, w_perm, tm, pos0):
    nb, s, d = x.shape
    nt = s // tm
    row = lambda c: pl.BlockSpec((1, tm, c), lambda b, i: (b, i, 0))
    const = lambda shp: pl.BlockSpec(shp, lambda b, i: (0,) * len(shp))
    out_shape = (
        jax.ShapeDtypeStruct((nb, s, 512), BF16),
        jax.ShapeDtypeStruct((nb, s, KX_COLS), BF16),
        jax.ShapeDtypeStruct((nb, s // KT, A_HEADS * VT_ROWS, KT), BF16),
        jax.ShapeDtypeStruct((nb, s, 256), BF16),
        jax.ShapeDtypeStruct((nb, s, 256), BF16),
        jax.ShapeDtypeStruct((nb, s, 256), BF16),
        jax.ShapeDtypeStruct((nb, s, 512), BF16),
        jax.ShapeDtypeStruct((nb, s, 512), F32),
        jax.ShapeDtypeStruct((nb, s, 128), BF16),
        jax.ShapeDtypeStruct((nb, s, 128), F32),
        jax.ShapeDtypeStruct((nb, 128, s), F32),
    )
    out_specs = (
        row(512), row(KX_COLS),
        pl.BlockSpec((1, tm // KT, A_HEADS * VT_ROWS, KT), lambda b, i: (b, i, 0, 0)),
        row(256), row(256), row(256), row(512), row(512), row(128), row(128),
        pl.BlockSpec((1, 128, tm), lambda b, i: (b, 0, i)),
    )
    return pl.pallas_call(
        functools.partial(_inproj_kernel, pos0=pos0),
        grid=(nb, nt),
        in_specs=[row(d), const((1, d)), const((1, d)), const((d, PROJ_COLS))],
        out_specs=out_specs,
        out_shape=out_shape,
        compiler_params=pltpu.CompilerParams(
            dimension_semantics=("arbitrary", "arbitrary"), vmem_limit_bytes=VMEM_LIMIT),
        name="inproj",
    )(x, ln_g, ln_b, w_perm)


def _dsa_kernel(aq_ref, iq_ref, miscT_ref, akx_ref, avT_ref, ik2_ref,
                maq_ref, miq_ref, mmiscT_ref, makx_ref, mavT_ref, mik2_ref, qfeat_ref, ng_ref,
                out_ref, mout_ref,
                keys_s, khi_s, klo_s, qm_s, iqm_s, iw_s, m_s, acc_s, bias_s, lg_s, p_s, *, topk):
    g = pl.program_id(1)
    is_meta = g == 0
    n_past = jnp.maximum(g - 1, 0)

    lane_half = lax.broadcasted_iota(I32, (1, LANES), 1) // A_HEAD_DIM

    def stage(aq, iq, miscT):
        for h in range(A_HEADS):
            p = h // 2
            qp = aq[:, p * LANES:(p + 1) * LANES]
            qm_s[h, :, 0:LANES] = jnp.where(lane_half == (h % 2), qp, jnp.zeros_like(qp))
            qm_s[h, :, LANES:2 * LANES] = jnp.broadcast_to(qfeat_ref[h:h + 1, :], (QB, LANES)).astype(BF16)
        for h in range(IDX_HEADS):
            p = h // 2
            qp = iq[:, p * LANES:(p + 1) * LANES]
            iqm_s[h] = jnp.where(lane_half == (h % 2), qp, jnp.zeros_like(qp))
        iw_s[...] = miscT[MISC_IW:MISC_IW + 8, :]

    @pl.when(is_meta)
    def _():
        stage(maq_ref[...], miq_ref[...], mmiscT_ref[...])

    @pl.when(g > 0)
    def _():
        stage(aq_ref[0], iq_ref[0], miscT_ref[0])

    lane = lax.broadcasted_iota(I32, (1, QB), 1)
    t_real = (g - 1) * QB + lane
    qpos = jnp.where(is_meta, lane, N_META + t_real)

    def keys_for_tile(ik2_t, allowed):
        s = None
        for h in range(IDX_HEADS):
            sh = _dot_nt(ik2_t, iqm_s[h])
            term = iw_s[h:h + 1, :] * jnp.maximum(sh, 0.0)
            s = term if s is None else s + term
        s = jnp.where(s == 0.0, 0.0, s)
        bits = lax.bitcast_convert_type(s, I32)
        key = bits ^ ((bits >> 31) & 0x7FFFFFFF)
        return key if allowed is None else jnp.where(allowed, key, INT_MIN)

    def split16(slot, r, key):
        keys_s[slot, 0:r, :] = key
        khi_s[slot, 0:r, :] = (key >> 16).astype(jnp.int16)
        klo_s[slot, 0:r, :] = ((key & 0xFFFF) + I16_MIN).astype(jnp.int16)

    mrow = lax.broadcasted_iota(I32, (MT, QB), 0)
    krow = lax.broadcasted_iota(I32, (KT, QB), 0)
    split16(0, MT, keys_for_tile(mik2_ref[0:MT, :], mrow < N_META))

    def fill_body(kt, c):
        split16(kt + 1, KT, keys_for_tile(ik2_ref[0, kt], None))
        return c

    lax.fori_loop(0, n_past, fill_body, 0)

    @pl.when(g > 0)
    def _():
        split16(g, KT, keys_for_tile(ik2_ref[0, g - 1], (krow >> 6) <= (lane >> 6)))

    def count(arr_s, pred):
        def cnt(tile):
            x = jnp.where(pred(tile), jnp.int16(1), jnp.int16(0))
            x = x.reshape(tile.shape[0] // 16, 16, QB)
            parts = [x[j] for j in range(x.shape[0])]
            while len(parts) > 1:
                parts = [parts[j] + parts[j + 1] for j in range(0, len(parts), 2)]
            return parts[0]

        acc = cnt(arr_s[0, 0:MT, :])
        acc = lax.fori_loop(0, g, lambda kt, a: a + cnt(arr_s[kt + 1]), acc)
        return jnp.sum(acc.astype(I32), axis=0, keepdims=True)

    def bisect16(arr_s, base):
        def body(i, ut):
            cand_u = ut | jnp.left_shift(jnp.int32(1), 15 - i)
            cand = (cand_u + I16_MIN).astype(jnp.int16)
            cnt = base + count(arr_s, lambda tile: tile >= cand)
            return jnp.where(cnt >= topk, cand_u, ut)

        return lax.fori_loop(0, 16, body, jnp.zeros((1, QB), I32))

    thi32 = bisect16(khi_s, 0) + I16_MIN
    thi = thi32.astype(jnp.int16)
    n_hi_gt = count(khi_s, lambda tile: tile > thi)

    def lo_mask(slot, r):
        klo_s[slot, 0:r, :] = jnp.where(khi_s[slot, 0:r, :] == thi, klo_s[slot, 0:r, :], jnp.int16(I16_MIN))

    lo_mask(0, MT)

    def lo_mask_body(kt, c):
        lo_mask(kt + 1, KT)
        return c

    lax.fori_loop(0, g, lo_mask_body, 0)
    tlo_u = bisect16(klo_s, n_hi_gt)
    tlo = (tlo_u + I16_MIN).astype(jnp.int16)
    thr = jnp.left_shift(thi32, 16) | tlo_u
    n_gt = n_hi_gt + count(klo_s, lambda tile: tile > tlo)
    need = jnp.where(thr == INT_MIN, 0, topk - n_gt).astype(F32)

    ninf = jnp.float32(-jnp.inf)
    m_s[...] = jnp.full(m_s.shape, -jnp.inf, F32)
    acc_s[...] = jnp.zeros(acc_s.shape, F32)

    def attend(keys_t, kx_fn, vT_fn, after, n_eq_before):
        r = keys_t.shape[0]
        ri = lax.broadcasted_iota(I32, (r, r), 0)
        ci = lax.broadcasted_iota(I32, (r, r), 1)
        lower = jnp.where(ri > ci, 1.0, 0.0).astype(BF16)
        eq = keys_t == thr
        rank = _dot(lower, jnp.where(eq, 1.0, 0.0).astype(BF16)) + n_eq_before
        bias_s[0:r, :] = jnp.where(keys_t > thr, 0.0, jnp.where(eq, jnp.where(rank < need, 0.0, ninf), ninf))
        n_eq_after = rank[r - 1:r, :] + jnp.where(eq[r - 1:r, :], 1.0, 0.0)
        m_safe, corr = [], []
        for h in range(A_HEADS):
            p = h // 2
            lg = _dot_nt(kx_fn(p), qm_s[h]) + bias_s[0:r, :]
            if after is not None:
                lg = lg + ALIBI_C[h] * after
            lg_s[h, 0:r, :] = lg
            m_old = m_s[h]
            m_new = jnp.maximum(m_old, jnp.max(lg, axis=0, keepdims=True))
            m_s[h] = m_new
            m_safe.append(jnp.where(m_new == ninf, 0.0, m_new))
            corr.append(jnp.exp2(m_old - m_safe[h]))
        for h in range(A_HEADS):
            p_s[h, 0:r, :] = jnp.exp2(lg_s[h, 0:r, :] - m_safe[h]).astype(BF16)
        for h in range(A_HEADS):
            acc_s[h] = acc_s[h] * corr[h] + _dot(vT_fn(h), p_s[h, 0:r, :])
        return n_eq_after

    def after_term(kpos):
        return jnp.minimum(2 * (qpos - kpos), 0).astype(F32)

    n_eq = attend(keys_s[0, 0:MT, :],
                  lambda p: makx_ref[0:MT, 2 * p * LANES:(2 * p + 2) * LANES],
                  lambda h: mavT_ref[0, h * VT_ROWS:(h + 1) * VT_ROWS, 0:MT],
                  after_term(mrow), jnp.zeros((1, QB), F32))

    def real_tile(kt, after, n_eq_c):
        return attend(keys_s[kt + 1],
                      lambda p: akx_ref[0, kt, :, 2 * p * LANES:(2 * p + 2) * LANES],
                      lambda h: avT_ref[0, kt, h * VT_ROWS:(h + 1) * VT_ROWS, :],
                      after, n_eq_c)

    n_eq = lax.fori_loop(0, n_past, lambda kt, c: real_tile(kt, None, c), n_eq)

    @pl.when(g > 0)
    def _():
        real_tile(g - 1, after_term(N_META + (g - 1) * KT + krow), n_eq)

    outs = []
    for h in range(A_HEADS):
        o = acc_s[h, 0:A_HEAD_DIM, :] / acc_s[h, A_HEAD_DIM:A_HEAD_DIM + 1, :]
        ms = jnp.mean(o * o, axis=0, keepdims=True)
        outs.append(o * lax.rsqrt(ms + RMS_EPS))
    res = (jnp.concatenate(outs, axis=0).T * ng_ref[...]).astype(BF16)

    @pl.when(is_meta)
    def _():
        mout_ref[...] = res

    @pl.when(g > 0)
    def _():
        out_ref[0] = res


def _alibi_query_features():
    rows = []
    for c in ALIBI_C:
        rest = jnp.float32(c)
        lanes = []
        for _ in range(N_SLOPE_PARTS):
            part = rest.astype(BF16).astype(F32)
            lanes += [part * CHUNK, part]
            rest = rest - part
        rows.append(jnp.stack(lanes + [jnp.float32(0.0)] * (LANES - len(lanes))))
    return jnp.stack(rows)


def _dsa(aq, iq, miscT, akx, avT, ik2, m, ng, topk):
    nb, s, _ = aq.shape
    nq = s // QB
    nkt = s // KT
    vt = A_HEADS * VT_ROWS
    qidx = lambda b, g: (b, jnp.maximum(g - 1, 0), 0)
    const = lambda shp: pl.BlockSpec(shp, lambda b, g: (0,) * len(shp))
    in_specs = [
        pl.BlockSpec((1, QB, 512), qidx),
        pl.BlockSpec((1, QB, 256), qidx),
        pl.BlockSpec((1, 128, QB), lambda b, g: (b, 0, jnp.maximum(g - 1, 0))),
        pl.BlockSpec((1, nkt, KT, KX_COLS), lambda b, g: (b, 0, 0, 0)),
        pl.BlockSpec((1, nkt, vt, KT), lambda b, g: (b, 0, 0, 0)),
        pl.BlockSpec((1, nkt, KT, 128), lambda b, g: (b, 0, 0, 0)),
        const((META_ROWS, 512)), const((META_ROWS, 256)), const((128, META_ROWS)),
        const((META_ROWS, KX_COLS)), const((1, vt, KT)), const((META_ROWS, 128)),
        const((A_HEADS, LANES)), const((1, 512)),
    ]
    out_shape = (jax.ShapeDtypeStruct((nb, s, 512), BF16), jax.ShapeDtypeStruct((META_ROWS, 512), BF16))
    out_specs = (pl.BlockSpec((1, QB, 512), qidx), const((META_ROWS, 512)))
    scratch = [
        pltpu.VMEM((nkt + 1, KT, QB), I32),
        pltpu.VMEM((nkt + 1, KT, QB), jnp.int16),
        pltpu.VMEM((nkt + 1, KT, QB), jnp.int16),
        pltpu.VMEM((A_HEADS, QB, 2 * LANES), BF16),
        pltpu.VMEM((IDX_HEADS, QB, LANES), BF16),
        pltpu.VMEM((8, QB), F32),
        pltpu.VMEM((A_HEADS, 1, QB), F32),
        pltpu.VMEM((A_HEADS, VT_ROWS, QB), F32),
        pltpu.VMEM((KT, QB), F32),
        pltpu.VMEM((A_HEADS, KT, QB), F32),
        pltpu.VMEM((A_HEADS, KT, QB), BF16),
    ]
    return pl.pallas_call(
        functools.partial(_dsa_kernel, topk=topk),
        grid=(nb, nq + 1),
        in_specs=in_specs,
        out_specs=out_specs,
        out_shape=out_shape,
        scratch_shapes=scratch,
        compiler_params=pltpu.CompilerParams(
            dimension_semantics=("arbitrary", "arbitrary"), vmem_limit_bytes=VMEM_LIMIT),
        name="dsa",
    )(aq, iq, miscT, akx.reshape(nb, nkt, KT, KX_COLS), avT, ik2.reshape(nb, nkt, KT, 128),
      m["aq"], m["iq"], m["miscT"], m["akx"], m["avT"], m["ik2"], _alibi_query_features(), ng)


def _chunk_cumsum_rows(x):
    row_in_chunk = lax.broadcasted_iota(I32, x.shape, 0) & (CHUNK - 1)
    sh = 1
    while sh < CHUNK:
        x = x + jnp.where(row_in_chunk >= sh, pltpu.roll(x, sh, 0), 0.0)
        sh *= 2
    return x


def _rows_of_chunk(rows_per_chunk):
    return jnp.concatenate([jnp.broadcast_to(r, (CHUNK, r.shape[1])) for r in rows_per_chunk], axis=0)


def _gla_block(q, k, v, lr, og, wg, bg, ng, st_s, valid_rows):
    nrow = q.shape[0]
    nch = nrow // CHUNK
    z = _dot(lr.astype(BF16), wg) + bg
    logg = (jnp.minimum(z, 0.0) - jnp.log1p(jnp.exp(-jnp.abs(z)))) / B_GATE_TAU
    if valid_rows is not None:
        rows = lax.broadcasted_iota(I32, logg.shape, 0)
        logg = jnp.where(rows < valid_rows, logg, 0.0)
    b = _chunk_cumsum_rows(logg)
    b_last_rows = [b[(c + 1) * CHUNK - 1:(c + 1) * CHUNK, :] for c in range(nch)]
    b_mid = _rows_of_chunk([b[c * CHUNK + CHUNK // 2 - 1:c * CHUNK + CHUNK // 2, :] for c in range(nch)])
    b_last = _rows_of_chunk(b_last_rows)
    q_in = q * jnp.exp(b)
    q_ic = q * jnp.exp(b - b_mid)
    k_ic = k * jnp.exp(b_mid - b)
    k_st = k * jnp.exp(b_last - b)
    decay = [jnp.exp(r) for r in b_last_rows]

    lane_half = lax.broadcasted_iota(I32, (1, LANES), 1) // B_KEY_DIM
    ri = lax.broadcasted_iota(I32, (nrow, nrow), 0)
    ci = lax.broadcasted_iota(I32, (nrow, nrow), 1)
    causal = ((ri >> 6) == (ci >> 6)) & (ci <= ri)
    outs = []
    for p in range(B_HEADS // 2):
        sl = slice(p * LANES, (p + 1) * LANES)
        heads = (2 * p, 2 * p + 1)
        v_h = [v[:, h * B_VAL_DIM:(h + 1) * B_VAL_DIM] for h in heads]
        k_st_h = [jnp.where(lane_half == hh, k_st[:, sl], 0.0).astype(BF16) for hh in range(2)]
        states = [st_s[p]]
        for c in range(nch):
            rs = slice(c * CHUNK, (c + 1) * CHUNK)
            upd = _dot_tn(v_h[0][rs], k_st_h[0][rs]) + _dot_tn(v_h[1][rs], k_st_h[1][rs])
            states.append(states[c] * decay[c][:, sl] + upd)
        st_s[p] = states[nch]
        k_ic_p = k_ic[:, sl].astype(BF16)
        for hh, h in enumerate(heads):
            hm = lane_half == hh
            a = _dot_nt(jnp.where(hm, q_ic[:, sl], 0.0).astype(BF16), k_ic_p)
            o = _dot(jnp.where(causal, a, 0.0).astype(BF16), v_h[hh])
            q_in_h = jnp.where(hm, q_in[:, sl], 0.0).astype(BF16)
            o = o + jnp.concatenate(
                [_dot_nt(q_in_h[c * CHUNK:(c + 1) * CHUNK], states[c].astype(BF16)) for c in range(nch)], axis=0)
            ms = jnp.mean(o * o, axis=-1, keepdims=True)
            on = o * lax.rsqrt(ms + RMS_EPS) * ng[:, h * B_VAL_DIM:(h + 1) * B_VAL_DIM]
            og_h = og[:, h * B_VAL_DIM:(h + 1) * B_VAL_DIM]
            outs.append(on * (og_h * jax.nn.sigmoid(og_h)))
    return outs


def _gla_kernel(bq_ref, bk_ref, bv_ref, misc_ref, bog_ref,
                mbq_ref, mbk_ref, mbv_ref, mmisc_ref, mbog_ref,
                wg_ref, bg_ref, ng_ref, out_ref, mout_ref, st_s):
    j = pl.program_id(1)
    wg = wg_ref[...]
    bg = bg_ref[...]
    ng = ng_ref[...]

    @pl.when(j == 0)
    def _():
        st_s[...] = jnp.zeros(st_s.shape, F32)
        outs = _gla_block(mbq_ref[0:CHUNK, :].astype(F32), mbk_ref[0:CHUNK, :].astype(F32),
                          mbv_ref[0:CHUNK, :], mmisc_ref[0:CHUNK, MISC_LR:MISC_LR + B_GATE_RANK],
                          mbog_ref[0:CHUNK, :], wg, bg, ng, st_s, N_META)
        for h, o in enumerate(outs):
            mout_ref[:, h * B_VAL_DIM:(h + 1) * B_VAL_DIM] = o.astype(BF16)

    outs = _gla_block(bq_ref[0].astype(F32), bk_ref[0].astype(F32), bv_ref[0],
                      misc_ref[0, :, MISC_LR:MISC_LR + B_GATE_RANK], bog_ref[0], wg, bg, ng, st_s, None)
    for h, o in enumerate(outs):
        out_ref[0, :, h * B_VAL_DIM:(h + 1) * B_VAL_DIM] = o.astype(BF16)


def _gla(bq, bk, bv, misc, bog, m, wg, bg, ng):
    nb, s, _ = bq.shape
    row = lambda c: pl.BlockSpec((1, GLA_BLOCK, c), lambda b, j: (b, j, 0))
    const = lambda shp: pl.BlockSpec(shp, lambda b, j: (0,) * len(shp))
    return pl.pallas_call(
        _gla_kernel,
        grid=(nb, s // GLA_BLOCK),
        in_specs=[row(256), row(256), row(512), row(128), row(512),
                  const((META_ROWS, 256)), const((META_ROWS, 256)), const((META_ROWS, 512)),
                  const((META_ROWS, 128)), const((META_ROWS, 512)),
                  const((B_GATE_RANK, 256)), const((1, 256)), const((1, 512))],
        out_specs=(row(512), const((CHUNK, 512))),
        out_shape=(jax.ShapeDtypeStruct((nb, s, 512), BF16), jax.ShapeDtypeStruct((CHUNK, 512), BF16)),
        scratch_shapes=[pltpu.VMEM((B_HEADS // 2, LANES, LANES), F32)],
        compiler_params=pltpu.CompilerParams(
            dimension_semantics=("arbitrary", "arbitrary"), vmem_limit_bytes=VMEM_LIMIT),
        name="gla",
    )(bq, bk, bv, misc, bog, m["bq"], m["bk"], m["bv"], m["misc"], m["bog"], wg, bg, ng)


def _outproj_kernel(x_ref, a_ref, b_ref, lng_ref, lnb_ref, wo_ref, g1_ref, b1_ref, out_ref):
    h = _layer_norm(x_ref[0], lng_ref[...], lnb_ref[...])
    mixed = _dot(a_ref[0], wo_ref[0:A_WIDTH, :]) + _dot(b_ref[0], wo_ref[A_WIDTH:, :])
    out_ref[0] = _layer_norm(ALPHA * h + mixed, g1_ref[...], b1_ref[...])


def _outproj(x, a, b, lng, lnb, wo, g1, b1, tm):
    nb, s, d = x.shape
    row = lambda c: pl.BlockSpec((1, tm, c), lambda bb, i: (bb, i, 0))
    const = lambda shp: pl.BlockSpec(shp, lambda bb, i: (0,) * len(shp))
    return pl.pallas_call(
        _outproj_kernel,
        grid=(nb, s // tm),
        in_specs=[row(d), row(512), row(512), const((1, d)), const((1, d)), const((d, d)),
                  const((1, d)), const((1, d))],
        out_specs=row(d),
        out_shape=jax.ShapeDtypeStruct((nb, s, d), F32),
        compiler_params=pltpu.CompilerParams(
            dimension_semantics=("arbitrary", "arbitrary"), vmem_limit_bytes=VMEM_LIMIT),
        name="outproj",
    )(x, a, b, lng, lnb, wo, g1, b1)


def _gelu_tanh(x):
    c = 0.7978845608028654
    return 0.5 * x * (1.0 + jnp.tanh(c * (x + 0.044715 * (x * x * x))))


def _ffn_kernel(h_ref, halo_ref, hm_ref, wup_ref, cw_ref, cb_ref, wdn_ref, g2_ref, b2_ref, out_ref, y_s):
    j = pl.program_id(1)
    h = h_ref[0]
    tm = h.shape[0]
    halo = jnp.where(j == 0, hm_ref[...], halo_ref[0])
    hb = jnp.concatenate([halo, h], axis=0).astype(BF16)
    for f in range(D_FF // FF_TILE):
        fs = slice(f * FF_TILE, (f + 1) * FF_TILE)
        a = _dot(hb, wup_ref[:, fs])
        gate = _dot(hb[HALO:, :], wup_ref[:, D_FF + f * FF_TILE:D_FF + (f + 1) * FF_TILE])
        cw = cw_ref[:, fs]
        conv = cb_ref[:, fs] + cw[CONV_W - 1:CONV_W, :] * a[HALO:, :]
        for back in range(1, CONV_W):
            tap = cw[CONV_W - 1 - back:CONV_W - back, :]
            conv = conv + tap * pltpu.roll(a, back, 0)[HALO:, :]
        y_s[:, fs] = (_gelu_tanh(conv) * gate).astype(BF16)
    ffn = _dot(y_s[...], wdn_ref[...])
    out_ref[0] = _layer_norm(ALPHA * h + ffn, g2_ref[...], b2_ref[...])


def _ffn(h1, h1_meta, wup, cw, cb, wdn, g2, b2, tm):
    nb, s, d = h1.shape
    per = tm // HALO
    const = lambda shp: pl.BlockSpec(shp, lambda b, j: (0,) * len(shp))
    return pl.pallas_call(
        _ffn_kernel,
        grid=(nb, s // tm),
        in_specs=[pl.BlockSpec((1, tm, d), lambda b, j: (b, j, 0)),
                  pl.BlockSpec((1, HALO, d), lambda b, j: (b, jnp.maximum(j * per - 1, 0), 0)),
                  const((HALO, d)), const((d, 2 * D_FF)), const((CONV_W, D_FF)), const((1, D_FF)),
                  const((D_FF, d)), const((1, d)), const((1, d))],
        out_specs=pl.BlockSpec((1, tm, d), lambda b, j: (b, j, 0)),
        out_shape=jax.ShapeDtypeStruct((nb, s, d), F32),
        scratch_shapes=[pltpu.VMEM((tm, D_FF), BF16)],
        compiler_params=pltpu.CompilerParams(
            dimension_semantics=("arbitrary", "arbitrary"), vmem_limit_bytes=VMEM_LIMIT),
        name="ffn",
    )(h1, h1, h1_meta, wup, cw, cb, wdn, g2, b2)


def _permute_w_in(w):
    o = 0
    parts = {}
    for name, n in (("a_q", 512), ("a_k", 512), ("a_v", 512), ("i_q", 256), ("i_k", 64), ("i_w", 4),
                    ("b_q", 256), ("b_k", 256), ("b_v", 512), ("b_lr", 16), ("b_og", 512)):
        parts[name] = w[:, o:o + n]
        o += n
    zeros = jnp.zeros((w.shape[0], 128 - B_GATE_RANK - IDX_HEADS), w.dtype)
    cols = [parts["a_q"] * (A_HEAD_DIM ** -0.5 * LOG2E), parts["a_k"], parts["a_v"], parts["i_q"],
            parts["b_q"] * (B_KEY_DIM ** -0.5), parts["b_k"], parts["b_v"], parts["b_og"],
            parts["i_k"], parts["i_k"],
            parts["b_lr"], parts["i_w"] * (IDX_HEADS ** -0.5) * (IDX_DIM ** -0.5), zeros]
    return jnp.concatenate(cols, axis=1).astype(BF16)


def kernel(x, meta, ln_in_g, ln_in_b, w_in, w_gate_b, b_gate_b, attn_norm_g, gla_norm_g, w_out,
           ln1_g, ln1_b, w_up, conv_w, conv_b, w_down, ln2_g, ln2_b):
    nb, s, d = x.shape
    assert d == D_MODEL and w_in.shape[0] == DEPTH == 1
    assert s % ROW_TILE == 0 and s % KT == 0 and s % GLA_BLOCK == 0
    topk = min(TOPK_MAX, s // 4)
    l = 0
    r2 = lambda v: v.reshape(1, -1)
    lng, lnb = r2(ln_in_g), r2(ln_in_b)
    w_perm = _permute_w_in(w_in[l])
    wo = w_out[l].astype(BF16)
    wup = w_up[l].astype(BF16)
    wdn = w_down[l].astype(BF16)
    wg = w_gate_b[l].astype(BF16)
    bg = r2(b_gate_b[l])
    ng_a = r2(attn_norm_g[l])
    ng_b = r2(gla_norm_g[l])

    x_meta = jnp.zeros((1, META_ROWS, d), x.dtype).at[0, :N_META].set(meta.astype(x.dtype))
    names = ("aq", "akx", "avT", "iq", "bq", "bk", "bv", "bog", "ik2", "misc", "miscT")
    mo = dict(zip(names, _inproj(x_meta, lng, lnb, w_perm, META_ROWS, 0)))
    keep = (jnp.arange(META_ROWS) < N_META)
    m = {}
    for n in names:
        v = mo[n][0]
        if n in ("avT", "miscT"):
            m[n] = jnp.where(keep[None, :], v, jnp.zeros_like(v)).reshape((1,) * (n == "avT") + v.shape[-2:])
        else:
            m[n] = jnp.where(keep[:, None], v, jnp.zeros_like(v))

    aq, akx, avT, iq, bq, bk, bv, bog, ik2, misc, miscT = _inproj(x, lng, lnb, w_perm, ROW_TILE, N_META)
    a_out, a_out_m = _dsa(aq, iq, miscT, akx, avT, ik2, m, ng_a, topk)
    b_out, b_out_m = _gla(bq, bk, bv, misc, bog, m, wg, bg, ng_b)

    h1 = _outproj(x, a_out, b_out, lng, lnb, wo, r2(ln1_g[l]), r2(ln1_b[l]), ROW_TILE)
    b_out_m = jnp.zeros((META_ROWS, B_WIDTH), BF16).at[:CHUNK].set(b_out_m)
    h1_m = _outproj(x_meta, a_out_m[None], b_out_m[None], lng, lnb, wo, r2(ln1_g[l]), r2(ln1_b[l]), META_ROWS)
    return _ffn(h1, h1_m[0, N_META - HALO:N_META], wup, conv_w[l], r2(conv_b[l]), wdn, r2(ln2_g[l]), r2(ln2_b[l]), ROW_TILE)
```

```python
import functools

import jax
import jax.numpy as jnp
from jax import lax
from jax.experimental import pallas as pl
from jax.experimental.pallas import tpu as pltpu

F32 = jnp.float32
BF16 = jnp.bfloat16
I32 = jnp.int32

D_MODEL = 1024
CHUNK = 64
N_META = 16
A_HEADS = 8
A_WIDTH = 512
A_HEAD_DIM = 64
IDX_HEADS = 4
IDX_DIM = 64
TOPK_MAX = 256
B_HEADS = 4
B_WIDTH = 512
B_VAL_DIM = 128
B_KEY_DIM = 64
B_GATE_RANK = 16
B_GATE_TAU = 16.0
D_FF = 2816
CONV_W = 3
LN_EPS = 1e-5
RMS_EPS = 1e-6
DEPTH = 1
ALPHA = (2.0 * DEPTH) ** 0.25

LANES = 128
ROW_TILE = 512
QB = 256
KT = 256
MT = 64
GROUP_ROWS = MT + KT
META_ROWS = 256
GLA_BLOCK = 256
FF_TILE = 256
HALO = 16
INT_MIN = -(2 ** 31)
I16_MIN = -(2 ** 15)
N_SLOPE_PARTS = 3
VT_ROWS = 80
KX_COLS = 2 * A_WIDTH
LOG2E = 1.4426950408889634
ALIBI_C = tuple(2.0 ** (-8.0 * (h + 1) / A_HEADS) * LOG2E for h in range(A_HEADS))
VMEM_LIMIT = 56 * 1024 * 1024

_COLS = {}
_off = 0
for _name, _w in (("aq", 512), ("ak", 512), ("av", 512), ("iq", 256), ("bq", 256), ("bk", 256),
                  ("bv", 512), ("bog", 512), ("ik2", 128), ("misc", 128)):
    _COLS[_name] = (_off, _w)
    _off += _w
PROJ_COLS = _off
MISC_LR = 0
MISC_IW = 16


def _layer_norm(x, g, b):
    mu = jnp.mean(x, axis=-1, keepdims=True)
    xc = x - mu
    var = jnp.mean(xc * xc, axis=-1, keepdims=True)
    return xc * lax.rsqrt(var + LN_EPS) * g + b


def _dot(a, b):
    return jnp.dot(a, b, preferred_element_type=F32)


def _dot_nt(a, b):
    return lax.dot_general(a, b, (((1,), (1,)), ((), ())), preferred_element_type=F32)


def _dot_tn(a, b):
    return lax.dot_general(a, b, (((0,), (0,)), ((), ())), preferred_element_type=F32)


def _inproj_kernel(x_ref, g_ref, b_ref, w_ref, aq_ref, akx_ref, avT_ref, iq_ref, bq_ref, bk_ref,
                   bv_ref, bog_ref, ik2_ref, misc_ref, miscT_ref, *, pos0):
    h = _layer_norm(x_ref[0], g_ref[...], b_ref[...])
    hb = h.astype(BF16)
    tm = hb.shape[0]

    def proj(name):
        lo, w = _COLS[name]
        return _dot(hb, w_ref[:, lo:lo + w])

    aq_ref[0] = proj("aq").astype(BF16)
    ak = proj("ak")
    rows = lax.broadcasted_iota(I32, (tm, LANES), 0)
    lanes = lax.broadcasted_iota(I32, (tm, LANES), 1)
    pos = pos0 + pl.program_id(1) * tm + rows
    feat = jnp.where(lanes < 2 * N_SLOPE_PARTS, jnp.where(lanes % 2 == 0, pos >> 6, pos & 63), 0)
    feat = feat.astype(F32).astype(BF16)
    for p in range(A_HEADS // 2):
        akx_ref[0, :, 2 * p * LANES:(2 * p + 1) * LANES] = ak[:, p * LANES:(p + 1) * LANES].astype(BF16)
        akx_ref[0, :, (2 * p + 1) * LANES:(2 * p + 2) * LANES] = feat
    avT = proj("av").T
    ones = jnp.ones((VT_ROWS - A_HEAD_DIM, KT), BF16)
    for i in range(avT_ref.shape[1]):
        for hd in range(A_HEADS):
            avT_ref[0, i, hd * VT_ROWS:hd * VT_ROWS + A_HEAD_DIM, :] = (
                avT[hd * A_HEAD_DIM:(hd + 1) * A_HEAD_DIM, i * KT:(i + 1) * KT].astype(BF16))
            avT_ref[0, i, hd * VT_ROWS + A_HEAD_DIM:(hd + 1) * VT_ROWS, :] = ones
    iq_ref[0] = proj("iq").astype(BF16)
    bq_ref[0] = proj("bq").astype(BF16)
    bk_ref[0] = proj("bk").astype(BF16)
    bv_ref[0] = proj("bv").astype(BF16)
    bog_ref[0] = proj("bog")
    ik2_ref[0] = proj("ik2").astype(BF16)
    misc = proj("misc")
    misc_ref[0] = misc
    miscT_ref[0] = misc.T


def _inproj(x, ln_g, ln_b, w_perm, tm, pos0):
    nb, s, d = x.shape
    nt = s // tm
    row = lambda c: pl.BlockSpec((1, tm, c), lambda b, i: (b, i, 0))
    const = lambda shp: pl.BlockSpec(shp, lambda b, i: (0,) * len(shp))
    out_shape = (
        jax.ShapeDtypeStruct((nb, s, 512), BF16),
        jax.ShapeDtypeStruct((nb, s, KX_COLS), BF16),
        jax.ShapeDtypeStruct((nb, s // KT, A_HEADS * VT_ROWS, KT), BF16),
        jax.ShapeDtypeStruct((nb, s, 256), BF16),
        jax.ShapeDtypeStruct((nb, s, 256), BF16),
        jax.ShapeDtypeStruct((nb, s, 256), BF16),
        jax.ShapeDtypeStruct((nb, s, 512), BF16),
        jax.ShapeDtypeStruct((nb, s, 512), F32),
        jax.ShapeDtypeStruct((nb, s, 128), BF16),
        jax.ShapeDtypeStruct((nb, s, 128), F32),
        jax.ShapeDtypeStruct((nb, 128, s), F32),
    )
    out_specs = (
        row(512), row(KX_COLS),
        pl.BlockSpec((1, tm // KT, A_HEADS * VT_ROWS, KT), lambda b, i: (b, i, 0, 0)),
        row(256), row(256), row(256), row(512), row(512), row(128), row(128),
        pl.BlockSpec((1, 128, tm), lambda b, i: (b, 0, i)),
    )
    return pl.pallas_call(
        functools.partial(_inproj_kernel, pos0=pos0),
        grid=(nb, nt),
        in_specs=[row(d), const((1, d)), const((1, d)), const((d, PROJ_COLS))],
        out_specs=out_specs,
        out_shape=out_shape,
        compiler_params=pltpu.CompilerParams(
            dimension_semantics=("arbitrary", "arbitrary"), vmem_limit_bytes=VMEM_LIMIT),
        name="inproj",
    )(x, ln_g, ln_b, w_perm)


def _dsa_kernel(aq_ref, iq_ref, miscT_ref, akx_ref, avT_ref, ik2_ref,
                maq_ref, miq_ref, mmiscT_ref, makx_ref, mavT_ref, mik2_ref, qfeat_ref, ng_ref,
                out_ref, mout_ref,
                keys_s, khi_s, klo_s, qm_s, iqm_s, iw_s, m_s, acc_s, bias_s, lg_s, p_s, neq_s, *, topk):
    g = pl.program_id(1)
    is_meta = g == 0
    n_past = jnp.maximum(g - 1, 0)

    lane_half = lax.broadcasted_iota(I32, (1, LANES), 1) // A_HEAD_DIM

    def stage(aq, iq, miscT):
        for h in range(A_HEADS):
            p = h // 2
            qp = aq[:, p * LANES:(p + 1) * LANES]
            qm_s[h, :, 0:LANES] = jnp.where(lane_half == (h % 2), qp, jnp.zeros_like(qp))
            qm_s[h, :, LANES:2 * LANES] = jnp.broadcast_to(qfeat_ref[h:h + 1, :], (QB, LANES)).astype(BF16)
        for h in range(IDX_HEADS):
            p = h // 2
            qp = iq[:, p * LANES:(p + 1) * LANES]
            iqm_s[h] = jnp.where(lane_half == (h % 2), qp, jnp.zeros_like(qp))
        iw_s[...] = miscT[MISC_IW:MISC_IW + 8, :]

    @pl.when(is_meta)
    def _():
        stage(maq_ref[...], miq_ref[...], mmiscT_ref[...])

    @pl.when(g > 0)
    def _():
        stage(aq_ref[0], iq_ref[0], miscT_ref[0])

    lane = lax.broadcasted_iota(I32, (1, QB), 1)
    t_real = (g - 1) * QB + lane
    qpos = jnp.where(is_meta, lane, N_META + t_real)

    def keys_for_tile(ik2_t, allowed):
        s = None
        for h in range(IDX_HEADS):
            sh = _dot_nt(ik2_t, iqm_s[h])
            term = iw_s[h:h + 1, :] * jnp.maximum(sh, 0.0)
            s = term if s is None else s + term
        s = jnp.where(s == 0.0, 0.0, s)
        bits = lax.bitcast_convert_type(s, I32)
        key = bits ^ ((bits >> 31) & 0x7FFFFFFF)
        return key if allowed is None else jnp.where(allowed, key, INT_MIN)

    def split16(slot, r, key):
        keys_s[slot, 0:r, :] = key
        khi_s[slot, 0:r, :] = (key >> 16).astype(jnp.int16)
        klo_s[slot, 0:r, :] = ((key & 0xFFFF) + I16_MIN).astype(jnp.int16)

    mrow = lax.broadcasted_iota(I32, (MT, QB), 0)
    krow = lax.broadcasted_iota(I32, (KT, QB), 0)
    split16(0, MT, keys_for_tile(mik2_ref[0:MT, :], mrow < N_META))

    def fill_body(kt, c):
        split16(kt + 1, KT, keys_for_tile(ik2_ref[0, kt], None))
        return c

    lax.fori_loop(0, n_past, fill_body, 0)

    @pl.when(g > 0)
    def _():
        split16(g, KT, keys_for_tile(ik2_ref[0, g - 1], (krow >> 6) <= (lane >> 6)))

    def count(arr_s, pred):
        def cnt(tile):
            x = jnp.where(pred(tile), jnp.int16(1), jnp.int16(0))
            x = x.reshape(tile.shape[0] // 16, 16, QB)
            parts = [x[j] for j in range(x.shape[0])]
            while len(parts) > 1:
                parts = [parts[j] + parts[j + 1] for j in range(0, len(parts), 2)]
            return parts[0]

        acc = cnt(arr_s[0, 0:MT, :])
        acc = lax.fori_loop(0, g, lambda kt, a: a + cnt(arr_s[kt + 1]), acc)
        return jnp.sum(acc.astype(I32), axis=0, keepdims=True)

    def bisect16(arr_s, base):
        def body(i, ut):
            cand_u = ut | jnp.left_shift(jnp.int32(1), 15 - i)
            cand = (cand_u + I16_MIN).astype(jnp.int16)
            cnt = base + count(arr_s, lambda tile: tile >= cand)
            return jnp.where(cnt >= topk, cand_u, ut)

        return lax.fori_loop(0, 16, body, jnp.zeros((1, QB), I32))

    thi32 = bisect16(khi_s, 0) + I16_MIN
    thi = thi32.astype(jnp.int16)
    n_hi_gt = count(khi_s, lambda tile: tile > thi)

    def lo_mask(slot, r):
        klo_s[slot, 0:r, :] = jnp.where(khi_s[slot, 0:r, :] == thi, klo_s[slot, 0:r, :], jnp.int16(I16_MIN))

    lo_mask(0, MT)

    def lo_mask_body(kt, c):
        lo_mask(kt + 1, KT)
        return c

    lax.fori_loop(0, g, lo_mask_body, 0)
    tlo_u = bisect16(klo_s, n_hi_gt)
    tlo = (tlo_u + I16_MIN).astype(jnp.int16)
    thr = jnp.left_shift(thi32, 16) | tlo_u
    n_gt = n_hi_gt + count(klo_s, lambda tile: tile > tlo)
    need = jnp.where(thr == INT_MIN, 0, topk - n_gt).astype(F32)

    ninf = jnp.float32(-jnp.inf)
    m_s[...] = jnp.full(m_s.shape, -jnp.inf, F32)
    acc_s[...] = jnp.zeros(acc_s.shape, F32)

    def attend(subs):
        row0, offs, n_eq_after = 0, [], None
        for keys_t, _, _, _, n_eq_before in subs:
            if n_eq_before is None:
                n_eq_before = n_eq_after
            r = keys_t.shape[0]
            ri = lax.broadcasted_iota(I32, (r, r), 0)
            ci = lax.broadcasted_iota(I32, (r, r), 1)
            lower = jnp.where(ri > ci, 1.0, 0.0).astype(BF16)
            eq = keys_t == thr
            rank = _dot(lower, jnp.where(eq, 1.0, 0.0).astype(BF16)) + n_eq_before
            bias_s[row0:row0 + r, :] = jnp.where(
                keys_t > thr, 0.0, jnp.where(eq, jnp.where(rank < need, 0.0, ninf), ninf))
            n_eq_after = rank[r - 1:r, :] + jnp.where(eq[r - 1:r, :], 1.0, 0.0)
            offs.append((row0, r))
            row0 += r
        rows = row0
        m_safe, corr = [], []
        for h in range(A_HEADS):
            p = h // 2
            m_old = m_s[h]
            m_new = m_old
            for (_, kx_fn, _, after, _), (o, r) in zip(subs, offs):
                lg = _dot_nt(kx_fn(p), qm_s[h]) + bias_s[o:o + r, :]
                if after is not None:
                    lg = lg + ALIBI_C[h] * after
                lg_s[h, o:o + r, :] = lg
                m_new = jnp.maximum(m_new, jnp.max(lg, axis=0, keepdims=True))
            m_s[h] = m_new
            m_safe.append(jnp.where(m_new == ninf, 0.0, m_new))
            corr.append(jnp.exp2(m_old - m_safe[h]))
        for h in range(A_HEADS):
            p_s[h, 0:rows, :] = jnp.exp2(lg_s[h, 0:rows, :] - m_safe[h]).astype(BF16)
        for h in range(A_HEADS):
            acc = acc_s[h] * corr[h]
            for (_, _, vT_fn, _, _), (o, r) in zip(subs, offs):
                acc = acc + _dot(vT_fn(h), p_s[h, o:o + r, :])
            acc_s[h] = acc
        return n_eq_after

    def after_term(kpos):
        return jnp.minimum(2 * (qpos - kpos), 0).astype(F32)

    def meta_sub():
        return (keys_s[0, 0:MT, :],
                lambda p: makx_ref[0:MT, 2 * p * LANES:(2 * p + 2) * LANES],
                lambda h: mavT_ref[0, h * VT_ROWS:(h + 1) * VT_ROWS, 0:MT],
                after_term(mrow), jnp.zeros((1, QB), F32))

    def real_sub(kt, after, n_eq_before):
        return (keys_s[kt + 1],
                lambda p: akx_ref[0, kt, :, 2 * p * LANES:(2 * p + 2) * LANES],
                lambda h: avT_ref[0, kt, h * VT_ROWS:(h + 1) * VT_ROWS, :],
                after, n_eq_before)

    neq_s[...] = jnp.sum(jnp.where(keys_s[0, 0:MT, :] == thr, 1.0, 0.0), axis=0, keepdims=True)

    def past_body(kt, c):
        neq_s[...] = attend([real_sub(kt, None, neq_s[...])])
        return c

    lax.fori_loop(0, n_past, past_body, 0)

    @pl.when(g > 0)
    def _():
        attend([meta_sub(), real_sub(g - 1, after_term(N_META + (g - 1) * KT + krow), neq_s[...])])

    @pl.when(is_meta)
    def _():
        attend([meta_sub()])

    outs = []
    for h in range(A_HEADS):
        o = acc_s[h, 0:A_HEAD_DIM, :] / acc_s[h, A_HEAD_DIM:A_HEAD_DIM + 1, :]
        ms = jnp.mean(o * o, axis=0, keepdims=True)
        outs.append(o * lax.rsqrt(ms + RMS_EPS))
    res = (jnp.concatenate(outs, axis=0).T * ng_ref[...]).astype(BF16)

    @pl.when(is_meta)
    def _():
        mout_ref[...] = res

    @pl.when(g > 0)
    def _():
        out_ref[0] = res


def _alibi_query_features():
    rows = []
    for c in ALIBI_C:
        rest = jnp.float32(c)
        lanes = []
        for _ in range(N_SLOPE_PARTS):
            part = rest.astype(BF16).astype(F32)
            lanes += [part * CHUNK, part]
            rest = rest - part
        rows.append(jnp.stack(lanes + [jnp.float32(0.0)] * (LANES - len(lanes))))
    return jnp.stack(rows)


def _dsa(aq, iq, miscT, akx, avT, ik2, m, ng, topk):
    nb, s, _ = aq.shape
    nq = s // QB
    nkt = s // KT
    vt = A_HEADS * VT_ROWS
    qidx = lambda b, g: (b, jnp.maximum(g - 1, 0), 0)
    const = lambda shp: pl.BlockSpec(shp, lambda b, g: (0,) * len(shp))
    in_specs = [
        pl.BlockSpec((1, QB, 512), qidx),
        pl.BlockSpec((1, QB, 256), qidx),
        pl.BlockSpec((1, 128, QB), lambda b, g: (b, 0, jnp.maximum(g - 1, 0))),
        pl.BlockSpec((1, nkt, KT, KX_COLS), lambda b, g: (b, 0, 0, 0)),
        pl.BlockSpec((1, nkt, vt, KT), lambda b, g: (b, 0, 0, 0)),
        pl.BlockSpec((1, nkt, KT, 128), lambda b, g: (b, 0, 0, 0)),
        const((META_ROWS, 512)), const((META_ROWS, 256)), const((128, META_ROWS)),
        const((META_ROWS, KX_COLS)), const((1, vt, KT)), const((META_ROWS, 128)),
        const((A_HEADS, LANES)), const((1, 512)),
    ]
    out_shape = (jax.ShapeDtypeStruct((nb, s, 512), BF16), jax.ShapeDtypeStruct((META_ROWS, 512), BF16))
    out_specs = (pl.BlockSpec((1, QB, 512), qidx), const((META_ROWS, 512)))
    scratch = [
        pltpu.VMEM((nkt + 1, KT, QB), I32),
        pltpu.VMEM((nkt + 1, KT, QB), jnp.int16),
        pltpu.VMEM((nkt + 1, KT, QB), jnp.int16),
        pltpu.VMEM((A_HEADS, QB, 2 * LANES), BF16),
        pltpu.VMEM((IDX_HEADS, QB, LANES), BF16),
        pltpu.VMEM((8, QB), F32),
        pltpu.VMEM((A_HEADS, 1, QB), F32),
        pltpu.VMEM((A_HEADS, VT_ROWS, QB), F32),
        pltpu.VMEM((GROUP_ROWS, QB), F32),
        pltpu.VMEM((A_HEADS, GROUP_ROWS, QB), F32),
        pltpu.VMEM((A_HEADS, GROUP_ROWS, QB), BF16),
        pltpu.VMEM((1, QB), F32),
    ]
    return pl.pallas_call(
        functools.partial(_dsa_kernel, topk=topk),
        grid=(nb, nq + 1),
        in_specs=in_specs,
        out_specs=out_specs,
        out_shape=out_shape,
        scratch_shapes=scratch,
        compiler_params=pltpu.CompilerParams(
            dimension_semantics=("arbitrary", "arbitrary"), vmem_limit_bytes=VMEM_LIMIT),
        name="dsa",
    )(aq, iq, miscT, akx.reshape(nb, nkt, KT, KX_COLS), avT, ik2.reshape(nb, nkt, KT, 128),
      m["aq"], m["iq"], m["miscT"], m["akx"], m["avT"], m["ik2"], _alibi_query_features(), ng)


def _chunk_cumsum_rows(x):
    row_in_chunk = lax.broadcasted_iota(I32, x.shape, 0) & (CHUNK - 1)
    sh = 1
    while sh < CHUNK:
        x = x + jnp.where(row_in_chunk >= sh, pltpu.roll(x, sh, 0), 0.0)
        sh *= 2
    return x


def _rows_of_chunk(rows_per_chunk):
    return jnp.concatenate([jnp.broadcast_to(r, (CHUNK, r.shape[1])) for r in rows_per_chunk], axis=0)


def _gla_block(q, k, v, lr, og, wg, bg, ng, st_s, valid_rows):
    nrow = q.shape[0]
    nch = nrow // CHUNK
    z = _dot(lr.astype(BF16), wg) + bg
    logg = (jnp.minimum(z, 0.0) - jnp.log1p(jnp.exp(-jnp.abs(z)))) / B_GATE_TAU
    if valid_rows is not None:
        rows = lax.broadcasted_iota(I32, logg.shape, 0)
        logg = jnp.where(rows < valid_rows, logg, 0.0)
    b = _chunk_cumsum_rows(logg)
    b_last_rows = [b[(c + 1) * CHUNK - 1:(c + 1) * CHUNK, :] for c in range(nch)]
    b_mid = _rows_of_chunk([b[c * CHUNK + CHUNK // 2 - 1:c * CHUNK + CHUNK // 2, :] for c in range(nch)])
    b_last = _rows_of_chunk(b_last_rows)
    q_in = q * jnp.exp(b)
    q_ic = q * jnp.exp(b - b_mid)
    k_ic = k * jnp.exp(b_mid - b)
    k_st = k * jnp.exp(b_last - b)
    decay = [jnp.exp(r) for r in b_last_rows]

    lane_half = lax.broadcasted_iota(I32, (1, LANES), 1) // B_KEY_DIM
    ri = lax.broadcasted_iota(I32, (nrow, nrow), 0)
    ci = lax.broadcasted_iota(I32, (nrow, nrow), 1)
    causal = ((ri >> 6) == (ci >> 6)) & (ci <= ri)
    outs = []
    for p in range(B_HEADS // 2):
        sl = slice(p * LANES, (p + 1) * LANES)
        heads = (2 * p, 2 * p + 1)
        v_h = [v[:, h * B_VAL_DIM:(h + 1) * B_VAL_DIM] for h in heads]
        k_st_h = [jnp.where(lane_half == hh, k_st[:, sl], 0.0).astype(BF16) for hh in range(2)]
        states = [st_s[p]]
        for c in range(nch):
            rs = slice(c * CHUNK, (c + 1) * CHUNK)
            upd = _dot_tn(v_h[0][rs], k_st_h[0][rs]) + _dot_tn(v_h[1][rs], k_st_h[1][rs])
            states.append(states[c] * decay[c][:, sl] + upd)
        st_s[p] = states[nch]
        k_ic_p = k_ic[:, sl].astype(BF16)
        for hh, h in enumerate(heads):
            hm = lane_half == hh
            a = _dot_nt(jnp.where(hm, q_ic[:, sl], 0.0).astype(BF16), k_ic_p)
            o = _dot(jnp.where(causal, a, 0.0).astype(BF16), v_h[hh])
            q_in_h = jnp.where(hm, q_in[:, sl], 0.0).astype(BF16)
            o = o + jnp.concatenate(
                [_dot_nt(q_in_h[c * CHUNK:(c + 1) * CHUNK], states[c].astype(BF16)) for c in range(nch)], axis=0)
            ms = jnp.mean(o * o, axis=-1, keepdims=True)
            on = o * lax.rsqrt(ms + RMS_EPS) * ng[:, h * B_VAL_DIM:(h + 1) * B_VAL_DIM]
            og_h = og[:, h * B_VAL_DIM:(h + 1) * B_VAL_DIM]
            outs.append(on * (og_h * jax.nn.sigmoid(og_h)))
    return outs


def _gla_kernel(bq_ref, bk_ref, bv_ref, misc_ref, bog_ref,
                mbq_ref, mbk_ref, mbv_ref, mmisc_ref, mbog_ref,
                wg_ref, bg_ref, ng_ref, out_ref, mout_ref, st_s):
    j = pl.program_id(1)
    wg = wg_ref[...]
    bg = bg_ref[...]
    ng = ng_ref[...]

    @pl.when(j == 0)
    def _():
        st_s[...] = jnp.zeros(st_s.shape, F32)
        outs = _gla_block(mbq_ref[0:CHUNK, :].astype(F32), mbk_ref[0:CHUNK, :].astype(F32),
                          mbv_ref[0:CHUNK, :], mmisc_ref[0:CHUNK, MISC_LR:MISC_LR + B_GATE_RANK],
                          mbog_ref[0:CHUNK, :], wg, bg, ng, st_s, N_META)
        for h, o in enumerate(outs):
            mout_ref[:, h * B_VAL_DIM:(h + 1) * B_VAL_DIM] = o.astype(BF16)

    outs = _gla_block(bq_ref[0].astype(F32), bk_ref[0].astype(F32), bv_ref[0],
                      misc_ref[0, :, MISC_LR:MISC_LR + B_GATE_RANK], bog_ref[0], wg, bg, ng, st_s, None)
    for h, o in enumerate(outs):
        out_ref[0, :, h * B_VAL_DIM:(h + 1) * B_VAL_DIM] = o.astype(BF16)


def _gla(bq, bk, bv, misc, bog, m, wg, bg, ng):
    nb, s, _ = bq.shape
    row = lambda c: pl.BlockSpec((1, GLA_BLOCK, c), lambda b, j: (b, j, 0))
    const = lambda shp: pl.BlockSpec(shp, lambda b, j: (0,) * len(shp))
    return pl.pallas_call(
        _gla_kernel,
        grid=(nb, s // GLA_BLOCK),
        in_specs=[row(256), row(256), row(512), row(128), row(512),
                  const((META_ROWS, 256)), const((META_ROWS, 256)), const((META_ROWS, 512)),
                  const((META_ROWS, 128)), const((META_ROWS, 512)),
                  const((B_GATE_RANK, 256)), const((1, 256)), const((1, 512))],
        out_specs=(row(512), const((CHUNK, 512))),
        out_shape=(jax.ShapeDtypeStruct((nb, s, 512), BF16), jax.ShapeDtypeStruct((CHUNK, 512), BF16)),
        scratch_shapes=[pltpu.VMEM((B_HEADS // 2, LANES, LANES), F32)],
        compiler_params=pltpu.CompilerParams(
            dimension_semantics=("arbitrary", "arbitrary"), vmem_limit_bytes=VMEM_LIMIT),
        name="gla",
    )(bq, bk, bv, misc, bog, m["bq"], m["bk"], m["bv"], m["misc"], m["bog"], wg, bg, ng)


def _outproj_kernel(x_ref, a_ref, b_ref, lng_ref, lnb_ref, wo_ref, g1_ref, b1_ref, out_ref):
    h = _layer_norm(x_ref[0], lng_ref[...], lnb_ref[...])
    mixed = _dot(a_ref[0], wo_ref[0:A_WIDTH, :]) + _dot(b_ref[0], wo_ref[A_WIDTH:, :])
    out_ref[0] = _layer_norm(ALPHA * h + mixed, g1_ref[...], b1_ref[...])


def _outproj(x, a, b, lng, lnb, wo, g1, b1, tm):
    nb, s, d = x.shape
    row = lambda c: pl.BlockSpec((1, tm, c), lambda bb, i: (bb, i, 0))
    const = lambda shp: pl.BlockSpec(shp, lambda bb, i: (0,) * len(shp))
    return pl.pallas_call(
        _outproj_kernel,
        grid=(nb, s // tm),
        in_specs=[row(d), row(512), row(512), const((1, d)), const((1, d)), const((d, d)),
                  const((1, d)), const((1, d))],
        out_specs=row(d),
        out_shape=jax.ShapeDtypeStruct((nb, s, d), F32),
        compiler_params=pltpu.CompilerParams(
            dimension_semantics=("arbitrary", "arbitrary"), vmem_limit_bytes=VMEM_LIMIT),
        name="outproj",
    )(x, a, b, lng, lnb, wo, g1, b1)


def _gelu_tanh(x):
    c = 0.7978845608028654
    return 0.5 * x * (1.0 + jnp.tanh(c * (x + 0.044715 * (x * x * x))))


def _ffn_kernel(x_ref, xh_ref, a_ref, ah_ref, b_ref, bh_ref, hm_ref, lng_ref, lnb_ref, wo_ref, g1_ref, b1_ref,
                wup_ref, cw_ref, cb_ref, wdn_ref, g2_ref, b2_ref, out_ref, y_s):
    j = pl.program_id(1)
    x = jnp.concatenate([xh_ref[0], x_ref[0]], axis=0)
    a_mix = jnp.concatenate([ah_ref[0], a_ref[0]], axis=0)
    b_mix = jnp.concatenate([bh_ref[0], b_ref[0]], axis=0)
    mixed = _dot(a_mix, wo_ref[0:A_WIDTH, :]) + _dot(b_mix, wo_ref[A_WIDTH:, :])
    h1 = _layer_norm(ALPHA * _layer_norm(x, lng_ref[...], lnb_ref[...]) + mixed, g1_ref[...], b1_ref[...])
    h = h1[HALO:, :]
    halo = jnp.where(j == 0, hm_ref[...], h1[0:HALO, :])
    hb = jnp.concatenate([halo, h], axis=0).astype(BF16)
    for f in range(D_FF // FF_TILE):
        fs = slice(f * FF_TILE, (f + 1) * FF_TILE)
        a = _dot(hb, wup_ref[:, fs])
        gate = _dot(hb[HALO:, :], wup_ref[:, D_FF + f * FF_TILE:D_FF + (f + 1) * FF_TILE])
        cw = cw_ref[:, fs]
        conv = cb_ref[:, fs] + cw[CONV_W - 1:CONV_W, :] * a[HALO:, :]
        for back in range(1, CONV_W):
            tap = cw[CONV_W - 1 - back:CONV_W - back, :]
            conv = conv + tap * pltpu.roll(a, back, 0)[HALO:, :]
        y_s[:, fs] = (_gelu_tanh(conv) * gate).astype(BF16)
    ffn = _dot(y_s[...], wdn_ref[...])
    out_ref[0] = _layer_norm(ALPHA * h + ffn, g2_ref[...], b2_ref[...])


def _ffn(x, a, b, h1_meta, lng, lnb, wo, g1, b1, wup, cw, cb, wdn, g2, b2, tm):
    nb, s, d = x.shape
    per = tm // HALO
    const = lambda shp: pl.BlockSpec(shp, lambda bb, j: (0,) * len(shp), pipeline_mode=pl.Buffered(1))
    main = lambda c: pl.BlockSpec((1, tm, c), lambda bb, j: (bb, j, 0))
    halo = lambda c: pl.BlockSpec((1, HALO, c), lambda bb, j: (bb, jnp.maximum(j * per - 1, 0), 0))
    return pl.pallas_call(
        _ffn_kernel,
        grid=(nb, s // tm),
        in_specs=[main(d), halo(d), main(A_WIDTH), halo(A_WIDTH), main(B_WIDTH), halo(B_WIDTH),
                  const((HALO, d)), const((1, d)), const((1, d)), const((d, d)), const((1, d)), const((1, d)),
                  const((d, 2 * D_FF)), const((CONV_W, D_FF)), const((1, D_FF)),
                  const((D_FF, d)), const((1, d)), const((1, d))],
        out_specs=main(d),
        out_shape=jax.ShapeDtypeStruct((nb, s, d), F32),
        scratch_shapes=[pltpu.VMEM((tm, D_FF), BF16)],
        compiler_params=pltpu.CompilerParams(
            dimension_semantics=("arbitrary", "arbitrary"), vmem_limit_bytes=VMEM_LIMIT),
        name="ffn",
    )(x, x, a, a, b, b, h1_meta, lng, lnb, wo, g1, b1, wup, cw, cb, wdn, g2, b2)


def _permute_w_in(w):
    o = 0
    parts = {}
    for name, n in (("a_q", 512), ("a_k", 512), ("a_v", 512), ("i_q", 256), ("i_k", 64), ("i_w", 4),
                    ("b_q", 256), ("b_k", 256), ("b_v", 512), ("b_lr", 16), ("b_og", 512)):
        parts[name] = w[:, o:o + n]
        o += n
    zeros = jnp.zeros((w.shape[0], 128 - B_GATE_RANK - IDX_HEADS), w.dtype)
    cols = [parts["a_q"] * (A_HEAD_DIM ** -0.5 * LOG2E), parts["a_k"], parts["a_v"], parts["i_q"],
            parts["b_q"] * (B_KEY_DIM ** -0.5), parts["b_k"], parts["b_v"], parts["b_og"],
            parts["i_k"], parts["i_k"],
            parts["b_lr"], parts["i_w"] * (IDX_HEADS ** -0.5) * (IDX_DIM ** -0.5), zeros]
    return jnp.concatenate(cols, axis=1).astype(BF16)


def kernel(x, meta, ln_in_g, ln_in_b, w_in, w_gate_b, b_gate_b, attn_norm_g, gla_norm_g, w_out,
           ln1_g, ln1_b, w_up, conv_w, conv_b, w_down, ln2_g, ln2_b):
    nb, s, d = x.shape
    assert d == D_MODEL and w_in.shape[0] == DEPTH == 1
    assert s % ROW_TILE == 0 and s % KT == 0 and s % GLA_BLOCK == 0
    topk = min(TOPK_MAX, s // 4)
    l = 0
    r2 = lambda v: v.reshape(1, -1)
    lng, lnb = r2(ln_in_g), r2(ln_in_b)
    w_perm = _permute_w_in(w_in[l])
    wo = w_out[l].astype(BF16)
    wup = w_up[l].astype(BF16)
    wdn = w_down[l].astype(BF16)
    wg = w_gate_b[l].astype(BF16)
    bg = r2(b_gate_b[l])
    ng_a = r2(attn_norm_g[l])
    ng_b = r2(gla_norm_g[l])

    x_meta = jnp.zeros((1, META_ROWS, d), x.dtype).at[0, :N_META].set(meta.astype(x.dtype))
    names = ("aq", "akx", "avT", "iq", "bq", "bk", "bv", "bog", "ik2", "misc", "miscT")
    mo = dict(zip(names, _inproj(x_meta, lng, lnb, w_perm, META_ROWS, 0)))
    keep = (jnp.arange(META_ROWS) < N_META)
    m = {}
    for n in names:
        v = mo[n][0]
        if n in ("avT", "miscT"):
            m[n] = jnp.where(keep[None, :], v, jnp.zeros_like(v)).reshape((1,) * (n == "avT") + v.shape[-2:])
        else:
            m[n] = jnp.where(keep[:, None], v, jnp.zeros_like(v))

    aq, akx, avT, iq, bq, bk, bv, bog, ik2, misc, miscT = _inproj(x, lng, lnb, w_perm, ROW_TILE, N_META)
    a_out, a_out_m = _dsa(aq, iq, miscT, akx, avT, ik2, m, ng_a, topk)
    b_out, b_out_m = _gla(bq, bk, bv, misc, bog, m, wg, bg, ng_b)

    g1, b1 = r2(ln1_g[l]), r2(ln1_b[l])
    b_out_m = jnp.zeros((META_ROWS, B_WIDTH), BF16).at[:CHUNK].set(b_out_m)
    h1_m = _outproj(x_meta, a_out_m[None], b_out_m[None], lng, lnb, wo, g1, b1, META_ROWS)
    return _ffn(x, a_out, b_out, h1_m[0, N_META - HALO:N_META], lng, lnb, wo, g1, b1,
                wup, conv_w[l], r2(conv_b[l]), wdn, r2(ln2_g[l]), r2(ln2_b[l]), ROW_TILE)
```

```python
import functools

import jax
import jax.numpy as jnp
from jax import lax
from jax.experimental import pallas as pl
from jax.experimental.pallas import tpu as pltpu

F32 = jnp.float32
BF16 = jnp.bfloat16
I32 = jnp.int32

D_MODEL = 1024
CHUNK = 64
N_META = 16
A_HEADS = 8
A_WIDTH = 512
A_HEAD_DIM = 64
IDX_HEADS = 4
IDX_DIM = 64
TOPK_MAX = 256
B_HEADS = 4
B_WIDTH = 512
B_VAL_DIM = 128
B_KEY_DIM = 64
B_GATE_RANK = 16
B_GATE_TAU = 16.0
D_FF = 2816
CONV_W = 3
LN_EPS = 1e-5
RMS_EPS = 1e-6
DEPTH = 1
ALPHA = (2.0 * DEPTH) ** 0.25

LANES = 128
ROW_TILE = 512
QB = 256
KT = 256
MT = 64
GROUP_ROWS = MT + KT
META_ROWS = 256
GLA_BLOCK = 256
FF_TILE = 256
HALO = 16
U_NEG_INF, U_POS_INF = 0x007F, 0xFF80
N_SLOPE_PARTS = 3
VT_ROWS = 80
KX_COLS = 2 * A_WIDTH
LOG2E = 1.4426950408889634
ALIBI_C = tuple(2.0 ** (-8.0 * (h + 1) / A_HEADS) * LOG2E for h in range(A_HEADS))
VMEM_LIMIT = 56 * 1024 * 1024

_COLS = {}
_off = 0
for _name, _w in (("aq", 512), ("ak", 512), ("av", 512), ("iq", 256), ("bq", 256), ("bk", 256),
                  ("bv", 512), ("bog", 512), ("ik2", 128), ("misc", 128)):
    _COLS[_name] = (_off, _w)
    _off += _w
PROJ_COLS = _off
MISC_LR = 0
MISC_IW = 16


def _layer_norm(x, g, b):
    mu = jnp.mean(x, axis=-1, keepdims=True)
    xc = x - mu
    var = jnp.mean(xc * xc, axis=-1, keepdims=True)
    return xc * lax.rsqrt(var + LN_EPS) * g + b


def _dot(a, b):
    return jnp.dot(a, b, preferred_element_type=F32)


def _dot_nt(a, b):
    return lax.dot_general(a, b, (((1,), (1,)), ((), ())), preferred_element_type=F32)


def _dot_tn(a, b):
    return lax.dot_general(a, b, (((0,), (0,)), ((), ())), preferred_element_type=F32)


def _inproj_kernel(x_ref, g_ref, b_ref, w_ref, aq_ref, akx_ref, avT_ref, iq_ref, bq_ref, bk_ref,
                   bv_ref, bog_ref, ik2_ref, misc_ref, miscT_ref, *, pos0):
    h = _layer_norm(x_ref[0], g_ref[...], b_ref[...])
    hb = h.astype(BF16)
    tm = hb.shape[0]

    def proj(name):
        lo, w = _COLS[name]
        return _dot(hb, w_ref[:, lo:lo + w])

    aq_ref[0] = proj("aq").astype(BF16)
    ak = proj("ak")
    rows = lax.broadcasted_iota(I32, (tm, LANES), 0)
    lanes = lax.broadcasted_iota(I32, (tm, LANES), 1)
    pos = pos0 + pl.program_id(1) * tm + rows
    feat = jnp.where(lanes < 2 * N_SLOPE_PARTS, jnp.where(lanes % 2 == 0, pos >> 6, pos & 63), 0)
    feat = feat.astype(F32).astype(BF16)
    for p in range(A_HEADS // 2):
        akx_ref[0, :, 2 * p * LANES:(2 * p + 1) * LANES] = ak[:, p * LANES:(p + 1) * LANES].astype(BF16)
        akx_ref[0, :, (2 * p + 1) * LANES:(2 * p + 2) * LANES] = feat
    avT = proj("av").T
    ones = jnp.ones((VT_ROWS - A_HEAD_DIM, KT), BF16)
    for i in range(avT_ref.shape[1]):
        for hd in range(A_HEADS):
            avT_ref[0, i, hd * VT_ROWS:hd * VT_ROWS + A_HEAD_DIM, :] = (
                avT[hd * A_HEAD_DIM:(hd + 1) * A_HEAD_DIM, i * KT:(i + 1) * KT].astype(BF16))
            avT_ref[0, i, hd * VT_ROWS + A_HEAD_DIM:(hd + 1) * VT_ROWS, :] = ones
    iq_ref[0] = proj("iq").astype(BF16)
    bq_ref[0] = proj("bq").astype(BF16)
    bk_ref[0] = proj("bk").astype(BF16)
    bv_ref[0] = proj("bv").astype(BF16)
    bog_ref[0] = proj("bog")
    ik2_ref[0] = proj("ik2").astype(BF16)
    misc = proj("misc")
    misc_ref[0] = misc
    miscT_ref[0] = misc.T


def _inproj(x, ln_g, ln_b, w_perm, tm, pos0):
    nb, s, d = x.shape
    nt = s // tm
    row = lambda c: pl.BlockSpec((1, tm, c), lambda b, i: (b, i, 0))
    const = lambda shp: pl.BlockSpec(shp, lambda b, i: (0,) * len(shp))
    out_shape = (
        jax.ShapeDtypeStruct((nb, s, 512), BF16),
        jax.ShapeDtypeStruct((nb, s, KX_COLS), BF16),
        jax.ShapeDtypeStruct((nb, s // KT, A_HEADS * VT_ROWS, KT), BF16),
        jax.ShapeDtypeStruct((nb, s, 256), BF16),
        jax.ShapeDtypeStruct((nb, s, 256), BF16),
        jax.ShapeDtypeStruct((nb, s, 256), BF16),
        jax.ShapeDtypeStruct((nb, s, 512), BF16),
        jax.ShapeDtypeStruct((nb, s, 512), F32),
        jax.ShapeDtypeStruct((nb, s, 128), BF16),
        jax.ShapeDtypeStruct((nb, s, 128), F32),
        jax.ShapeDtypeStruct((nb, 128, s), F32),
    )
    out_specs = (
        row(512), row(KX_COLS),
        pl.BlockSpec((1, tm // KT, A_HEADS * VT_ROWS, KT), lambda b, i: (b, i, 0, 0)),
        row(256), row(256), row(256), row(512), row(512), row(128), row(128),
        pl.BlockSpec((1, 128, tm), lambda b, i: (b, 0, i)),
    )
    return pl.pallas_call(
        functools.partial(_inproj_kernel, pos0=pos0),
        grid=(nb, nt),
        in_specs=[row(d), const((1, d)), const((1, d)), const((d, PROJ_COLS))],
        out_specs=out_specs,
        out_shape=out_shape,
        compiler_params=pltpu.CompilerParams(
            dimension_semantics=("arbitrary", "arbitrary"), vmem_limit_bytes=VMEM_LIMIT),
        name="inproj",
    )(x, ln_g, ln_b, w_perm)


def _dsa_kernel(aq_ref, iq_ref, miscT_ref, akx_ref, avT_ref, ik2_ref,
                maq_ref, miq_ref, mmiscT_ref, makx_ref, mavT_ref, mik2_ref, qfeat_ref, ng_ref,
                out_ref, mout_ref,
                sc_s, scb_s, qm_s, iqm_s, iw_s, m_s, acc_s, bias_s, lg_s, p_s, neq_s, shift_s, corr_s,
                *, topk):
    g = pl.program_id(1)
    is_meta = g == 0
    n_past = jnp.maximum(g - 1, 0)

    lane_half = lax.broadcasted_iota(I32, (1, LANES), 1) // A_HEAD_DIM

    def stage(aq, iq, miscT):
        for h in range(A_HEADS):
            p = h // 2
            qp = aq[:, p * LANES:(p + 1) * LANES]
            qm_s[h, :, 0:LANES] = jnp.where(lane_half == (h % 2), qp, jnp.zeros_like(qp))
            qm_s[h, :, LANES:2 * LANES] = jnp.broadcast_to(qfeat_ref[h:h + 1, :], (QB, LANES)).astype(BF16)
        for h in range(IDX_HEADS):
            p = h // 2
            qp = iq[:, p * LANES:(p + 1) * LANES]
            iqm_s[h] = jnp.where(lane_half == (h % 2), qp, jnp.zeros_like(qp))
        iw_s[...] = miscT[MISC_IW:MISC_IW + 8, :]

    @pl.when(is_meta)
    def _():
        stage(maq_ref[...], miq_ref[...], mmiscT_ref[...])

    @pl.when(g > 0)
    def _():
        stage(aq_ref[0], iq_ref[0], miscT_ref[0])

    lane = lax.broadcasted_iota(I32, (1, QB), 1)
    t_real = (g - 1) * QB + lane
    qpos = jnp.where(is_meta, lane, N_META + t_real)

    ninf = jnp.float32(-jnp.inf)

    def scores_for_tile(ik2_t, allowed):
        s = None
        for h in range(IDX_HEADS):
            sh = _dot_nt(ik2_t, iqm_s[h])
            term = iw_s[h:h + 1, :] * jnp.maximum(sh, 0.0)
            s = term if s is None else s + term
        s = jnp.where(s == 0.0, 0.0, s)
        return s if allowed is None else jnp.where(allowed, s, ninf)

    def store_scores(slot, r, s):
        sc_s[slot, 0:r, :] = s
        scb_s[slot, 0:r, :] = s.astype(BF16)

    mrow = lax.broadcasted_iota(I32, (MT, QB), 0)
    krow = lax.broadcasted_iota(I32, (KT, QB), 0)
    store_scores(0, MT, scores_for_tile(mik2_ref[0:MT, :], mrow < N_META))

    def fill_past(kt):
        store_scores(kt + 1, KT, scores_for_tile(ik2_ref[0, kt], None))

    def fill_pair(i, c):
        fill_past(2 * i)
        fill_past(2 * i + 1)
        return c

    lax.fori_loop(0, n_past // 2, fill_pair, 0)

    @pl.when(n_past % 2 == 1)
    def _():
        fill_past(n_past - 1)

    @pl.when(g > 0)
    def _():
        store_scores(g, KT, scores_for_tile(ik2_ref[0, g - 1], (krow >> 6) <= (lane >> 6)))

    def count(arr_s, pred, pack):
        acc_t = jnp.int16 if pack == 16 else I32

        def cnt(tile):
            x = jnp.where(pred(tile), jnp.ones((), acc_t), jnp.zeros((), acc_t))
            x = x.reshape(tile.shape[0] // pack, pack, QB)
            parts = [x[j] for j in range(x.shape[0])]
            while len(parts) > 1:
                parts = [parts[j] + parts[j + 1] for j in range(0, len(parts), 2)]
            return parts[0]

        acc = cnt(arr_s[0, 0:MT, :])
        acc = lax.fori_loop(0, g, lambda kt, a: a + cnt(arr_s[kt + 1]), acc)
        return jnp.sum(acc.astype(I32), axis=0, keepdims=True)

    def ordered_to_bits(k, sign_bit, low_mask):
        return jnp.where(k >= sign_bit, k ^ sign_bit, k ^ low_mask)

    def coarse_body(i, u):
        cand_u = u | jnp.left_shift(jnp.int32(1), 15 - i)
        finite = (cand_u >= U_NEG_INF) & (cand_u <= U_POS_INF)
        bits = ordered_to_bits(jnp.clip(cand_u, U_NEG_INF, U_POS_INF), 0x8000, 0xFFFF)
        cand = lax.bitcast_convert_type(jnp.left_shift(bits, 16), F32).astype(BF16)
        cnt = count(scb_s, lambda tile: tile >= cand, 16)
        return jnp.where(finite & (cnt >= topk), cand_u, u)

    u1 = lax.fori_loop(0, 16, coarse_body, jnp.zeros((1, QB), I32))
    few = u1 == 0
    t1_bits = jnp.left_shift(ordered_to_bits(jnp.where(few, 0x8000, u1), 0x8000, 0xFFFF), 16)
    base = (t1_bits ^ ((t1_bits >> 31) & 0x7FFFFFFF)) - 0x8000

    def key_to_f32(k):
        return lax.bitcast_convert_type(k ^ ((k >> 31) & 0x7FFFFFFF), F32)

    def fine_body(i, o):
        cand_o = o | jnp.left_shift(jnp.int32(1), 16 - i)
        cand = key_to_f32(base + cand_o)
        cnt = count(sc_s, lambda tile: tile >= cand, 8)
        return jnp.where(cnt >= topk, cand_o, o)

    o2 = lax.fori_loop(0, 17, fine_body, jnp.zeros((1, QB), I32))
    thr = jnp.where(few, ninf, key_to_f32(base + o2))
    n_gt = count(sc_s, lambda tile: tile > thr, 8)
    need = jnp.where(few, 0, topk - n_gt).astype(F32)

    m_s[...] = jnp.full(m_s.shape, -jnp.inf, F32)
    acc_s[...] = jnp.zeros(acc_s.shape, F32)

    def offsets(subs):
        offs, row0 = [], 0
        for sub in subs:
            offs.append((row0, sub[0].shape[0]))
            row0 += sub[0].shape[0]
        return offs, row0

    def bias_stage(subs):
        offs, _ = offsets(subs)
        n_eq_after = None
        for (sc_t, _, _, _, n_eq_before), (o, r) in zip(subs, offs):
            if n_eq_before is None:
                n_eq_before = n_eq_after
            ri = lax.broadcasted_iota(I32, (r, r), 0)
            ci = lax.broadcasted_iota(I32, (r, r), 1)
            lower = jnp.where(ri > ci, 1.0, 0.0).astype(BF16)
            eq = sc_t == thr
            rank = _dot(lower, jnp.where(eq, 1.0, 0.0).astype(BF16)) + n_eq_before
            bias_s[o:o + r, :] = jnp.where(
                sc_t > thr, 0.0, jnp.where(eq, jnp.where(rank < need, 0.0, ninf), ninf))
            n_eq_after = rank[r - 1:r, :] + jnp.where(eq[r - 1:r, :], 1.0, 0.0)
        return n_eq_after

    def logits_head(subs, h):
        offs, _ = offsets(subs)
        m_old = m_s[h]
        m_new = m_old
        for (_, kx_fn, _, after, _), (o, r) in zip(subs, offs):
            lg = _dot_nt(kx_fn(h // 2), qm_s[h]) + bias_s[o:o + r, :]
            if after is not None:
                lg = lg + ALIBI_C[h] * after
            lg_s[h, o:o + r, :] = lg
            m_new = jnp.maximum(m_new, jnp.max(lg, axis=0, keepdims=True))
        m_s[h] = m_new
        m_safe = jnp.where(m_new == ninf, 0.0, m_new)
        shift_s[h] = m_safe
        corr_s[h] = jnp.exp2(m_old - m_safe)

    def softmax_pv_head(subs, h):
        offs, rows = offsets(subs)
        p_s[h, 0:rows, :] = jnp.exp2(lg_s[h, 0:rows, :] - shift_s[h]).astype(BF16)
        acc = acc_s[h] * corr_s[h]
        for (_, _, vT_fn, _, _), (o, r) in zip(subs, offs):
            acc = acc + _dot(vT_fn(h), p_s[h, o:o + r, :])
        acc_s[h] = acc

    def run_stages(done, nxt):
        n_eq_after = bias_stage(nxt) if nxt is not None else None
        for h in range(A_HEADS):
            if done is not None:
                softmax_pv_head(done, h)
            if nxt is not None:
                logits_head(nxt, h)
        return n_eq_after

    def after_term(kpos):
        return jnp.minimum(2 * (qpos - kpos), 0).astype(F32)

    def meta_sub():
        return (sc_s[0, 0:MT, :],
                lambda p: makx_ref[0:MT, 2 * p * LANES:(2 * p + 2) * LANES],
                lambda h: mavT_ref[0, h * VT_ROWS:(h + 1) * VT_ROWS, 0:MT],
                after_term(mrow), jnp.zeros((1, QB), F32))

    def real_sub(kt, after, n_eq_before):
        return (sc_s[kt + 1],
                lambda p: akx_ref[0, kt, :, 2 * p * LANES:(2 * p + 2) * LANES],
                lambda h: avT_ref[0, kt, h * VT_ROWS:(h + 1) * VT_ROWS, :],
                after, n_eq_before)

    neq_s[...] = jnp.sum(jnp.where(sc_s[0, 0:MT, :] == thr, 1.0, 0.0), axis=0, keepdims=True)

    def past(kt):
        return [real_sub(kt, None, neq_s[...])]

    def last_group():
        return [meta_sub(), real_sub(g - 1, after_term(N_META + (g - 1) * KT + krow), neq_s[...])]

    @pl.when(n_past > 0)
    def _():
        neq_s[...] = run_stages(None, past(0))

    def pipe_body(i, c):
        neq_s[...] = run_stages(past(i), past(i + 1))
        return c

    lax.fori_loop(0, jnp.maximum(n_past - 1, 0), pipe_body, 0)

    @pl.when(n_past > 0)
    def _():
        run_stages(past(n_past - 1), last_group())

    @pl.when(g == 1)
    def _():
        run_stages(None, last_group())

    @pl.when(g > 0)
    def _():
        run_stages(last_group(), None)

    @pl.when(is_meta)
    def _():
        run_stages(None, [meta_sub()])
        run_stages([meta_sub()], None)

    outs = []
    for h in range(A_HEADS):
        o = acc_s[h, 0:A_HEAD_DIM, :] / acc_s[h, A_HEAD_DIM:A_HEAD_DIM + 1, :]
        ms = jnp.mean(o * o, axis=0, keepdims=True)
        outs.append(o * lax.rsqrt(ms + RMS_EPS))
    res = (jnp.concatenate(outs, axis=0).T * ng_ref[...]).astype(BF16)

    @pl.when(is_meta)
    def _():
        mout_ref[...] = res

    @pl.when(g > 0)
    def _():
        out_ref[0] = res


def _alibi_query_features():
    rows = []
    for c in ALIBI_C:
        rest = jnp.float32(c)
        lanes = []
        for _ in range(N_SLOPE_PARTS):
            part = rest.astype(BF16).astype(F32)
            lanes += [part * CHUNK, part]
            rest = rest - part
        rows.append(jnp.stack(lanes + [jnp.float32(0.0)] * (LANES - len(lanes))))
    return jnp.stack(rows)


def _dsa(aq, iq, miscT, akx, avT, ik2, m, ng, topk):
    nb, s, _ = aq.shape
    nq = s // QB
    nkt = s // KT
    vt = A_HEADS * VT_ROWS
    qidx = lambda b, g: (b, jnp.maximum(g - 1, 0), 0)
    const = lambda shp: pl.BlockSpec(shp, lambda b, g: (0,) * len(shp))
    in_specs = [
        pl.BlockSpec((1, QB, 512), qidx),
        pl.BlockSpec((1, QB, 256), qidx),
        pl.BlockSpec((1, 128, QB), lambda b, g: (b, 0, jnp.maximum(g - 1, 0))),
        pl.BlockSpec((1, nkt, KT, KX_COLS), lambda b, g: (b, 0, 0, 0)),
        pl.BlockSpec((1, nkt, vt, KT), lambda b, g: (b, 0, 0, 0)),
        pl.BlockSpec((1, nkt, KT, 128), lambda b, g: (b, 0, 0, 0)),
        const((META_ROWS, 512)), const((META_ROWS, 256)), const((128, META_ROWS)),
        const((META_ROWS, KX_COLS)), const((1, vt, KT)), const((META_ROWS, 128)),
        const((A_HEADS, LANES)), const((1, 512)),
    ]
    out_shape = (jax.ShapeDtypeStruct((nb, s, 512), BF16), jax.ShapeDtypeStruct((META_ROWS, 512), BF16))
    out_specs = (pl.BlockSpec((1, QB, 512), qidx), const((META_ROWS, 512)))
    scratch = [
        pltpu.VMEM((nkt + 1, KT, QB), F32),
        pltpu.VMEM((nkt + 1, KT, QB), BF16),
        pltpu.VMEM((A_HEADS, QB, 2 * LANES), BF16),
        pltpu.VMEM((IDX_HEADS, QB, LANES), BF16),
        pltpu.VMEM((8, QB), F32),
        pltpu.VMEM((A_HEADS, 1, QB), F32),
        pltpu.VMEM((A_HEADS, VT_ROWS, QB), F32),
        pltpu.VMEM((GROUP_ROWS, QB), F32),
        pltpu.VMEM((A_HEADS, GROUP_ROWS, QB), F32),
        pltpu.VMEM((A_HEADS, GROUP_ROWS, QB), BF16),
        pltpu.VMEM((1, QB), F32),
        pltpu.VMEM((A_HEADS, 1, QB), F32),
        pltpu.VMEM((A_HEADS, 1, QB), F32),
    ]
    return pl.pallas_call(
        functools.partial(_dsa_kernel, topk=topk),
        grid=(nb, nq + 1),
        in_specs=in_specs,
        out_specs=out_specs,
        out_shape=out_shape,
        scratch_shapes=scratch,
        compiler_params=pltpu.CompilerParams(
            dimension_semantics=("arbitrary", "arbitrary"), vmem_limit_bytes=VMEM_LIMIT),
        name="dsa",
    )(aq, iq, miscT, akx.reshape(nb, nkt, KT, KX_COLS), avT, ik2.reshape(nb, nkt, KT, 128),
      m["aq"], m["iq"], m["miscT"], m["akx"], m["avT"], m["ik2"], _alibi_query_features(), ng)


def _chunk_cumsum_rows(x):
    row_in_chunk = lax.broadcasted_iota(I32, x.shape, 0) & (CHUNK - 1)
    sh = 1
    while sh < CHUNK:
        x = x + jnp.where(row_in_chunk >= sh, pltpu.roll(x, sh, 0), 0.0)
        sh *= 2
    return x


def _rows_of_chunk(rows_per_chunk):
    return jnp.concatenate([jnp.broadcast_to(r, (CHUNK, r.shape[1])) for r in rows_per_chunk], axis=0)


def _gla_block(q, k, v, lr, og, wg, bg, ng, st_s, valid_rows):
    nrow = q.shape[0]
    nch = nrow // CHUNK
    z = _dot(lr.astype(BF16), wg) + bg
    logg = (jnp.minimum(z, 0.0) - jnp.log(1.0 + jnp.exp(-jnp.abs(z)))) / B_GATE_TAU
    if valid_rows is not None:
        rows = lax.broadcasted_iota(I32, logg.shape, 0)
        logg = jnp.where(rows < valid_rows, logg, 0.0)
    b = _chunk_cumsum_rows(logg)
    b_last_rows = [b[(c + 1) * CHUNK - 1:(c + 1) * CHUNK, :] for c in range(nch)]
    b_mid = _rows_of_chunk([b[c * CHUNK + CHUNK // 2 - 1:c * CHUNK + CHUNK // 2, :] for c in range(nch)])
    b_last = _rows_of_chunk(b_last_rows)
    q_in = q * jnp.exp(b)
    q_ic = q * jnp.exp(b - b_mid)
    k_ic = k * jnp.exp(b_mid - b)
    k_st = k * jnp.exp(b_last - b)
    decay = [jnp.exp(r) for r in b_last_rows]

    lane_half = lax.broadcasted_iota(I32, (1, LANES), 1) // B_KEY_DIM
    ri = lax.broadcasted_iota(I32, (nrow, nrow), 0)
    ci = lax.broadcasted_iota(I32, (nrow, nrow), 1)
    causal = ((ri >> 6) == (ci >> 6)) & (ci <= ri)
    outs = []
    for p in range(B_HEADS // 2):
        sl = slice(p * LANES, (p + 1) * LANES)
        heads = (2 * p, 2 * p + 1)
        v_h = [v[:, h * B_VAL_DIM:(h + 1) * B_VAL_DIM] for h in heads]
        k_st_h = [jnp.where(lane_half == hh, k_st[:, sl], 0.0).astype(BF16) for hh in range(2)]
        states = [st_s[p]]
        for c in range(nch):
            rs = slice(c * CHUNK, (c + 1) * CHUNK)
            upd = _dot_tn(v_h[0][rs], k_st_h[0][rs]) + _dot_tn(v_h[1][rs], k_st_h[1][rs])
            states.append(states[c] * decay[c][:, sl] + upd)
        st_s[p] = states[nch]
        k_ic_p = k_ic[:, sl].astype(BF16)
        for hh, h in enumerate(heads):
            hm = lane_half == hh
            a = _dot_nt(jnp.where(hm, q_ic[:, sl], 0.0).astype(BF16), k_ic_p)
            o = _dot(jnp.where(causal, a, 0.0).astype(BF16), v_h[hh])
            q_in_h = jnp.where(hm, q_in[:, sl], 0.0).astype(BF16)
            o = o + jnp.concatenate(
                [_dot_nt(q_in_h[c * CHUNK:(c + 1) * CHUNK], states[c].astype(BF16)) for c in range(nch)], axis=0)
            ms = jnp.mean(o * o, axis=-1, keepdims=True)
            on = o * lax.rsqrt(ms + RMS_EPS) * ng[:, h * B_VAL_DIM:(h + 1) * B_VAL_DIM]
            og_h = og[:, h * B_VAL_DIM:(h + 1) * B_VAL_DIM]
            outs.append(on * (og_h * jax.nn.sigmoid(og_h)))
    return outs


def _gla_kernel(bq_ref, bk_ref, bv_ref, misc_ref, bog_ref,
                mbq_ref, mbk_ref, mbv_ref, mmisc_ref, mbog_ref,
                wg_ref, bg_ref, ng_ref, out_ref, mout_ref, st_s):
    j = pl.program_id(1)
    wg = wg_ref[...]
    bg = bg_ref[...]
    ng = ng_ref[...]

    @pl.when(j == 0)
    def _():
        st_s[...] = jnp.zeros(st_s.shape, F32)
        outs = _gla_block(mbq_ref[0:CHUNK, :].astype(F32), mbk_ref[0:CHUNK, :].astype(F32),
                          mbv_ref[0:CHUNK, :], mmisc_ref[0:CHUNK, MISC_LR:MISC_LR + B_GATE_RANK],
                          mbog_ref[0:CHUNK, :], wg, bg, ng, st_s, N_META)
        for h, o in enumerate(outs):
            mout_ref[:, h * B_VAL_DIM:(h + 1) * B_VAL_DIM] = o.astype(BF16)

    outs = _gla_block(bq_ref[0].astype(F32), bk_ref[0].astype(F32), bv_ref[0],
                      misc_ref[0, :, MISC_LR:MISC_LR + B_GATE_RANK], bog_ref[0], wg, bg, ng, st_s, None)
    for h, o in enumerate(outs):
        out_ref[0, :, h * B_VAL_DIM:(h + 1) * B_VAL_DIM] = o.astype(BF16)


def _gla(bq, bk, bv, misc, bog, m, wg, bg, ng):
    nb, s, _ = bq.shape
    row = lambda c: pl.BlockSpec((1, GLA_BLOCK, c), lambda b, j: (b, j, 0))
    const = lambda shp: pl.BlockSpec(shp, lambda b, j: (0,) * len(shp))
    return pl.pallas_call(
        _gla_kernel,
        grid=(nb, s // GLA_BLOCK),
        in_specs=[row(256), row(256), row(512), row(128), row(512),
                  const((META_ROWS, 256)), const((META_ROWS, 256)), const((META_ROWS, 512)),
                  const((META_ROWS, 128)), const((META_ROWS, 512)),
                  const((B_GATE_RANK, 256)), const((1, 256)), const((1, 512))],
        out_specs=(row(512), const((CHUNK, 512))),
        out_shape=(jax.ShapeDtypeStruct((nb, s, 512), BF16), jax.ShapeDtypeStruct((CHUNK, 512), BF16)),
        scratch_shapes=[pltpu.VMEM((B_HEADS // 2, LANES, LANES), F32)],
        compiler_params=pltpu.CompilerParams(
            dimension_semantics=("arbitrary", "arbitrary"), vmem_limit_bytes=VMEM_LIMIT),
        name="gla",
    )(bq, bk, bv, misc, bog, m["bq"], m["bk"], m["bv"], m["misc"], m["bog"], wg, bg, ng)


def _outproj_kernel(x_ref, a_ref, b_ref, lng_ref, lnb_ref, wo_ref, g1_ref, b1_ref, out_ref):
    h = _layer_norm(x_ref[0], lng_ref[...], lnb_ref[...])
    mixed = _dot(a_ref[0], wo_ref[0:A_WIDTH, :]) + _dot(b_ref[0], wo_ref[A_WIDTH:, :])
    out_ref[0] = _layer_norm(ALPHA * h + mixed, g1_ref[...], b1_ref[...])


def _outproj(x, a, b, lng, lnb, wo, g1, b1, tm):
    nb, s, d = x.shape
    row = lambda c: pl.BlockSpec((1, tm, c), lambda bb, i: (bb, i, 0))
    const = lambda shp: pl.BlockSpec(shp, lambda bb, i: (0,) * len(shp))
    return pl.pallas_call(
        _outproj_kernel,
        grid=(nb, s // tm),
        in_specs=[row(d), row(512), row(512), const((1, d)), const((1, d)), const((d, d)),
                  const((1, d)), const((1, d))],
        out_specs=row(d),
        out_shape=jax.ShapeDtypeStruct((nb, s, d), F32),
        compiler_params=pltpu.CompilerParams(
            dimension_semantics=("arbitrary", "arbitrary"), vmem_limit_bytes=VMEM_LIMIT),
        name="outproj",
    )(x, a, b, lng, lnb, wo, g1, b1)


def _gelu_tanh(x):
    c = 0.7978845608028654
    return 0.5 * x * (1.0 + jnp.tanh(c * (x + 0.044715 * (x * x * x))))


def _ffn_kernel(x_ref, xh_ref, a_ref, ah_ref, b_ref, bh_ref, hm_ref, lng_ref, lnb_ref, wo_ref, g1_ref, b1_ref,
                wup_ref, cw_ref, cb_ref, wdn_ref, g2_ref, b2_ref, out_ref, y_s):
    j = pl.program_id(1)
    x = jnp.concatenate([xh_ref[0], x_ref[0]], axis=0)
    a_mix = jnp.concatenate([ah_ref[0], a_ref[0]], axis=0)
    b_mix = jnp.concatenate([bh_ref[0], b_ref[0]], axis=0)
    mixed = _dot(a_mix, wo_ref[0:A_WIDTH, :]) + _dot(b_mix, wo_ref[A_WIDTH:, :])
    h1 = _layer_norm(ALPHA * _layer_norm(x, lng_ref[...], lnb_ref[...]) + mixed, g1_ref[...], b1_ref[...])
    h = h1[HALO:, :]
    halo = jnp.where(j == 0, hm_ref[...], h1[0:HALO, :])
    hb = jnp.concatenate([halo, h], axis=0).astype(BF16)
    for f in range(D_FF // FF_TILE):
        fs = slice(f * FF_TILE, (f + 1) * FF_TILE)
        a = _dot(hb, wup_ref[:, fs])
        gate = _dot(hb[HALO:, :], wup_ref[:, D_FF + f * FF_TILE:D_FF + (f + 1) * FF_TILE])
        cw = cw_ref[:, fs]
        conv = cb_ref[:, fs] + cw[CONV_W - 1:CONV_W, :] * a[HALO:, :]
        for back in range(1, CONV_W):
            tap = cw[CONV_W - 1 - back:CONV_W - back, :]
            conv = conv + tap * pltpu.roll(a, back, 0)[HALO:, :]
        y_s[:, fs] = (_gelu_tanh(conv) * gate).astype(BF16)
    ffn = _dot(y_s[...], wdn_ref[...])
    out_ref[0] = _layer_norm(ALPHA * h + ffn, g2_ref[...], b2_ref[...])


def _ffn(x, a, b, h1_meta, lng, lnb, wo, g1, b1, wup, cw, cb, wdn, g2, b2, tm):
    nb, s, d = x.shape
    per = tm // HALO
    const = lambda shp: pl.BlockSpec(shp, lambda bb, j: (0,) * len(shp), pipeline_mode=pl.Buffered(1))
    main = lambda c: pl.BlockSpec((1, tm, c), lambda bb, j: (bb, j, 0))
    halo = lambda c: pl.BlockSpec((1, HALO, c), lambda bb, j: (bb, jnp.maximum(j * per - 1, 0), 0))
    return pl.pallas_call(
        _ffn_kernel,
        grid=(nb, s // tm),
        in_specs=[main(d), halo(d), main(A_WIDTH), halo(A_WIDTH), main(B_WIDTH), halo(B_WIDTH),
                  const((HALO, d)), const((1, d)), const((1, d)), const((d, d)), const((1, d)), const((1, d)),
                  const((d, 2 * D_FF)), const((CONV_W, D_FF)), const((1, D_FF)),
                  const((D_FF, d)), const((1, d)), const((1, d))],
        out_specs=main(d),
        out_shape=jax.ShapeDtypeStruct((nb, s, d), F32),
        scratch_shapes=[pltpu.VMEM((tm, D_FF), BF16)],
        compiler_params=pltpu.CompilerParams(
            dimension_semantics=("arbitrary", "arbitrary"), vmem_limit_bytes=VMEM_LIMIT),
        name="ffn",
    )(x, x, a, a, b, b, h1_meta, lng, lnb, wo, g1, b1, wup, cw, cb, wdn, g2, b2)


def _permute_w_in(w):
    o = 0
    parts = {}
    for name, n in (("a_q", 512), ("a_k", 512), ("a_v", 512), ("i_q", 256), ("i_k", 64), ("i_w", 4),
                    ("b_q", 256), ("b_k", 256), ("b_v", 512), ("b_lr", 16), ("b_og", 512)):
        parts[name] = w[:, o:o + n]
        o += n
    zeros = jnp.zeros((w.shape[0], 128 - B_GATE_RANK - IDX_HEADS), w.dtype)
    cols = [parts["a_q"] * (A_HEAD_DIM ** -0.5 * LOG2E), parts["a_k"], parts["a_v"], parts["i_q"],
            parts["b_q"] * (B_KEY_DIM ** -0.5), parts["b_k"], parts["b_v"], parts["b_og"],
            parts["i_k"], parts["i_k"],
            parts["b_lr"], parts["i_w"] * (IDX_HEADS ** -0.5) * (IDX_DIM ** -0.5), zeros]
    return jnp.concatenate(cols, axis=1).astype(BF16)


def kernel(x, meta, ln_in_g, ln_in_b, w_in, w_gate_b, b_gate_b, attn_norm_g, gla_norm_g, w_out,
           ln1_g, ln1_b, w_up, conv_w, conv_b, w_down, ln2_g, ln2_b):
    nb, s, d = x.shape
    assert d == D_MODEL and w_in.shape[0] == DEPTH == 1
    assert s % ROW_TILE == 0 and s % KT == 0 and s % GLA_BLOCK == 0
    topk = min(TOPK_MAX, s // 4)
    l = 0
    r2 = lambda v: v.reshape(1, -1)
    lng, lnb = r2(ln_in_g), r2(ln_in_b)
    w_perm = _permute_w_in(w_in[l])
    wo = w_out[l].astype(BF16)
    wup = w_up[l].astype(BF16)
    wdn = w_down[l].astype(BF16)
    wg = w_gate_b[l].astype(BF16)
    bg = r2(b_gate_b[l])
    ng_a = r2(attn_norm_g[l])
    ng_b = r2(gla_norm_g[l])

    x_meta = jnp.zeros((1, META_ROWS, d), x.dtype).at[0, :N_META].set(meta.astype(x.dtype))
    names = ("aq", "akx", "avT", "iq", "bq", "bk", "bv", "bog", "ik2", "misc", "miscT")
    mo = dict(zip(names, _inproj(x_meta, lng, lnb, w_perm, META_ROWS, 0)))
    keep = (jnp.arange(META_ROWS) < N_META)
    m = {}
    for n in names:
        v = mo[n][0]
        if n in ("avT", "miscT"):
            m[n] = jnp.where(keep[None, :], v, jnp.zeros_like(v)).reshape((1,) * (n == "avT") + v.shape[-2:])
        else:
            m[n] = jnp.where(keep[:, None], v, jnp.zeros_like(v))

    aq, akx, avT, iq, bq, bk, bv, bog, ik2, misc, miscT = _inproj(x, lng, lnb, w_perm, ROW_TILE, N_META)
    a_out, a_out_m = _dsa(aq, iq, miscT, akx, avT, ik2, m, ng_a, topk)
    b_out, b_out_m = _gla(bq, bk, bv, misc, bog, m, wg, bg, ng_b)

    g1, b1 = r2(ln1_g[l]), r2(ln1_b[l])
    b_out_m = jnp.zeros((META_ROWS, B_WIDTH), BF16).at[:CHUNK].set(b_out_m)
    h1_m = _outproj(x_meta, a_out_m[None], b_out_m[None], lng, lnb, wo, g1, b1, META_ROWS)
    return _ffn(x, a_out, b_out, h1_m[0, N_META - HALO:N_META], lng, lnb, wo, g1, b1,
                wup, conv_w[l], r2(conv_b[l]), wdn, r2(ln2_g[l]), r2(ln2_b[l]), ROW_TILE)
```

```python
import functools

import jax
import jax.numpy as jnp
from jax import lax
from jax.experimental import pallas as pl
from jax.experimental.pallas import tpu as pltpu

F32 = jnp.float32
BF16 = jnp.bfloat16
I32 = jnp.int32

D_MODEL = 1024
CHUNK = 64
N_META = 16
A_HEADS = 8
A_WIDTH = 512
A_HEAD_DIM = 64
IDX_HEADS = 4
IDX_DIM = 64
TOPK_MAX = 256
B_HEADS = 4
B_WIDTH = 512
B_VAL_DIM = 128
B_KEY_DIM = 64
B_GATE_RANK = 16
B_GATE_TAU = 16.0
D_FF = 2816
CONV_W = 3
LN_EPS = 1e-5
RMS_EPS = 1e-6
DEPTH = 1
ALPHA = (2.0 * DEPTH) ** 0.25

LANES = 128
ROW_TILE = 512
QB = 256
KT = 256
MT = 64
GROUP_ROWS = MT + KT
META_ROWS = 256
GLA_BLOCK = 256
FF_TILE = 256
HALO = 16
U_NEG_INF, U_POS_INF = 0x007F, 0xFF80
N_SLOPE_PARTS = 3
VT_ROWS = 80
KX_COLS = 2 * A_WIDTH
LOG2E = 1.4426950408889634
ALIBI_C = tuple(2.0 ** (-8.0 * (h + 1) / A_HEADS) * LOG2E for h in range(A_HEADS))
VMEM_LIMIT = 56 * 1024 * 1024

_COLS = {}
_off = 0
for _name, _w in (("aq", 512), ("ak", 512), ("av", 512), ("iq", 256), ("bq", 256), ("bk", 256),
                  ("bv", 512), ("bog", 512), ("ik2", 128), ("misc", 128)):
    _COLS[_name] = (_off, _w)
    _off += _w
PROJ_COLS = _off
MISC_LR = 0
MISC_IW = 16


def _layer_norm(x, g, b):
    mu = jnp.mean(x, axis=-1, keepdims=True)
    xc = x - mu
    var = jnp.mean(xc * xc, axis=-1, keepdims=True)
    return xc * lax.rsqrt(var + LN_EPS) * g + b


def _dot(a, b):
    return jnp.dot(a, b, preferred_element_type=F32)


def _dot_nt(a, b):
    return lax.dot_general(a, b, (((1,), (1,)), ((), ())), preferred_element_type=F32)


def _dot_tn(a, b):
    return lax.dot_general(a, b, (((0,), (0,)), ((), ())), preferred_element_type=F32)


def _inproj_kernel(x_ref, g_ref, b_ref, w_ref, aq_ref, akx_ref, avT_ref, iq_ref, bq_ref, bk_ref,
                   bv_ref, bog_ref, ik2_ref, misc_ref, miscT_ref, *, pos0):
    h = _layer_norm(x_ref[0], g_ref[...], b_ref[...])
    hb = h.astype(BF16)
    tm = hb.shape[0]

    def proj(name):
        lo, w = _COLS[name]
        return _dot(hb, w_ref[:, lo:lo + w])

    aq_ref[0] = proj("aq").astype(BF16)
    ak = proj("ak")
    rows = lax.broadcasted_iota(I32, (tm, LANES), 0)
    lanes = lax.broadcasted_iota(I32, (tm, LANES), 1)
    pos = pos0 + pl.program_id(1) * tm + rows
    feat = jnp.where(lanes < 2 * N_SLOPE_PARTS, jnp.where(lanes % 2 == 0, pos >> 6, pos & 63), 0)
    feat = feat.astype(F32).astype(BF16)
    for p in range(A_HEADS // 2):
        akx_ref[0, :, 2 * p * LANES:(2 * p + 1) * LANES] = ak[:, p * LANES:(p + 1) * LANES].astype(BF16)
        akx_ref[0, :, (2 * p + 1) * LANES:(2 * p + 2) * LANES] = feat
    avT = proj("av").T
    ones = jnp.ones((VT_ROWS - A_HEAD_DIM, KT), BF16)
    for i in range(avT_ref.shape[1]):
        for hd in range(A_HEADS):
            avT_ref[0, i, hd * VT_ROWS:hd * VT_ROWS + A_HEAD_DIM, :] = (
                avT[hd * A_HEAD_DIM:(hd + 1) * A_HEAD_DIM, i * KT:(i + 1) * KT].astype(BF16))
            avT_ref[0, i, hd * VT_ROWS + A_HEAD_DIM:(hd + 1) * VT_ROWS, :] = ones
    iq_ref[0] = proj("iq").astype(BF16)
    bq_ref[0] = proj("bq").astype(BF16)
    bk_ref[0] = proj("bk").astype(BF16)
    bv_ref[0] = proj("bv").astype(BF16)
    bog_ref[0] = proj("bog")
    ik2_ref[0] = proj("ik2").astype(BF16)
    misc = proj("misc")
    misc_ref[0] = misc
    miscT_ref[0] = misc.T


def _inproj(x, ln_g, ln_b, w_perm, tm, pos0):
    nb, s, d = x.shape
    nt = s // tm
    row = lambda c: pl.BlockSpec((1, tm, c), lambda b, i: (b, i, 0))
    const = lambda shp: pl.BlockSpec(shp, lambda b, i: (0,) * len(shp))
    out_shape = (
        jax.ShapeDtypeStruct((nb, s, 512), BF16),
        jax.ShapeDtypeStruct((nb, s, KX_COLS), BF16),
        jax.ShapeDtypeStruct((nb, s // KT, A_HEADS * VT_ROWS, KT), BF16),
        jax.ShapeDtypeStruct((nb, s, 256), BF16),
        jax.ShapeDtypeStruct((nb, s, 256), BF16),
        jax.ShapeDtypeStruct((nb, s, 256), BF16),
        jax.ShapeDtypeStruct((nb, s, 512), BF16),
        jax.ShapeDtypeStruct((nb, s, 512), F32),
        jax.ShapeDtypeStruct((nb, s, 128), BF16),
        jax.ShapeDtypeStruct((nb, s, 128), F32),
        jax.ShapeDtypeStruct((nb, 128, s), F32),
    )
    out_specs = (
        row(512), row(KX_COLS),
        pl.BlockSpec((1, tm // KT, A_HEADS * VT_ROWS, KT), lambda b, i: (b, i, 0, 0)),
        row(256), row(256), row(256), row(512), row(512), row(128), row(128),
        pl.BlockSpec((1, 128, tm), lambda b, i: (b, 0, i)),
    )
    return pl.pallas_call(
        functools.partial(_inproj_kernel, pos0=pos0),
        grid=(nb, nt),
        in_specs=[row(d), const((1, d)), const((1, d)), const((d, PROJ_COLS))],
        out_specs=out_specs,
        out_shape=out_shape,
        compiler_params=pltpu.CompilerParams(
            dimension_semantics=("arbitrary", "arbitrary"), vmem_limit_bytes=VMEM_LIMIT),
        name="inproj",
    )(x, ln_g, ln_b, w_perm)


def _static_when(cond: bool):
    def deco(fn):
        if cond:
            fn()
    return deco


def _dsa_kernel(aq_ref, iq_ref, miscT_ref, akx_ref, avT_ref, ik2_ref,
                maq_ref, miq_ref, mmiscT_ref, makx_ref, mavT_ref, mik2_ref, qfeat_ref, ng_ref,
                out_ref,
                sc_s, scb_s, qm_s, iqm_s, iw_s, m_s, acc_s, bias_s, lg_s, p_s, neq_s, shift_s, corr_s,
                *, topk, is_meta):
    g = 0 if is_meta else pl.program_id(1) + 1
    n_past = 0 if is_meta else g - 1
    when_real = _static_when(not is_meta)
    when_meta = _static_when(is_meta)

    lane_half = lax.broadcasted_iota(I32, (1, LANES), 1) // A_HEAD_DIM

    def stage(aq, iq, miscT):
        for h in range(A_HEADS):
            p = h // 2
            qp = aq[:, p * LANES:(p + 1) * LANES]
            qm_s[h, :, 0:LANES] = jnp.where(lane_half == (h % 2), qp, jnp.zeros_like(qp))
            qm_s[h, :, LANES:2 * LANES] = jnp.broadcast_to(qfeat_ref[h:h + 1, :], (QB, LANES)).astype(BF16)
        for h in range(IDX_HEADS):
            p = h // 2
            qp = iq[:, p * LANES:(p + 1) * LANES]
            iqm_s[h] = jnp.where(lane_half == (h % 2), qp, jnp.zeros_like(qp))
        iw_s[...] = miscT[MISC_IW:MISC_IW + 8, :]

    lane = lax.broadcasted_iota(I32, (1, QB), 1)
    if is_meta:
        stage(maq_ref[...], miq_ref[...], mmiscT_ref[...])
        qpos = lane
    else:
        stage(aq_ref[0], iq_ref[0], miscT_ref[0])
        qpos = N_META + (g - 1) * QB + lane

    ninf = jnp.float32(-jnp.inf)

    def scores_for_tile(ik2_t, allowed):
        s = None
        for h in range(IDX_HEADS):
            sh = _dot_nt(ik2_t, iqm_s[h])
            term = iw_s[h:h + 1, :] * jnp.maximum(sh, 0.0)
            s = term if s is None else s + term
        s = jnp.where(s == 0.0, 0.0, s)
        return s if allowed is None else jnp.where(allowed, s, ninf)

    def store_scores(slot, r, s):
        sc_s[slot, 0:r, :] = s
        scb_s[slot, 0:r, :] = s.astype(BF16)

    mrow = lax.broadcasted_iota(I32, (MT, QB), 0)
    krow = lax.broadcasted_iota(I32, (KT, QB), 0)
    store_scores(0, MT, scores_for_tile(mik2_ref[0:MT, :], mrow < N_META))

    def fill_past(kt):
        store_scores(kt + 1, KT, scores_for_tile(ik2_ref[0, kt], None))

    def fill_pair(i, c):
        fill_past(2 * i)
        fill_past(2 * i + 1)
        return c

    @when_real
    def _():
        lax.fori_loop(0, n_past // 2, fill_pair, 0)

        @pl.when(n_past % 2 == 1)
        def _():
            fill_past(n_past - 1)

        store_scores(g, KT, scores_for_tile(ik2_ref[0, g - 1], (krow >> 6) <= (lane >> 6)))

    def count(arr_s, pred, pack):
        acc_t = jnp.int16 if pack == 16 else I32

        def cnt(tile):
            x = jnp.where(pred(tile), jnp.ones((), acc_t), jnp.zeros((), acc_t))
            x = x.reshape(tile.shape[0] // pack, pack, QB)
            parts = [x[j] for j in range(x.shape[0])]
            while len(parts) > 1:
                parts = [parts[j] + parts[j + 1] for j in range(0, len(parts), 2)]
            return parts[0]

        acc = cnt(arr_s[0, 0:MT, :])
        acc = lax.fori_loop(0, g, lambda kt, a: a + cnt(arr_s[kt + 1]), acc)
        return jnp.sum(acc.astype(I32), axis=0, keepdims=True)

    def ordered_to_bits(k, sign_bit, low_mask):
        return jnp.where(k >= sign_bit, k ^ sign_bit, k ^ low_mask)

    def coarse_body(i, u):
        cand_u = u | jnp.left_shift(jnp.int32(1), 15 - i)
        finite = (cand_u >= U_NEG_INF) & (cand_u <= U_POS_INF)
        bits = ordered_to_bits(jnp.clip(cand_u, U_NEG_INF, U_POS_INF), 0x8000, 0xFFFF)
        cand = lax.bitcast_convert_type(jnp.left_shift(bits, 16), F32).astype(BF16)
        cnt = count(scb_s, lambda tile: tile >= cand, 16)
        return jnp.where(finite & (cnt >= topk), cand_u, u)

    u1 = lax.fori_loop(0, 16, coarse_body, jnp.zeros((1, QB), I32))
    few = u1 == 0
    t1_bits = jnp.left_shift(ordered_to_bits(jnp.where(few, 0x8000, u1), 0x8000, 0xFFFF), 16)
    base = (t1_bits ^ ((t1_bits >> 31) & 0x7FFFFFFF)) - 0x8000

    def key_to_f32(k):
        return lax.bitcast_convert_type(k ^ ((k >> 31) & 0x7FFFFFFF), F32)

    def fine_body(i, o):
        cand_o = o | jnp.left_shift(jnp.int32(1), 16 - i)
        cand = key_to_f32(base + cand_o)
        cnt = count(sc_s, lambda tile: tile >= cand, 8)
        return jnp.where(cnt >= topk, cand_o, o)

    o2 = lax.fori_loop(0, 17, fine_body, jnp.zeros((1, QB), I32))
    thr = jnp.where(few, ninf, key_to_f32(base + o2))
    n_gt = count(sc_s, lambda tile: tile > thr, 8)
    need = jnp.where(few, 0, topk - n_gt).astype(F32)

    m_s[...] = jnp.full(m_s.shape, -jnp.inf, F32)
    acc_s[...] = jnp.zeros(acc_s.shape, F32)

    def offsets(subs):
        offs, row0 = [], 0
        for sub in subs:
            offs.append((row0, sub[0].shape[0]))
            row0 += sub[0].shape[0]
        return offs, row0

    def bias_stage(subs):
        offs, _ = offsets(subs)
        n_eq_after = None
        for (sc_t, _, _, _, n_eq_before), (o, r) in zip(subs, offs):
            if n_eq_before is None:
                n_eq_before = n_eq_after
            ri = lax.broadcasted_iota(I32, (r, r), 0)
            ci = lax.broadcasted_iota(I32, (r, r), 1)
            lower = jnp.where(ri > ci, 1.0, 0.0).astype(BF16)
            eq = sc_t == thr
            rank = _dot(lower, jnp.where(eq, 1.0, 0.0).astype(BF16)) + n_eq_before
            bias_s[o:o + r, :] = jnp.where(
                sc_t > thr, 0.0, jnp.where(eq, jnp.where(rank < need, 0.0, ninf), ninf))
            n_eq_after = rank[r - 1:r, :] + jnp.where(eq[r - 1:r, :], 1.0, 0.0)
        return n_eq_after

    def logits_head(subs, h):
        offs, _ = offsets(subs)
        m_old = m_s[h]
        m_new = m_old
        for (_, kx_fn, _, after, _), (o, r) in zip(subs, offs):
            lg = _dot_nt(kx_fn(h // 2), qm_s[h]) + bias_s[o:o + r, :]
            if after is not None:
                lg = lg + ALIBI_C[h] * after
            lg_s[h, o:o + r, :] = lg
            m_new = jnp.maximum(m_new, jnp.max(lg, axis=0, keepdims=True))
        m_s[h] = m_new
        m_safe = jnp.where(m_new == ninf, 0.0, m_new)
        shift_s[h] = m_safe
        corr_s[h] = jnp.exp2(m_old - m_safe)

    def softmax_pv_head(subs, h):
        offs, rows = offsets(subs)
        p_s[h, 0:rows, :] = jnp.exp2(lg_s[h, 0:rows, :] - shift_s[h]).astype(BF16)
        acc = acc_s[h] * corr_s[h]
        for (_, _, vT_fn, _, _), (o, r) in zip(subs, offs):
            acc = acc + _dot(vT_fn(h), p_s[h, o:o + r, :])
        acc_s[h] = acc

    def run_stages(done, nxt):
        n_eq_after = bias_stage(nxt) if nxt is not None else None
        for h in range(A_HEADS):
            if done is not None:
                softmax_pv_head(done, h)
            if nxt is not None:
                logits_head(nxt, h)
        return n_eq_after

    def after_term(kpos):
        return jnp.minimum(2 * (qpos - kpos), 0).astype(F32)

    def meta_sub():
        return (sc_s[0, 0:MT, :],
                lambda p: makx_ref[0:MT, 2 * p * LANES:(2 * p + 2) * LANES],
                lambda h: mavT_ref[0, h * VT_ROWS:(h + 1) * VT_ROWS, 0:MT],
                after_term(mrow), jnp.zeros((1, QB), F32))

    def real_sub(kt, after, n_eq_before):
        return (sc_s[kt + 1],
                lambda p: akx_ref[0, kt, :, 2 * p * LANES:(2 * p + 2) * LANES],
                lambda h: avT_ref[0, kt, h * VT_ROWS:(h + 1) * VT_ROWS, :],
                after, n_eq_before)

    neq_s[...] = jnp.sum(jnp.where(sc_s[0, 0:MT, :] == thr, 1.0, 0.0), axis=0, keepdims=True)

    def past(kt):
        return [real_sub(kt, None, neq_s[...])]

    def last_group():
        return [meta_sub(), real_sub(g - 1, after_term(N_META + (g - 1) * KT + krow), neq_s[...])]

    @when_real
    def _():
        @pl.when(n_past > 0)
        def _():
            neq_s[...] = run_stages(None, past(0))

        def pipe_body(i, c):
            neq_s[...] = run_stages(past(i), past(i + 1))
            return c

        lax.fori_loop(0, jnp.maximum(n_past - 1, 0), pipe_body, 0)

        @pl.when(n_past > 0)
        def _():
            run_stages(past(n_past - 1), last_group())

        @pl.when(n_past == 0)
        def _():
            run_stages(None, last_group())

        run_stages(last_group(), None)

    @when_meta
    def _():
        run_stages(None, [meta_sub()])
        run_stages([meta_sub()], None)

    outs = []
    for h in range(A_HEADS):
        o = acc_s[h, 0:A_HEAD_DIM, :] / acc_s[h, A_HEAD_DIM:A_HEAD_DIM + 1, :]
        ms = jnp.mean(o * o, axis=0, keepdims=True)
        outs.append(o * lax.rsqrt(ms + RMS_EPS))
    res = (jnp.concatenate(outs, axis=0).T * ng_ref[...]).astype(BF16)

    out_ref[0] = res


def _alibi_query_features():
    rows = []
    for c in ALIBI_C:
        rest = jnp.float32(c)
        lanes = []
        for _ in range(N_SLOPE_PARTS):
            part = rest.astype(BF16).astype(F32)
            lanes += [part * CHUNK, part]
            rest = rest - part
        rows.append(jnp.stack(lanes + [jnp.float32(0.0)] * (LANES - len(lanes))))
    return jnp.stack(rows)


def _dsa(aq, iq, miscT, akx, avT, ik2, m, ng, topk, is_meta):
    nb, s, _ = aq.shape
    nq = s // QB
    nkt = s // KT
    vt = A_HEADS * VT_ROWS
    qidx = lambda b, i: (b, i, 0)
    const = lambda shp: pl.BlockSpec(shp, lambda b, i: (0,) * len(shp))
    in_specs = [
        pl.BlockSpec((1, QB, 512), qidx),
        pl.BlockSpec((1, QB, 256), qidx),
        pl.BlockSpec((1, 128, QB), lambda b, i: (b, 0, i)),
        pl.BlockSpec((1, nkt, KT, KX_COLS), lambda b, i: (b, 0, 0, 0)),
        pl.BlockSpec((1, nkt, vt, KT), lambda b, i: (b, 0, 0, 0)),
        pl.BlockSpec((1, nkt, KT, 128), lambda b, i: (b, 0, 0, 0)),
        const((META_ROWS, 512)), const((META_ROWS, 256)), const((128, META_ROWS)),
        const((META_ROWS, KX_COLS)), const((1, vt, KT)), const((META_ROWS, 128)),
        const((A_HEADS, LANES)), const((1, 512)),
    ]
    out_shape = jax.ShapeDtypeStruct((nb, s, 512), BF16)
    out_specs = pl.BlockSpec((1, QB, 512), qidx)
    scratch = [
        pltpu.VMEM((nkt + 1, KT, QB), F32),
        pltpu.VMEM((nkt + 1, KT, QB), BF16),
        pltpu.VMEM((A_HEADS, QB, 2 * LANES), BF16),
        pltpu.VMEM((IDX_HEADS, QB, LANES), BF16),
        pltpu.VMEM((8, QB), F32),
        pltpu.VMEM((A_HEADS, 1, QB), F32),
        pltpu.VMEM((A_HEADS, VT_ROWS, QB), F32),
        pltpu.VMEM((GROUP_ROWS, QB), F32),
        pltpu.VMEM((A_HEADS, GROUP_ROWS, QB), F32),
        pltpu.VMEM((A_HEADS, GROUP_ROWS, QB), BF16),
        pltpu.VMEM((1, QB), F32),
        pltpu.VMEM((A_HEADS, 1, QB), F32),
        pltpu.VMEM((A_HEADS, 1, QB), F32),
    ]
    return pl.pallas_call(
        functools.partial(_dsa_kernel, topk=topk, is_meta=is_meta),
        grid=(nb, nq),
        in_specs=in_specs,
        out_specs=out_specs,
        out_shape=out_shape,
        scratch_shapes=scratch,
        compiler_params=pltpu.CompilerParams(
            dimension_semantics=("arbitrary", "arbitrary"), vmem_limit_bytes=VMEM_LIMIT),
        name="dsa",
    )(aq, iq, miscT, akx.reshape(nb, nkt, KT, KX_COLS), avT, ik2.reshape(nb, nkt, KT, 128),
      m["aq"], m["iq"], m["miscT"], m["akx"], m["avT"], m["ik2"], _alibi_query_features(), ng)


def _chunk_cumsum_rows(x):
    row_in_chunk = lax.broadcasted_iota(I32, x.shape, 0) & (CHUNK - 1)
    sh = 1
    while sh < CHUNK:
        x = x + jnp.where(row_in_chunk >= sh, pltpu.roll(x, sh, 0), 0.0)
        sh *= 2
    return x


def _rows_of_chunk(rows_per_chunk):
    return jnp.concatenate([jnp.broadcast_to(r, (CHUNK, r.shape[1])) for r in rows_per_chunk], axis=0)


def _gla_block(q, k, v, lr, og, wg, bg, ng, st_s, valid_rows):
    nrow = q.shape[0]
    nch = nrow // CHUNK
    z = _dot(lr.astype(BF16), wg) + bg
    logg = (jnp.minimum(z, 0.0) - jnp.log(1.0 + jnp.exp(-jnp.abs(z)))) / B_GATE_TAU
    if valid_rows is not None:
        rows = lax.broadcasted_iota(I32, logg.shape, 0)
        logg = jnp.where(rows < valid_rows, logg, 0.0)
    b = _chunk_cumsum_rows(logg)
    b_last_rows = [b[(c + 1) * CHUNK - 1:(c + 1) * CHUNK, :] for c in range(nch)]
    b_mid = _rows_of_chunk([b[c * CHUNK + CHUNK // 2 - 1:c * CHUNK + CHUNK // 2, :] for c in range(nch)])
    b_last = _rows_of_chunk(b_last_rows)
    q_in = q * jnp.exp(b)
    q_ic = q * jnp.exp(b - b_mid)
    k_ic = k * jnp.exp(b_mid - b)
    k_st = k * jnp.exp(b_last - b)
    decay = [jnp.exp(r) for r in b_last_rows]

    lane_half = lax.broadcasted_iota(I32, (1, LANES), 1) // B_KEY_DIM
    ri = lax.broadcasted_iota(I32, (nrow, nrow), 0)
    ci = lax.broadcasted_iota(I32, (nrow, nrow), 1)
    causal = ((ri >> 6) == (ci >> 6)) & (ci <= ri)
    outs = []
    for p in range(B_HEADS // 2):
        sl = slice(p * LANES, (p + 1) * LANES)
        heads = (2 * p, 2 * p + 1)
        v_h = [v[:, h * B_VAL_DIM:(h + 1) * B_VAL_DIM] for h in heads]
        k_st_h = [jnp.where(lane_half == hh, k_st[:, sl], 0.0).astype(BF16) for hh in range(2)]
        states = [st_s[p]]
        for c in range(nch):
            rs = slice(c * CHUNK, (c + 1) * CHUNK)
            upd = _dot_tn(v_h[0][rs], k_st_h[0][rs]) + _dot_tn(v_h[1][rs], k_st_h[1][rs])
            states.append(states[c] * decay[c][:, sl] + upd)
        st_s[p] = states[nch]
        k_ic_p = k_ic[:, sl].astype(BF16)
        for hh, h in enumerate(heads):
            hm = lane_half == hh
            a = _dot_nt(jnp.where(hm, q_ic[:, sl], 0.0).astype(BF16), k_ic_p)
            o = _dot(jnp.where(causal, a, 0.0).astype(BF16), v_h[hh])
            q_in_h = jnp.where(hm, q_in[:, sl], 0.0).astype(BF16)
            o = o + jnp.concatenate(
                [_dot_nt(q_in_h[c * CHUNK:(c + 1) * CHUNK], states[c].astype(BF16)) for c in range(nch)], axis=0)
            ms = jnp.mean(o * o, axis=-1, keepdims=True)
            on = o * lax.rsqrt(ms + RMS_EPS) * ng[:, h * B_VAL_DIM:(h + 1) * B_VAL_DIM]
            og_h = og[:, h * B_VAL_DIM:(h + 1) * B_VAL_DIM]
            outs.append(on * (og_h * jax.nn.sigmoid(og_h)))
    return outs


def _gla_kernel(bq_ref, bk_ref, bv_ref, misc_ref, bog_ref,
                mbq_ref, mbk_ref, mbv_ref, mmisc_ref, mbog_ref,
                wg_ref, bg_ref, ng_ref, out_ref, mout_ref, st_s):
    j = pl.program_id(1)
    wg = wg_ref[...]
    bg = bg_ref[...]
    ng = ng_ref[...]

    @pl.when(j == 0)
    def _():
        st_s[...] = jnp.zeros(st_s.shape, F32)
        outs = _gla_block(mbq_ref[0:CHUNK, :].astype(F32), mbk_ref[0:CHUNK, :].astype(F32),
                          mbv_ref[0:CHUNK, :], mmisc_ref[0:CHUNK, MISC_LR:MISC_LR + B_GATE_RANK],
                          mbog_ref[0:CHUNK, :], wg, bg, ng, st_s, N_META)
        for h, o in enumerate(outs):
            mout_ref[:, h * B_VAL_DIM:(h + 1) * B_VAL_DIM] = o.astype(BF16)

    outs = _gla_block(bq_ref[0].astype(F32), bk_ref[0].astype(F32), bv_ref[0],
                      misc_ref[0, :, MISC_LR:MISC_LR + B_GATE_RANK], bog_ref[0], wg, bg, ng, st_s, None)
    for h, o in enumerate(outs):
        out_ref[0, :, h * B_VAL_DIM:(h + 1) * B_VAL_DIM] = o.astype(BF16)


def _gla(bq, bk, bv, misc, bog, m, wg, bg, ng):
    nb, s, _ = bq.shape
    row = lambda c: pl.BlockSpec((1, GLA_BLOCK, c), lambda b, j: (b, j, 0))
    const = lambda shp: pl.BlockSpec(shp, lambda b, j: (0,) * len(shp))
    return pl.pallas_call(
        _gla_kernel,
        grid=(nb, s // GLA_BLOCK),
        in_specs=[row(256), row(256), row(512), row(128), row(512),
                  const((META_ROWS, 256)), const((META_ROWS, 256)), const((META_ROWS, 512)),
                  const((META_ROWS, 128)), const((META_ROWS, 512)),
                  const((B_GATE_RANK, 256)), const((1, 256)), const((1, 512))],
        out_specs=(row(512), const((CHUNK, 512))),
        out_shape=(jax.ShapeDtypeStruct((nb, s, 512), BF16), jax.ShapeDtypeStruct((CHUNK, 512), BF16)),
        scratch_shapes=[pltpu.VMEM((B_HEADS // 2, LANES, LANES), F32)],
        compiler_params=pltpu.CompilerParams(
            dimension_semantics=("arbitrary", "arbitrary"), vmem_limit_bytes=VMEM_LIMIT),
        name="gla",
    )(bq, bk, bv, misc, bog, m["bq"], m["bk"], m["bv"], m["misc"], m["bog"], wg, bg, ng)


def _outproj_kernel(x_ref, a_ref, b_ref, lng_ref, lnb_ref, wo_ref, g1_ref, b1_ref, out_ref):
    h = _layer_norm(x_ref[0], lng_ref[...], lnb_ref[...])
    mixed = _dot(a_ref[0], wo_ref[0:A_WIDTH, :]) + _dot(b_ref[0], wo_ref[A_WIDTH:, :])
    out_ref[0] = _layer_norm(ALPHA * h + mixed, g1_ref[...], b1_ref[...])


def _outproj(x, a, b, lng, lnb, wo, g1, b1, tm):
    nb, s, d = x.shape
    row = lambda c: pl.BlockSpec((1, tm, c), lambda bb, i: (bb, i, 0))
    const = lambda shp: pl.BlockSpec(shp, lambda bb, i: (0,) * len(shp))
    return pl.pallas_call(
        _outproj_kernel,
        grid=(nb, s // tm),
        in_specs=[row(d), row(512), row(512), const((1, d)), const((1, d)), const((d, d)),
                  const((1, d)), const((1, d))],
        out_specs=row(d),
        out_shape=jax.ShapeDtypeStruct((nb, s, d), F32),
        compiler_params=pltpu.CompilerParams(
            dimension_semantics=("arbitrary", "arbitrary"), vmem_limit_bytes=VMEM_LIMIT),
        name="outproj",
    )(x, a, b, lng, lnb, wo, g1, b1)


def _gelu_tanh(x):
    c = 0.7978845608028654
    return 0.5 * x * (1.0 + jnp.tanh(c * (x + 0.044715 * (x * x * x))))


def _ffn_kernel(x_ref, xh_ref, a_ref, ah_ref, b_ref, bh_ref, hm_ref, lng_ref, lnb_ref, wo_ref, g1_ref, b1_ref,
                wup_ref, cw_ref, cb_ref, wdn_ref, g2_ref, b2_ref, out_ref, y_s):
    j = pl.program_id(1)
    def mix(x, a_mix, b_mix):
        mixed = _dot(a_mix, wo_ref[0:A_WIDTH, :]) + _dot(b_mix, wo_ref[A_WIDTH:, :])
        return _layer_norm(ALPHA * _layer_norm(x, lng_ref[...], lnb_ref[...]) + mixed, g1_ref[...], b1_ref[...])

    half = x_ref.shape[1] // 2
    h1_lo = mix(jnp.concatenate([xh_ref[0], x_ref[0, 0:half, :]], axis=0),
                jnp.concatenate([ah_ref[0], a_ref[0, 0:half, :]], axis=0),
                jnp.concatenate([bh_ref[0], b_ref[0, 0:half, :]], axis=0))
    h1_hi = mix(x_ref[0, half:, :], a_ref[0, half:, :], b_ref[0, half:, :])
    h = jnp.concatenate([h1_lo[HALO:, :], h1_hi], axis=0)
    halo = jnp.where(j == 0, hm_ref[...], h1_lo[0:HALO, :])
    hb = jnp.concatenate([halo, h], axis=0).astype(BF16)
    for f in range(D_FF // FF_TILE):
        fs = slice(f * FF_TILE, (f + 1) * FF_TILE)
        a = _dot(hb, wup_ref[:, fs])
        gate = _dot(hb[HALO:, :], wup_ref[:, D_FF + f * FF_TILE:D_FF + (f + 1) * FF_TILE])
        cw = cw_ref[:, fs]
        conv = cb_ref[:, fs] + cw[CONV_W - 1:CONV_W, :] * a[HALO:, :]
        for back in range(1, CONV_W):
            tap = cw[CONV_W - 1 - back:CONV_W - back, :]
            conv = conv + tap * pltpu.roll(a, back, 0)[HALO:, :]
        y_s[:, fs] = (_gelu_tanh(conv) * gate).astype(BF16)
    ffn = _dot(y_s[...], wdn_ref[...])
    out_ref[0] = _layer_norm(ALPHA * h + ffn, g2_ref[...], b2_ref[...])


def _ffn(x, a, b, h1_meta, lng, lnb, wo, g1, b1, wup, cw, cb, wdn, g2, b2, tm):
    nb, s, d = x.shape
    per = tm // HALO
    const = lambda shp: pl.BlockSpec(shp, lambda bb, j: (0,) * len(shp), pipeline_mode=pl.Buffered(1))
    main = lambda c: pl.BlockSpec((1, tm, c), lambda bb, j: (bb, j, 0))
    halo = lambda c: pl.BlockSpec((1, HALO, c), lambda bb, j: (bb, jnp.maximum(j * per - 1, 0), 0))
    return pl.pallas_call(
        _ffn_kernel,
        grid=(nb, s // tm),
        in_specs=[main(d), halo(d), main(A_WIDTH), halo(A_WIDTH), main(B_WIDTH), halo(B_WIDTH),
                  const((HALO, d)), const((1, d)), const((1, d)), const((d, d)), const((1, d)), const((1, d)),
                  const((d, 2 * D_FF)), const((CONV_W, D_FF)), const((1, D_FF)),
                  const((D_FF, d)), const((1, d)), const((1, d))],
        out_specs=main(d),
        out_shape=jax.ShapeDtypeStruct((nb, s, d), F32),
        scratch_shapes=[pltpu.VMEM((tm, D_FF), BF16)],
        compiler_params=pltpu.CompilerParams(
            dimension_semantics=("arbitrary", "arbitrary"), vmem_limit_bytes=VMEM_LIMIT),
        name="ffn",
    )(x, x, a, a, b, b, h1_meta, lng, lnb, wo, g1, b1, wup, cw, cb, wdn, g2, b2)


def _permute_w_in(w):
    o = 0
    parts = {}
    for name, n in (("a_q", 512), ("a_k", 512), ("a_v", 512), ("i_q", 256), ("i_k", 64), ("i_w", 4),
                    ("b_q", 256), ("b_k", 256), ("b_v", 512), ("b_lr", 16), ("b_og", 512)):
        parts[name] = w[:, o:o + n]
        o += n
    zeros = jnp.zeros((w.shape[0], 128 - B_GATE_RANK - IDX_HEADS), w.dtype)
    cols = [parts["a_q"] * (A_HEAD_DIM ** -0.5 * LOG2E), parts["a_k"], parts["a_v"], parts["i_q"],
            parts["b_q"] * (B_KEY_DIM ** -0.5), parts["b_k"], parts["b_v"], parts["b_og"],
            parts["i_k"], parts["i_k"],
            parts["b_lr"], parts["i_w"] * (IDX_HEADS ** -0.5) * (IDX_DIM ** -0.5), zeros]
    return jnp.concatenate(cols, axis=1).astype(BF16)


def kernel(x, meta, ln_in_g, ln_in_b, w_in, w_gate_b, b_gate_b, attn_norm_g, gla_norm_g, w_out,
           ln1_g, ln1_b, w_up, conv_w, conv_b, w_down, ln2_g, ln2_b):
    nb, s, d = x.shape
    assert d == D_MODEL and w_in.shape[0] == DEPTH == 1
    assert s % ROW_TILE == 0 and s % KT == 0 and s % GLA_BLOCK == 0
    topk = min(TOPK_MAX, s // 4)
    l = 0
    r2 = lambda v: v.reshape(1, -1)
    lng, lnb = r2(ln_in_g), r2(ln_in_b)
    w_perm = _permute_w_in(w_in[l])
    wo = w_out[l].astype(BF16)
    wup = w_up[l].astype(BF16)
    wdn = w_down[l].astype(BF16)
    wg = w_gate_b[l].astype(BF16)
    bg = r2(b_gate_b[l])
    ng_a = r2(attn_norm_g[l])
    ng_b = r2(gla_norm_g[l])

    x_meta = jnp.zeros((1, META_ROWS, d), x.dtype).at[0, :N_META].set(meta.astype(x.dtype))
    names = ("aq", "akx", "avT", "iq", "bq", "bk", "bv", "bog", "ik2", "misc", "miscT")
    mo = dict(zip(names, _inproj(x_meta, lng, lnb, w_perm, META_ROWS, 0)))
    keep = (jnp.arange(META_ROWS) < N_META)
    m = {}
    for n in names:
        v = mo[n][0]
        if n in ("avT", "miscT"):
            m[n] = jnp.where(keep[None, :], v, jnp.zeros_like(v)).reshape((1,) * (n == "avT") + v.shape[-2:])
        else:
            m[n] = jnp.where(keep[:, None], v, jnp.zeros_like(v))

    aq, akx, avT, iq, bq, bk, bv, bog, ik2, misc, miscT = _inproj(x, lng, lnb, w_perm, ROW_TILE, N_META)
    a_out = _dsa(aq, iq, miscT, akx, avT, ik2, m, ng_a, topk, False)
    a_out_m = _dsa(m["aq"][None], m["iq"][None], m["miscT"][None], m["akx"][None], m["avT"][None],
                   m["ik2"][None], m, ng_a, topk, True)[0]
    b_out, b_out_m = _gla(bq, bk, bv, misc, bog, m, wg, bg, ng_b)

    g1, b1 = r2(ln1_g[l]), r2(ln1_b[l])
    b_out_m = jnp.zeros((META_ROWS, B_WIDTH), BF16).at[:CHUNK].set(b_out_m)
    h1_m = _outproj(x_meta, a_out_m[None], b_out_m[None], lng, lnb, wo, g1, b1, META_ROWS)
    return _ffn(x, a_out, b_out, h1_m[0, N_META - HALO:N_META], lng, lnb, wo, g1, b1,
                wup, conv_w[l], r2(conv_b[l]), wdn, r2(ln2_g[l]), r2(ln2_b[l]), ROW_TILE)
```

```python
import functools

import jax
import jax.numpy as jnp
from jax import lax
from jax.experimental import pallas as pl
from jax.experimental.pallas import tpu as pltpu

F32 = jnp.float32
BF16 = jnp.bfloat16
I32 = jnp.int32

D_MODEL = 1024
CHUNK = 64
N_META = 16
A_HEADS = 8
A_WIDTH = 512
A_HEAD_DIM = 64
IDX_HEADS = 4
IDX_DIM = 64
TOPK_MAX = 256
B_HEADS = 4
B_WIDTH = 512
B_VAL_DIM = 128
B_KEY_DIM = 64
B_GATE_RANK = 16
B_GATE_TAU = 16.0
D_FF = 2816
CONV_W = 3
LN_EPS = 1e-5
RMS_EPS = 1e-6
DEPTH = 1
ALPHA = (2.0 * DEPTH) ** 0.25

LANES = 128
ROW_TILE = 512
QB = 256
KT = 256
MT = 64
GROUP_ROWS = MT + KT
META_ROWS = 256
GLA_BLOCK = 256
FF_TILE = 256
HALO = 16
U_NEG_INF, U_POS_INF = 0x007F, 0xFF80
N_SLOPE_PARTS = 3
VT_ROWS = 80
KX_COLS = 2 * A_WIDTH
LOG2E = 1.4426950408889634
ALIBI_C = tuple(2.0 ** (-8.0 * (h + 1) / A_HEADS) * LOG2E for h in range(A_HEADS))
VMEM_LIMIT = 56 * 1024 * 1024

_COLS = {}
_off = 0
for _name, _w in (("aq", 512), ("ak", 512), ("av", 512), ("iq", 256), ("bq", 256), ("bk", 256),
                  ("bv", 512), ("bog", 512), ("ik2", 128), ("misc", 128)):
    _COLS[_name] = (_off, _w)
    _off += _w
PROJ_COLS = _off
MISC_LR = 0
MISC_IW = 16


def _layer_norm(x, g, b):
    mu = jnp.mean(x, axis=-1, keepdims=True)
    xc = x - mu
    var = jnp.mean(xc * xc, axis=-1, keepdims=True)
    return xc * lax.rsqrt(var + LN_EPS) * g + b


def _dot(a, b):
    return jnp.dot(a, b, preferred_element_type=F32)


def _dot_nt(a, b):
    return lax.dot_general(a, b, (((1,), (1,)), ((), ())), preferred_element_type=F32)


def _dot_tn(a, b):
    return lax.dot_general(a, b, (((0,), (0,)), ((), ())), preferred_element_type=F32)


def _inproj_kernel(x_ref, g_ref, b_ref, w_ref, aq_ref, akx_ref, avT_ref, iq_ref, bq_ref, bk_ref,
                   bv_ref, bog_ref, ik2_ref, misc_ref, miscT_ref, *, pos0, valid_rows):
    h = _layer_norm(x_ref[0], g_ref[...], b_ref[...])
    if valid_rows is not None:
        h = jnp.where(lax.broadcasted_iota(I32, h.shape, 0) < valid_rows, h, 0.0)
    hb = h.astype(BF16)
    tm = hb.shape[0]

    def proj(name):
        lo, w = _COLS[name]
        return _dot(hb, w_ref[:, lo:lo + w])

    aq_ref[0] = proj("aq").astype(BF16)
    ak = proj("ak")
    rows = lax.broadcasted_iota(I32, (tm, LANES), 0)
    lanes = lax.broadcasted_iota(I32, (tm, LANES), 1)
    pos = pos0 + pl.program_id(1) * tm + rows
    feat = jnp.where(lanes < 2 * N_SLOPE_PARTS, jnp.where(lanes % 2 == 0, pos >> 6, pos & 63), 0)
    feat = feat.astype(F32).astype(BF16)
    for p in range(A_HEADS // 2):
        akx_ref[0, :, 2 * p * LANES:(2 * p + 1) * LANES] = ak[:, p * LANES:(p + 1) * LANES].astype(BF16)
        akx_ref[0, :, (2 * p + 1) * LANES:(2 * p + 2) * LANES] = feat
    avT = proj("av").T
    ones = jnp.ones((VT_ROWS - A_HEAD_DIM, KT), BF16)
    for i in range(avT_ref.shape[1]):
        for hd in range(A_HEADS):
            avT_ref[0, i, hd * VT_ROWS:hd * VT_ROWS + A_HEAD_DIM, :] = (
                avT[hd * A_HEAD_DIM:(hd + 1) * A_HEAD_DIM, i * KT:(i + 1) * KT].astype(BF16))
            avT_ref[0, i, hd * VT_ROWS + A_HEAD_DIM:(hd + 1) * VT_ROWS, :] = ones
    iq_ref[0] = proj("iq").astype(BF16)
    bq_ref[0] = proj("bq").astype(BF16)
    bk_ref[0] = proj("bk").astype(BF16)
    bv_ref[0] = proj("bv").astype(BF16)
    bog_ref[0] = proj("bog")
    ik2_ref[0] = proj("ik2").astype(BF16)
    misc = proj("misc")
    misc_ref[0] = misc
    miscT_ref[0] = misc.T


def _inproj(x, ln_g, ln_b, w_perm, tm, pos0, valid_rows=None):
    nb, s, d = x.shape
    nt = s // tm
    row = lambda c: pl.BlockSpec((1, tm, c), lambda b, i: (b, i, 0))
    const = lambda shp: pl.BlockSpec(shp, lambda b, i: (0,) * len(shp))
    out_shape = (
        jax.ShapeDtypeStruct((nb, s, 512), BF16),
        jax.ShapeDtypeStruct((nb, s, KX_COLS), BF16),
        jax.ShapeDtypeStruct((nb, s // KT, A_HEADS * VT_ROWS, KT), BF16),
        jax.ShapeDtypeStruct((nb, s, 256), BF16),
        jax.ShapeDtypeStruct((nb, s, 256), BF16),
        jax.ShapeDtypeStruct((nb, s, 256), BF16),
        jax.ShapeDtypeStruct((nb, s, 512), BF16),
        jax.ShapeDtypeStruct((nb, s, 512), F32),
        jax.ShapeDtypeStruct((nb, s, 128), BF16),
        jax.ShapeDtypeStruct((nb, s, 128), F32),
        jax.ShapeDtypeStruct((nb, 128, s), F32),
    )
    out_specs = (
        row(512), row(KX_COLS),
        pl.BlockSpec((1, tm // KT, A_HEADS * VT_ROWS, KT), lambda b, i: (b, i, 0, 0)),
        row(256), row(256), row(256), row(512), row(512), row(128), row(128),
        pl.BlockSpec((1, 128, tm), lambda b, i: (b, 0, i)),
    )
    return pl.pallas_call(
        functools.partial(_inproj_kernel, pos0=pos0, valid_rows=valid_rows),
        grid=(nb, nt),
        in_specs=[row(d), const((1, d)), const((1, d)), const((d, PROJ_COLS))],
        out_specs=out_specs,
        out_shape=out_shape,
        compiler_params=pltpu.CompilerParams(
            dimension_semantics=("arbitrary", "arbitrary"), vmem_limit_bytes=VMEM_LIMIT),
        name="inproj",
    )(x, ln_g, ln_b, w_perm)


def _static_when(cond: bool):
    def deco(fn):
        if cond:
            fn()
    return deco


def _dsa_kernel(aq_ref, iq_ref, miscT_ref, akx_ref, avT_ref, ik2_ref,
                maq_ref, miq_ref, mmiscT_ref, makx_ref, mavT_ref, mik2_ref, qfeat_ref, ng_ref,
                out_ref,
                sc_s, scb_s, qm_s, iqm_s, iw_s, m_s, acc_s, bias_s, lg_s, p_s, neq_s, shift_s, corr_s,
                *, topk, is_meta):
    g = 0 if is_meta else pl.program_id(1) + 1
    n_past = 0 if is_meta else g - 1
    when_real = _static_when(not is_meta)
    when_meta = _static_when(is_meta)

    lane_half = lax.broadcasted_iota(I32, (1, LANES), 1) // A_HEAD_DIM

    def stage(aq, iq, miscT):
        for h in range(A_HEADS):
            p = h // 2
            qp = aq[:, p * LANES:(p + 1) * LANES]
            qm_s[h, :, 0:LANES] = jnp.where(lane_half == (h % 2), qp, jnp.zeros_like(qp))
            qm_s[h, :, LANES:2 * LANES] = jnp.broadcast_to(qfeat_ref[h:h + 1, :], (QB, LANES)).astype(BF16)
        for h in range(IDX_HEADS):
            p = h // 2
            qp = iq[:, p * LANES:(p + 1) * LANES]
            iqm_s[h] = jnp.where(lane_half == (h % 2), qp, jnp.zeros_like(qp))
        iw_s[...] = miscT[MISC_IW:MISC_IW + 8, :]

    lane = lax.broadcasted_iota(I32, (1, QB), 1)
    if is_meta:
        stage(maq_ref[...], miq_ref[...], mmiscT_ref[...])
        qpos = lane
    else:
        stage(aq_ref[0], iq_ref[0], miscT_ref[0])
        qpos = N_META + (g - 1) * QB + lane

    ninf = jnp.float32(-jnp.inf)

    def scores_for_tile(ik2_t, allowed):
        s = None
        for h in range(IDX_HEADS):
            sh = _dot_nt(ik2_t, iqm_s[h])
            term = iw_s[h:h + 1, :] * jnp.maximum(sh, 0.0)
            s = term if s is None else s + term
        s = jnp.where(s == 0.0, 0.0, s)
        return s if allowed is None else jnp.where(allowed, s, ninf)

    def store_scores(slot, r, s):
        sc_s[slot, 0:r, :] = s
        scb_s[slot, 0:r, :] = s.astype(BF16)

    mrow = lax.broadcasted_iota(I32, (MT, QB), 0)
    krow = lax.broadcasted_iota(I32, (KT, QB), 0)
    store_scores(0, MT, scores_for_tile(mik2_ref[0:MT, :], mrow < N_META))

    def fill_past(kt):
        store_scores(kt + 1, KT, scores_for_tile(ik2_ref[0, kt], None))

    def fill_pair(i, c):
        fill_past(2 * i)
        fill_past(2 * i + 1)
        return c

    @when_real
    def _():
        lax.fori_loop(0, n_past // 2, fill_pair, 0)

        @pl.when(n_past % 2 == 1)
        def _():
            fill_past(n_past - 1)

        store_scores(g, KT, scores_for_tile(ik2_ref[0, g - 1], (krow >> 6) <= (lane >> 6)))

    def count(arr_s, pred, pack):
        acc_t = jnp.int16 if pack == 16 else I32

        def cnt(tile):
            x = jnp.where(pred(tile), jnp.ones((), acc_t), jnp.zeros((), acc_t))
            x = x.reshape(tile.shape[0] // pack, pack, QB)
            parts = [x[j] for j in range(x.shape[0])]
            while len(parts) > 1:
                parts = [parts[j] + parts[j + 1] for j in range(0, len(parts), 2)]
            return parts[0]

        acc = cnt(arr_s[0, 0:MT, :])
        acc = lax.fori_loop(0, g, lambda kt, a: a + cnt(arr_s[kt + 1]), acc)
        return jnp.sum(acc.astype(I32), axis=0, keepdims=True)

    def ordered_to_bits(k, sign_bit, low_mask):
        return jnp.where(k >= sign_bit, k ^ sign_bit, k ^ low_mask)

    def coarse_body(i, u):
        cand_u = u | jnp.left_shift(jnp.int32(1), 15 - i)
        finite = (cand_u >= U_NEG_INF) & (cand_u <= U_POS_INF)
        bits = ordered_to_bits(jnp.clip(cand_u, U_NEG_INF, U_POS_INF), 0x8000, 0xFFFF)
        cand = lax.bitcast_convert_type(jnp.left_shift(bits, 16), F32).astype(BF16)
        cnt = count(scb_s, lambda tile: tile >= cand, 16)
        return jnp.where(finite & (cnt >= topk), cand_u, u)

    u1 = lax.fori_loop(0, 16, coarse_body, jnp.zeros((1, QB), I32))
    few = u1 == 0
    t1_bits = jnp.left_shift(ordered_to_bits(jnp.where(few, 0x8000, u1), 0x8000, 0xFFFF), 16)
    base = (t1_bits ^ ((t1_bits >> 31) & 0x7FFFFFFF)) - 0x8000

    def key_to_f32(k):
        return lax.bitcast_convert_type(k ^ ((k >> 31) & 0x7FFFFFFF), F32)

    def fine_body(i, o):
        cand_o = o | jnp.left_shift(jnp.int32(1), 16 - i)
        cand = key_to_f32(base + cand_o)
        cnt = count(sc_s, lambda tile: tile >= cand, 8)
        return jnp.where(cnt >= topk, cand_o, o)

    o2 = lax.fori_loop(0, 17, fine_body, jnp.zeros((1, QB), I32))
    thr = jnp.where(few, ninf, key_to_f32(base + o2))
    n_gt = count(sc_s, lambda tile: tile > thr, 8)
    need = jnp.where(few, 0, topk - n_gt).astype(F32)

    m_s[...] = jnp.full(m_s.shape, -jnp.inf, F32)
    acc_s[...] = jnp.zeros(acc_s.shape, F32)

    def offsets(subs):
        offs, row0 = [], 0
        for sub in subs:
            offs.append((row0, sub[0].shape[0]))
            row0 += sub[0].shape[0]
        return offs, row0

    def bias_stage(subs):
        offs, _ = offsets(subs)
        n_eq_after = None
        for (sc_t, _, _, _, n_eq_before), (o, r) in zip(subs, offs):
            if n_eq_before is None:
                n_eq_before = n_eq_after
            ri = lax.broadcasted_iota(I32, (r, r), 0)
            ci = lax.broadcasted_iota(I32, (r, r), 1)
            lower = jnp.where(ri > ci, 1.0, 0.0).astype(BF16)
            eq = sc_t == thr
            rank = _dot(lower, jnp.where(eq, 1.0, 0.0).astype(BF16)) + n_eq_before
            bias_s[o:o + r, :] = jnp.where(
                sc_t > thr, 0.0, jnp.where(eq, jnp.where(rank < need, 0.0, ninf), ninf))
            n_eq_after = rank[r - 1:r, :] + jnp.where(eq[r - 1:r, :], 1.0, 0.0)
        return n_eq_after

    def logits_head(subs, h):
        offs, _ = offsets(subs)
        m_old = m_s[h]
        m_new = m_old
        for (_, kx_fn, _, after, _), (o, r) in zip(subs, offs):
            lg = _dot_nt(kx_fn(h // 2), qm_s[h]) + bias_s[o:o + r, :]
            if after is not None:
                lg = lg + ALIBI_C[h] * after
            lg_s[h, o:o + r, :] = lg
            m_new = jnp.maximum(m_new, jnp.max(lg, axis=0, keepdims=True))
        m_s[h] = m_new
        m_safe = jnp.where(m_new == ninf, 0.0, m_new)
        shift_s[h] = m_safe
        corr_s[h] = jnp.exp2(m_old - m_safe)

    def softmax_pv_head(subs, h):
        offs, rows = offsets(subs)
        p_s[h, 0:rows, :] = jnp.exp2(lg_s[h, 0:rows, :] - shift_s[h]).astype(BF16)
        acc = acc_s[h] * corr_s[h]
        for (_, _, vT_fn, _, _), (o, r) in zip(subs, offs):
            acc = acc + _dot(vT_fn(h), p_s[h, o:o + r, :])
        acc_s[h] = acc

    def run_stages(done, nxt):
        n_eq_after = bias_stage(nxt) if nxt is not None else None
        for h in range(A_HEADS):
            if done is not None:
                softmax_pv_head(done, h)
            if nxt is not None:
                logits_head(nxt, h)
        return n_eq_after

    def after_term(kpos):
        return jnp.minimum(2 * (qpos - kpos), 0).astype(F32)

    def meta_sub():
        return (sc_s[0, 0:MT, :],
                lambda p: makx_ref[0:MT, 2 * p * LANES:(2 * p + 2) * LANES],
                lambda h: mavT_ref[0, h * VT_ROWS:(h + 1) * VT_ROWS, 0:MT],
                after_term(mrow), jnp.zeros((1, QB), F32))

    def real_sub(kt, after, n_eq_before):
        return (sc_s[kt + 1],
                lambda p: akx_ref[0, kt, :, 2 * p * LANES:(2 * p + 2) * LANES],
                lambda h: avT_ref[0, kt, h * VT_ROWS:(h + 1) * VT_ROWS, :],
                after, n_eq_before)

    neq_s[...] = jnp.sum(jnp.where(sc_s[0, 0:MT, :] == thr, 1.0, 0.0), axis=0, keepdims=True)

    def past(kt):
        return [real_sub(kt, None, neq_s[...])]

    def last_group():
        return [meta_sub(), real_sub(g - 1, after_term(N_META + (g - 1) * KT + krow), neq_s[...])]

    @when_real
    def _():
        @pl.when(n_past > 0)
        def _():
            neq_s[...] = run_stages(None, past(0))

        def pipe_body(i, c):
            neq_s[...] = run_stages(past(i), past(i + 1))
            return c

        lax.fori_loop(0, jnp.maximum(n_past - 1, 0), pipe_body, 0)

        @pl.when(n_past > 0)
        def _():
            run_stages(past(n_past - 1), last_group())

        @pl.when(n_past == 0)
        def _():
            run_stages(None, last_group())

        run_stages(last_group(), None)

    @when_meta
    def _():
        run_stages(None, [meta_sub()])
        run_stages([meta_sub()], None)

    outs = []
    for h in range(A_HEADS):
        o = acc_s[h, 0:A_HEAD_DIM, :] / acc_s[h, A_HEAD_DIM:A_HEAD_DIM + 1, :]
        ms = jnp.mean(o * o, axis=0, keepdims=True)
        outs.append(o * lax.rsqrt(ms + RMS_EPS))
    res = (jnp.concatenate(outs, axis=0).T * ng_ref[...]).astype(BF16)

    out_ref[0] = res


def _alibi_query_features():
    rows = []
    for c in ALIBI_C:
        rest = jnp.float32(c)
        lanes = []
        for _ in range(N_SLOPE_PARTS):
            part = rest.astype(BF16).astype(F32)
            lanes += [part * CHUNK, part]
            rest = rest - part
        rows.append(jnp.stack(lanes + [jnp.float32(0.0)] * (LANES - len(lanes))))
    return jnp.stack(rows)


def _dsa(aq, iq, miscT, akx, avT, ik2, m, ng, topk, is_meta):
    nb, s, _ = aq.shape
    nq = s // QB
    nkt = s // KT
    vt = A_HEADS * VT_ROWS
    qidx = lambda b, i: (b, i, 0)
    const = lambda shp: pl.BlockSpec(shp, lambda b, i: (0,) * len(shp))
    in_specs = [
        pl.BlockSpec((1, QB, 512), qidx),
        pl.BlockSpec((1, QB, 256), qidx),
        pl.BlockSpec((1, 128, QB), lambda b, i: (b, 0, i)),
        pl.BlockSpec((1, nkt, KT, KX_COLS), lambda b, i: (b, 0, 0, 0)),
        pl.BlockSpec((1, nkt, vt, KT), lambda b, i: (b, 0, 0, 0)),
        pl.BlockSpec((1, nkt, KT, 128), lambda b, i: (b, 0, 0, 0)),
        const((META_ROWS, 512)), const((META_ROWS, 256)), const((128, META_ROWS)),
        const((META_ROWS, KX_COLS)), const((1, vt, KT)), const((META_ROWS, 128)),
        const((A_HEADS, LANES)), const((1, 512)),
    ]
    out_shape = jax.ShapeDtypeStruct((nb, s, 512), BF16)
    out_specs = pl.BlockSpec((1, QB, 512), qidx)
    scratch = [
        pltpu.VMEM((nkt + 1, KT, QB), F32),
        pltpu.VMEM((nkt + 1, KT, QB), BF16),
        pltpu.VMEM((A_HEADS, QB, 2 * LANES), BF16),
        pltpu.VMEM((IDX_HEADS, QB, LANES), BF16),
        pltpu.VMEM((8, QB), F32),
        pltpu.VMEM((A_HEADS, 1, QB), F32),
        pltpu.VMEM((A_HEADS, VT_ROWS, QB), F32),
        pltpu.VMEM((GROUP_ROWS, QB), F32),
        pltpu.VMEM((A_HEADS, GROUP_ROWS, QB), F32),
        pltpu.VMEM((A_HEADS, GROUP_ROWS, QB), BF16),
        pltpu.VMEM((1, QB), F32),
        pltpu.VMEM((A_HEADS, 1, QB), F32),
        pltpu.VMEM((A_HEADS, 1, QB), F32),
    ]
    return pl.pallas_call(
        functools.partial(_dsa_kernel, topk=topk, is_meta=is_meta),
        grid=(nb, nq),
        in_specs=in_specs,
        out_specs=out_specs,
        out_shape=out_shape,
        scratch_shapes=scratch,
        compiler_params=pltpu.CompilerParams(
            dimension_semantics=("arbitrary", "arbitrary"), vmem_limit_bytes=VMEM_LIMIT),
        name="dsa",
    )(aq, iq, miscT, akx.reshape(nb, nkt, KT, KX_COLS), avT, ik2.reshape(nb, nkt, KT, 128),
      m["aq"], m["iq"], m["miscT"], m["akx"], m["avT"], m["ik2"], _alibi_query_features(), ng)


def _chunk_cumsum_rows(x):
    row_in_chunk = lax.broadcasted_iota(I32, x.shape, 0) & (CHUNK - 1)
    sh = 1
    while sh < CHUNK:
        x = x + jnp.where(row_in_chunk >= sh, pltpu.roll(x, sh, 0), 0.0)
        sh *= 2
    return x


def _rows_of_chunk(rows_per_chunk):
    return jnp.concatenate([jnp.broadcast_to(r, (CHUNK, r.shape[1])) for r in rows_per_chunk], axis=0)


def _gla_block(q, k, v, lr, og, wg, bg, ng, st_s, valid_rows):
    nrow = q.shape[0]
    nch = nrow // CHUNK
    z = _dot(lr.astype(BF16), wg) + bg
    logg = (jnp.minimum(z, 0.0) - jnp.log(1.0 + jnp.exp(-jnp.abs(z)))) / B_GATE_TAU
    if valid_rows is not None:
        rows = lax.broadcasted_iota(I32, logg.shape, 0)
        logg = jnp.where(rows < valid_rows, logg, 0.0)
    b = _chunk_cumsum_rows(logg)
    b_last_rows = [b[(c + 1) * CHUNK - 1:(c + 1) * CHUNK, :] for c in range(nch)]
    b_mid = _rows_of_chunk([b[c * CHUNK + CHUNK // 2 - 1:c * CHUNK + CHUNK // 2, :] for c in range(nch)])
    b_last = _rows_of_chunk(b_last_rows)
    q_in = q * jnp.exp(b)
    q_ic = q * jnp.exp(b - b_mid)
    k_ic = k * jnp.exp(b_mid - b)
    k_st = k * jnp.exp(b_last - b)
    decay = [jnp.exp(r) for r in b_last_rows]

    lane_half = lax.broadcasted_iota(I32, (1, LANES), 1) // B_KEY_DIM
    ri = lax.broadcasted_iota(I32, (nrow, nrow), 0)
    ci = lax.broadcasted_iota(I32, (nrow, nrow), 1)
    causal = ((ri >> 6) == (ci >> 6)) & (ci <= ri)
    outs = []
    for p in range(B_HEADS // 2):
        sl = slice(p * LANES, (p + 1) * LANES)
        heads = (2 * p, 2 * p + 1)
        v_h = [v[:, h * B_VAL_DIM:(h + 1) * B_VAL_DIM] for h in heads]
        k_st_h = [jnp.where(lane_half == hh, k_st[:, sl], 0.0).astype(BF16) for hh in range(2)]
        states = [st_s[p]]
        for c in range(nch):
            rs = slice(c * CHUNK, (c + 1) * CHUNK)
            upd = _dot_tn(v_h[0][rs], k_st_h[0][rs]) + _dot_tn(v_h[1][rs], k_st_h[1][rs])
            states.append(states[c] * decay[c][:, sl] + upd)
        st_s[p] = states[nch]
        k_ic_p = k_ic[:, sl].astype(BF16)
        for hh, h in enumerate(heads):
            hm = lane_half == hh
            a = _dot_nt(jnp.where(hm, q_ic[:, sl], 0.0).astype(BF16), k_ic_p)
            o = _dot(jnp.where(causal, a, 0.0).astype(BF16), v_h[hh])
            q_in_h = jnp.where(hm, q_in[:, sl], 0.0).astype(BF16)
            o = o + jnp.concatenate(
                [_dot_nt(q_in_h[c * CHUNK:(c + 1) * CHUNK], states[c].astype(BF16)) for c in range(nch)], axis=0)
            ms = jnp.mean(o * o, axis=-1, keepdims=True)
            on = o * lax.rsqrt(ms + RMS_EPS) * ng[:, h * B_VAL_DIM:(h + 1) * B_VAL_DIM]
            og_h = og[:, h * B_VAL_DIM:(h + 1) * B_VAL_DIM]
            outs.append(on * (og_h * jax.nn.sigmoid(og_h)))
    return outs


def _gla_kernel(bq_ref, bk_ref, bv_ref, misc_ref, bog_ref,
                mbq_ref, mbk_ref, mbv_ref, mmisc_ref, mbog_ref,
                wg_ref, bg_ref, ng_ref, out_ref, mout_ref, st_s):
    j = pl.program_id(1)
    wg = wg_ref[...]
    bg = bg_ref[...]
    ng = ng_ref[...]

    @pl.when(j == 0)
    def _():
        st_s[...] = jnp.zeros(st_s.shape, F32)
        outs = _gla_block(mbq_ref[0:CHUNK, :].astype(F32), mbk_ref[0:CHUNK, :].astype(F32),
                          mbv_ref[0:CHUNK, :], mmisc_ref[0:CHUNK, MISC_LR:MISC_LR + B_GATE_RANK],
                          mbog_ref[0:CHUNK, :], wg, bg, ng, st_s, N_META)
        mout_ref[CHUNK:, :] = jnp.zeros((META_ROWS - CHUNK, B_WIDTH), BF16)
        for h, o in enumerate(outs):
            mout_ref[0:CHUNK, h * B_VAL_DIM:(h + 1) * B_VAL_DIM] = o.astype(BF16)

    outs = _gla_block(bq_ref[0].astype(F32), bk_ref[0].astype(F32), bv_ref[0],
                      misc_ref[0, :, MISC_LR:MISC_LR + B_GATE_RANK], bog_ref[0], wg, bg, ng, st_s, None)
    for h, o in enumerate(outs):
        out_ref[0, :, h * B_VAL_DIM:(h + 1) * B_VAL_DIM] = o.astype(BF16)


def _gla(bq, bk, bv, misc, bog, m, wg, bg, ng):
    nb, s, _ = bq.shape
    row = lambda c: pl.BlockSpec((1, GLA_BLOCK, c), lambda b, j: (b, j, 0))
    const = lambda shp: pl.BlockSpec(shp, lambda b, j: (0,) * len(shp))
    return pl.pallas_call(
        _gla_kernel,
        grid=(nb, s // GLA_BLOCK),
        in_specs=[row(256), row(256), row(512), row(128), row(512),
                  const((META_ROWS, 256)), const((META_ROWS, 256)), const((META_ROWS, 512)),
                  const((META_ROWS, 128)), const((META_ROWS, 512)),
                  const((B_GATE_RANK, 256)), const((1, 256)), const((1, 512))],
        out_specs=(row(512), const((META_ROWS, 512))),
        out_shape=(jax.ShapeDtypeStruct((nb, s, 512), BF16), jax.ShapeDtypeStruct((META_ROWS, 512), BF16)),
        scratch_shapes=[pltpu.VMEM((B_HEADS // 2, LANES, LANES), F32)],
        compiler_params=pltpu.CompilerParams(
            dimension_semantics=("arbitrary", "arbitrary"), vmem_limit_bytes=VMEM_LIMIT),
        name="gla",
    )(bq, bk, bv, misc, bog, m["bq"], m["bk"], m["bv"], m["misc"], m["bog"], wg, bg, ng)


def _outproj_kernel(x_ref, a_ref, b_ref, lng_ref, lnb_ref, wo_ref, g1_ref, b1_ref, out_ref):
    h = _layer_norm(x_ref[0], lng_ref[...], lnb_ref[...])
    mixed = _dot(a_ref[0], wo_ref[0:A_WIDTH, :]) + _dot(b_ref[0], wo_ref[A_WIDTH:, :])
    out_ref[0] = _layer_norm(ALPHA * h + mixed, g1_ref[...], b1_ref[...])


def _outproj(x, a, b, lng, lnb, wo, g1, b1, tm):
    nb, s, d = x.shape
    row = lambda c: pl.BlockSpec((1, tm, c), lambda bb, i: (bb, i, 0))
    const = lambda shp: pl.BlockSpec(shp, lambda bb, i: (0,) * len(shp))
    return pl.pallas_call(
        _outproj_kernel,
        grid=(nb, s // tm),
        in_specs=[row(d), row(512), row(512), const((1, d)), const((1, d)), const((d, d)),
                  const((1, d)), const((1, d))],
        out_specs=row(d),
        out_shape=jax.ShapeDtypeStruct((nb, s, d), F32),
        compiler_params=pltpu.CompilerParams(
            dimension_semantics=("arbitrary", "arbitrary"), vmem_limit_bytes=VMEM_LIMIT),
        name="outproj",
    )(x, a, b, lng, lnb, wo, g1, b1)


def _gelu_tanh(x):
    c = 0.7978845608028654
    return 0.5 * x * (1.0 + jnp.tanh(c * (x + 0.044715 * (x * x * x))))


def _ffn_kernel(x_ref, xh_ref, a_ref, ah_ref, b_ref, bh_ref, hm_ref, lng_ref, lnb_ref, wo_ref, g1_ref, b1_ref,
                wup_ref, cw_ref, cb_ref, wdn_ref, g2_ref, b2_ref, out_ref, y_s):
    j = pl.program_id(1)
    def mix(x, a_mix, b_mix):
        mixed = _dot(a_mix, wo_ref[0:A_WIDTH, :]) + _dot(b_mix, wo_ref[A_WIDTH:, :])
        return _layer_norm(ALPHA * _layer_norm(x, lng_ref[...], lnb_ref[...]) + mixed, g1_ref[...], b1_ref[...])

    half = x_ref.shape[1] // 2
    h1_lo = mix(jnp.concatenate([xh_ref[0], x_ref[0, 0:half, :]], axis=0),
                jnp.concatenate([ah_ref[0], a_ref[0, 0:half, :]], axis=0),
                jnp.concatenate([bh_ref[0], b_ref[0, 0:half, :]], axis=0))
    h1_hi = mix(x_ref[0, half:, :], a_ref[0, half:, :], b_ref[0, half:, :])
    h = jnp.concatenate([h1_lo[HALO:, :], h1_hi], axis=0)
    halo = jnp.where(j == 0, hm_ref[...], h1_lo[0:HALO, :])
    hb = jnp.concatenate([halo, h], axis=0).astype(BF16)
    for f in range(D_FF // FF_TILE):
        fs = slice(f * FF_TILE, (f + 1) * FF_TILE)
        a = _dot(hb, wup_ref[:, fs])
        gate = _dot(hb[HALO:, :], wup_ref[:, D_FF + f * FF_TILE:D_FF + (f + 1) * FF_TILE])
        cw = cw_ref[:, fs]
        conv = cb_ref[:, fs] + cw[CONV_W - 1:CONV_W, :] * a[HALO:, :]
        for back in range(1, CONV_W):
            tap = cw[CONV_W - 1 - back:CONV_W - back, :]
            conv = conv + tap * pltpu.roll(a, back, 0)[HALO:, :]
        y_s[:, fs] = (_gelu_tanh(conv) * gate).astype(BF16)
    ffn = _dot(y_s[...], wdn_ref[...])
    out_ref[0] = _layer_norm(ALPHA * h + ffn, g2_ref[...], b2_ref[...])


def _ffn(x, a, b, h1_meta, lng, lnb, wo, g1, b1, wup, cw, cb, wdn, g2, b2, tm):
    nb, s, d = x.shape
    per = tm // HALO
    const = lambda shp: pl.BlockSpec(shp, lambda bb, j: (0,) * len(shp), pipeline_mode=pl.Buffered(1))
    main = lambda c: pl.BlockSpec((1, tm, c), lambda bb, j: (bb, j, 0))
    halo = lambda c: pl.BlockSpec((1, HALO, c), lambda bb, j: (bb, jnp.maximum(j * per - 1, 0), 0))
    return pl.pallas_call(
        _ffn_kernel,
        grid=(nb, s // tm),
        in_specs=[main(d), halo(d), main(A_WIDTH), halo(A_WIDTH), main(B_WIDTH), halo(B_WIDTH),
                  const((HALO, d)), const((1, d)), const((1, d)), const((d, d)), const((1, d)), const((1, d)),
                  const((d, 2 * D_FF)), const((CONV_W, D_FF)), const((1, D_FF)),
                  const((D_FF, d)), const((1, d)), const((1, d))],
        out_specs=main(d),
        out_shape=jax.ShapeDtypeStruct((nb, s, d), F32),
        scratch_shapes=[pltpu.VMEM((tm, D_FF), BF16)],
        compiler_params=pltpu.CompilerParams(
            dimension_semantics=("arbitrary", "arbitrary"), vmem_limit_bytes=VMEM_LIMIT),
        name="ffn",
    )(x, x, a, a, b, b, h1_meta, lng, lnb, wo, g1, b1, wup, cw, cb, wdn, g2, b2)


def _permute_w_in(w):
    o = 0
    parts = {}
    for name, n in (("a_q", 512), ("a_k", 512), ("a_v", 512), ("i_q", 256), ("i_k", 64), ("i_w", 4),
                    ("b_q", 256), ("b_k", 256), ("b_v", 512), ("b_lr", 16), ("b_og", 512)):
        parts[name] = w[:, o:o + n]
        o += n
    zeros = jnp.zeros((w.shape[0], 128 - B_GATE_RANK - IDX_HEADS), w.dtype)
    cols = [parts["a_q"] * (A_HEAD_DIM ** -0.5 * LOG2E), parts["a_k"], parts["a_v"], parts["i_q"],
            parts["b_q"] * (B_KEY_DIM ** -0.5), parts["b_k"], parts["b_v"], parts["b_og"],
            parts["i_k"], parts["i_k"],
            parts["b_lr"], parts["i_w"] * (IDX_HEADS ** -0.5) * (IDX_DIM ** -0.5), zeros]
    return jnp.concatenate(cols, axis=1).astype(BF16)


def kernel(x, meta, ln_in_g, ln_in_b, w_in, w_gate_b, b_gate_b, attn_norm_g, gla_norm_g, w_out,
           ln1_g, ln1_b, w_up, conv_w, conv_b, w_down, ln2_g, ln2_b):
    nb, s, d = x.shape
    assert d == D_MODEL and w_in.shape[0] == DEPTH == 1
    assert s % ROW_TILE == 0 and s % KT == 0 and s % GLA_BLOCK == 0
    topk = min(TOPK_MAX, s // 4)
    l = 0
    r2 = lambda v: v.reshape(1, -1)
    lng, lnb = r2(ln_in_g), r2(ln_in_b)
    w_perm = _permute_w_in(w_in[l])
    wo = w_out[l].astype(BF16)
    wup = w_up[l].astype(BF16)
    wdn = w_down[l].astype(BF16)
    wg = w_gate_b[l].astype(BF16)
    bg = r2(b_gate_b[l])
    ng_a = r2(attn_norm_g[l])
    ng_b = r2(gla_norm_g[l])

    x_meta = jnp.zeros((1, META_ROWS, d), x.dtype).at[0, :N_META].set(meta.astype(x.dtype))
    names = ("aq", "akx", "avT", "iq", "bq", "bk", "bv", "bog", "ik2", "misc", "miscT")
    m = {n: v[0] for n, v in zip(names, _inproj(x_meta, lng, lnb, w_perm, META_ROWS, 0, valid_rows=N_META))}

    aq, akx, avT, iq, bq, bk, bv, bog, ik2, misc, miscT = _inproj(x, lng, lnb, w_perm, ROW_TILE, N_META)
    a_out = _dsa(aq, iq, miscT, akx, avT, ik2, m, ng_a, topk, False)
    a_out_m = _dsa(m["aq"][None], m["iq"][None], m["miscT"][None], m["akx"][None], m["avT"][None],
                   m["ik2"][None], m, ng_a, topk, True)[0]
    b_out, b_out_m = _gla(bq, bk, bv, misc, bog, m, wg, bg, ng_b)

    g1, b1 = r2(ln1_g[l]), r2(ln1_b[l])
    h1_m = _outproj(x_meta, a_out_m[None], b_out_m[None], lng, lnb, wo, g1, b1, META_ROWS)
    return _ffn(x, a_out, b_out, h1_m[0, N_META - HALO:N_META], lng, lnb, wo, g1, b1,
                wup, conv_w[l], r2(conv_b[l]), wdn, r2(ln2_g[l]), r2(ln2_b[l]), ROW_TILE)
```

```python
import functools

import jax
import jax.numpy as jnp
from jax import lax
from jax.experimental import pallas as pl
from jax.experimental.pallas import tpu as pltpu

F32 = jnp.float32
BF16 = jnp.bfloat16
I32 = jnp.int32

D_MODEL = 1024
CHUNK = 64
N_META = 16
A_HEADS = 8
A_WIDTH = 512
A_HEAD_DIM = 64
IDX_HEADS = 4
IDX_DIM = 64
TOPK_MAX = 256
B_HEADS = 4
B_WIDTH = 512
B_VAL_DIM = 128
B_KEY_DIM = 64
B_GATE_RANK = 16
B_GATE_TAU = 16.0
D_FF = 2816
CONV_W = 3
LN_EPS = 1e-5
RMS_EPS = 1e-6
DEPTH = 1
ALPHA = (2.0 * DEPTH) ** 0.25

LANES = 128
ROW_TILE = 512
QB = 256
KT = 256
MT = 64
GROUP_ROWS = MT + KT
META_ROWS = 256
GLA_BLOCK = 256
FF_TILE = 256
HALO = 16
U_NEG_INF, U_POS_INF = 0x007F, 0xFF80
N_SLOPE_PARTS = 3
VT_ROWS = 80
KX_COLS = 2 * A_WIDTH
LOG2E = 1.4426950408889634
ALIBI_C = tuple(2.0 ** (-8.0 * (h + 1) / A_HEADS) * LOG2E for h in range(A_HEADS))
VMEM_LIMIT = 56 * 1024 * 1024

_COLS = {}
_off = 0
for _name, _w in (("aq", 512), ("ak", 512), ("av", 512), ("iq", 256), ("bq", 256), ("bk", 256),
                  ("bv", 512), ("bog", 512), ("misc", 128)):
    _COLS[_name] = (_off, _w)
    _off += _w
PROJ_COLS = _off
MISC_LR = 0
MISC_IW = 16
MISC_IK = 64


def _layer_norm(x, g, b):
    mu = jnp.mean(x, axis=-1, keepdims=True)
    xc = x - mu
    var = jnp.mean(xc * xc, axis=-1, keepdims=True)
    return xc * lax.rsqrt(var + LN_EPS) * g + b


def _dot(a, b):
    return jnp.dot(a, b, preferred_element_type=F32)


def _dot_nt(a, b):
    return lax.dot_general(a, b, (((1,), (1,)), ((), ())), preferred_element_type=F32)


def _dot_tn(a, b):
    return lax.dot_general(a, b, (((0,), (0,)), ((), ())), preferred_element_type=F32)


def _inproj_kernel(x_ref, g_ref, b_ref, w_ref, aq_ref, akx_ref, avT_ref, iq_ref, bq_ref, bk_ref,
                   bv_ref, bog_ref, ik2_ref, misc_ref, miscT_ref, *, pos0, valid_rows):
    h = _layer_norm(x_ref[0], g_ref[...], b_ref[...])
    if valid_rows is not None:
        h = jnp.where(lax.broadcasted_iota(I32, h.shape, 0) < valid_rows, h, 0.0)
    hb = h.astype(BF16)
    tm = hb.shape[0]

    def proj(name):
        lo, w = _COLS[name]
        return _dot(hb, w_ref[:, lo:lo + w])

    aq_ref[0] = proj("aq").astype(BF16)
    ak = proj("ak")
    rows = lax.broadcasted_iota(I32, (tm, LANES), 0)
    lanes = lax.broadcasted_iota(I32, (tm, LANES), 1)
    pos = pos0 + pl.program_id(1) * tm + rows
    feat = jnp.where(lanes < 2 * N_SLOPE_PARTS, jnp.where(lanes % 2 == 0, pos >> 6, pos & 63), 0)
    feat = feat.astype(F32).astype(BF16)
    for p in range(A_HEADS // 2):
        akx_ref[0, :, 2 * p * LANES:(2 * p + 1) * LANES] = ak[:, p * LANES:(p + 1) * LANES].astype(BF16)
        akx_ref[0, :, (2 * p + 1) * LANES:(2 * p + 2) * LANES] = feat
    avT = proj("av").T
    ones = jnp.ones((VT_ROWS - A_HEAD_DIM, KT), BF16)
    for i in range(avT_ref.shape[1]):
        for hd in range(A_HEADS):
            avT_ref[0, i, hd * VT_ROWS:hd * VT_ROWS + A_HEAD_DIM, :] = (
                avT[hd * A_HEAD_DIM:(hd + 1) * A_HEAD_DIM, i * KT:(i + 1) * KT].astype(BF16))
            avT_ref[0, i, hd * VT_ROWS + A_HEAD_DIM:(hd + 1) * VT_ROWS, :] = ones
    iq_ref[0] = proj("iq").astype(BF16)
    bq_ref[0] = proj("bq").astype(BF16)
    bk_ref[0] = proj("bk").astype(BF16)
    bv_ref[0] = proj("bv").astype(BF16)
    bog_ref[0] = proj("bog")
    misc = proj("misc")
    misc_ref[0] = misc
    miscT_ref[0] = misc.T
    ik_lo = pltpu.roll(misc, LANES - MISC_IK, 1)
    ik2_ref[0] = jnp.where(lanes < MISC_IK, ik_lo, misc).astype(BF16)


def _inproj(x, ln_g, ln_b, w_perm, tm, pos0, valid_rows=None):
    nb, s, d = x.shape
    nt = s // tm
    row = lambda c: pl.BlockSpec((1, tm, c), lambda b, i: (b, i, 0))
    const = lambda shp: pl.BlockSpec(shp, lambda b, i: (0,) * len(shp))
    out_shape = (
        jax.ShapeDtypeStruct((nb, s, 512), BF16),
        jax.ShapeDtypeStruct((nb, s, KX_COLS), BF16),
        jax.ShapeDtypeStruct((nb, s // KT, A_HEADS * VT_ROWS, KT), BF16),
        jax.ShapeDtypeStruct((nb, s, 256), BF16),
        jax.ShapeDtypeStruct((nb, s, 256), BF16),
        jax.ShapeDtypeStruct((nb, s, 256), BF16),
        jax.ShapeDtypeStruct((nb, s, 512), BF16),
        jax.ShapeDtypeStruct((nb, s, 512), F32),
        jax.ShapeDtypeStruct((nb, s, 128), BF16),
        jax.ShapeDtypeStruct((nb, s, 128), F32),
        jax.ShapeDtypeStruct((nb, 128, s), F32),
    )
    out_specs = (
        row(512), row(KX_COLS),
        pl.BlockSpec((1, tm // KT, A_HEADS * VT_ROWS, KT), lambda b, i: (b, i, 0, 0)),
        row(256), row(256), row(256), row(512), row(512), row(128), row(128),
        pl.BlockSpec((1, 128, tm), lambda b, i: (b, 0, i)),
    )
    return pl.pallas_call(
        functools.partial(_inproj_kernel, pos0=pos0, valid_rows=valid_rows),
        grid=(nb, nt),
        in_specs=[row(d), const((1, d)), const((1, d)), const((d, PROJ_COLS))],
        out_specs=out_specs,
        out_shape=out_shape,
        compiler_params=pltpu.CompilerParams(
            dimension_semantics=("arbitrary", "arbitrary"), vmem_limit_bytes=VMEM_LIMIT),
        name="inproj",
    )(x, ln_g, ln_b, w_perm)


def _static_when(cond: bool):
    def deco(fn):
        if cond:
            fn()
    return deco


def _dsa_kernel(aq_ref, iq_ref, miscT_ref, akx_ref, avT_ref, ik2_ref,
                maq_ref, miq_ref, mmiscT_ref, makx_ref, mavT_ref, mik2_ref, qfeat_ref, ng_ref,
                out_ref,
                sc_s, scb_s, qm_s, iqm_s, iw_s, m_s, acc_s, bias_s, lg_s, p_s, neq_s, shift_s, corr_s,
                *, topk, is_meta):
    g = 0 if is_meta else pl.program_id(1) + 1
    n_past = 0 if is_meta else g - 1
    when_real = _static_when(not is_meta)
    when_meta = _static_when(is_meta)

    lane_half = lax.broadcasted_iota(I32, (1, LANES), 1) // A_HEAD_DIM

    def stage(aq, iq, miscT):
        for h in range(A_HEADS):
            p = h // 2
            qp = aq[:, p * LANES:(p + 1) * LANES]
            qm_s[h, :, 0:LANES] = jnp.where(lane_half == (h % 2), qp, jnp.zeros_like(qp))
            qm_s[h, :, LANES:2 * LANES] = jnp.broadcast_to(qfeat_ref[h:h + 1, :], (QB, LANES)).astype(BF16)
        for h in range(IDX_HEADS):
            p = h // 2
            qp = iq[:, p * LANES:(p + 1) * LANES]
            iqm_s[h] = jnp.where(lane_half == (h % 2), qp, jnp.zeros_like(qp))
        iw_s[...] = miscT[MISC_IW:MISC_IW + 8, :]

    lane = lax.broadcasted_iota(I32, (1, QB), 1)
    if is_meta:
        stage(maq_ref[...], miq_ref[...], mmiscT_ref[...])
        qpos = lane
    else:
        stage(aq_ref[0], iq_ref[0], miscT_ref[0])
        qpos = N_META + (g - 1) * QB + lane

    ninf = jnp.float32(-jnp.inf)

    def scores_for_tile(ik2_t, allowed):
        s = None
        for h in range(IDX_HEADS):
            sh = _dot_nt(ik2_t, iqm_s[h])
            term = iw_s[h:h + 1, :] * jnp.maximum(sh, 0.0)
            s = term if s is None else s + term
        s = jnp.where(s == 0.0, 0.0, s)
        return s if allowed is None else jnp.where(allowed, s, ninf)

    def store_scores(slot, r, s):
        sc_s[slot, 0:r, :] = s
        scb_s[slot, 0:r, :] = s.astype(BF16)

    mrow = lax.broadcasted_iota(I32, (MT, QB), 0)
    krow = lax.broadcasted_iota(I32, (KT, QB), 0)
    store_scores(0, MT, scores_for_tile(mik2_ref[0:MT, :], mrow < N_META))

    def fill_past(kt):
        store_scores(kt + 1, KT, scores_for_tile(ik2_ref[0, kt], None))

    def fill_pair(i, c):
        fill_past(2 * i)
        fill_past(2 * i + 1)
        return c

    @when_real
    def _():
        lax.fori_loop(0, n_past // 2, fill_pair, 0)

        @pl.when(n_past % 2 == 1)
        def _():
            fill_past(n_past - 1)

        store_scores(g, KT, scores_for_tile(ik2_ref[0, g - 1], (krow >> 6) <= (lane >> 6)))

    def count(arr_s, pred, pack):
        acc_t = jnp.int16 if pack == 16 else I32

        def cnt(tile):
            x = jnp.where(pred(tile), jnp.ones((), acc_t), jnp.zeros((), acc_t))
            x = x.reshape(tile.shape[0] // pack, pack, QB)
            parts = [x[j] for j in range(x.shape[0])]
            while len(parts) > 1:
                parts = [parts[j] + parts[j + 1] for j in range(0, len(parts), 2)]
            return parts[0]

        acc = cnt(arr_s[0, 0:MT, :])
        acc = lax.fori_loop(0, g, lambda kt, a: a + cnt(arr_s[kt + 1]), acc)
        return jnp.sum(acc.astype(I32), axis=0, keepdims=True)

    def ordered_to_bits(k, sign_bit, low_mask):
        return jnp.where(k >= sign_bit, k ^ sign_bit, k ^ low_mask)

    def coarse_body(i, u):
        cand_u = u | jnp.left_shift(jnp.int32(1), 15 - i)
        finite = (cand_u >= U_NEG_INF) & (cand_u <= U_POS_INF)
        bits = ordered_to_bits(jnp.clip(cand_u, U_NEG_INF, U_POS_INF), 0x8000, 0xFFFF)
        cand = lax.bitcast_convert_type(jnp.left_shift(bits, 16), F32).astype(BF16)
        cnt = count(scb_s, lambda tile: tile >= cand, 16)
        return jnp.where(finite & (cnt >= topk), cand_u, u)

    u1 = lax.fori_loop(0, 16, coarse_body, jnp.zeros((1, QB), I32))
    few = u1 == 0
    t1_bits = jnp.left_shift(ordered_to_bits(jnp.where(few, 0x8000, u1), 0x8000, 0xFFFF), 16)
    base = (t1_bits ^ ((t1_bits >> 31) & 0x7FFFFFFF)) - 0x8000

    def key_to_f32(k):
        return lax.bitcast_convert_type(k ^ ((k >> 31) & 0x7FFFFFFF), F32)

    def fine_body(i, o):
        cand_o = o | jnp.left_shift(jnp.int32(1), 16 - i)
        cand = key_to_f32(base + cand_o)
        cnt = count(sc_s, lambda tile: tile >= cand, 8)
        return jnp.where(cnt >= topk, cand_o, o)

    o2 = lax.fori_loop(0, 17, fine_body, jnp.zeros((1, QB), I32))
    thr = jnp.where(few, ninf, key_to_f32(base + o2))
    n_gt = count(sc_s, lambda tile: tile > thr, 8)
    need = jnp.where(few, 0, topk - n_gt).astype(F32)

    m_s[...] = jnp.full(m_s.shape, -jnp.inf, F32)
    acc_s[...] = jnp.zeros(acc_s.shape, F32)

    def offsets(subs):
        offs, row0 = [], 0
        for sub in subs:
            offs.append((row0, sub[0].shape[0]))
            row0 += sub[0].shape[0]
        return offs, row0

    def bias_stage(subs):
        offs, _ = offsets(subs)
        n_eq_after = None
        for (sc_t, _, _, _, n_eq_before), (o, r) in zip(subs, offs):
            if n_eq_before is None:
                n_eq_before = n_eq_after
            ri = lax.broadcasted_iota(I32, (r, r), 0)
            ci = lax.broadcasted_iota(I32, (r, r), 1)
            lower = jnp.where(ri > ci, 1.0, 0.0).astype(BF16)
            eq = sc_t == thr
            rank = _dot(lower, jnp.where(eq, 1.0, 0.0).astype(BF16)) + n_eq_before
            bias_s[o:o + r, :] = jnp.where(
                sc_t > thr, 0.0, jnp.where(eq, jnp.where(rank < need, 0.0, ninf), ninf))
            n_eq_after = rank[r - 1:r, :] + jnp.where(eq[r - 1:r, :], 1.0, 0.0)
        return n_eq_after

    def logits_head(subs, h):
        offs, _ = offsets(subs)
        m_old = m_s[h]
        m_new = m_old
        for (_, kx_fn, _, after, _), (o, r) in zip(subs, offs):
            lg = _dot_nt(kx_fn(h // 2), qm_s[h]) + bias_s[o:o + r, :]
            if after is not None:
                lg = lg + ALIBI_C[h] * after
            lg_s[h, o:o + r, :] = lg
            m_new = jnp.maximum(m_new, jnp.max(lg, axis=0, keepdims=True))
        m_s[h] = m_new
        m_safe = jnp.where(m_new == ninf, 0.0, m_new)
        shift_s[h] = m_safe
        corr_s[h] = jnp.exp2(m_old - m_safe)

    def softmax_pv_head(subs, h):
        offs, rows = offsets(subs)
        p_s[h, 0:rows, :] = jnp.exp2(lg_s[h, 0:rows, :] - shift_s[h]).astype(BF16)
        acc = acc_s[h] * corr_s[h]
        for (_, _, vT_fn, _, _), (o, r) in zip(subs, offs):
            acc = acc + _dot(vT_fn(h), p_s[h, o:o + r, :])
        acc_s[h] = acc

    def run_stages(done, nxt):
        n_eq_after = bias_stage(nxt) if nxt is not None else None
        for h in range(A_HEADS):
            if done is not None:
                softmax_pv_head(done, h)
            if nxt is not None:
                logits_head(nxt, h)
        return n_eq_after

    def after_term(kpos):
        return jnp.minimum(2 * (qpos - kpos), 0).astype(F32)

    def meta_sub():
        return (sc_s[0, 0:MT, :],
                lambda p: makx_ref[0:MT, 2 * p * LANES:(2 * p + 2) * LANES],
                lambda h: mavT_ref[0, h * VT_ROWS:(h + 1) * VT_ROWS, 0:MT],
                after_term(mrow), jnp.zeros((1, QB), F32))

    def real_sub(kt, after, n_eq_before):
        return (sc_s[kt + 1],
                lambda p: akx_ref[0, kt, :, 2 * p * LANES:(2 * p + 2) * LANES],
                lambda h: avT_ref[0, kt, h * VT_ROWS:(h + 1) * VT_ROWS, :],
                after, n_eq_before)

    neq_s[...] = jnp.sum(jnp.where(sc_s[0, 0:MT, :] == thr, 1.0, 0.0), axis=0, keepdims=True)

    def past(kt):
        return [real_sub(kt, None, neq_s[...])]

    def last_group():
        return [meta_sub(), real_sub(g - 1, after_term(N_META + (g - 1) * KT + krow), neq_s[...])]

    @when_real
    def _():
        @pl.when(n_past > 0)
        def _():
            neq_s[...] = run_stages(None, past(0))

        def pipe_body(i, c):
            neq_s[...] = run_stages(past(i), past(i + 1))
            return c

        lax.fori_loop(0, jnp.maximum(n_past - 1, 0), pipe_body, 0)

        @pl.when(n_past > 0)
        def _():
            run_stages(past(n_past - 1), last_group())

        @pl.when(n_past == 0)
        def _():
            run_stages(None, last_group())

        run_stages(last_group(), None)

    @when_meta
    def _():
        run_stages(None, [meta_sub()])
        run_stages([meta_sub()], None)

    outs = []
    for h in range(A_HEADS):
        o = acc_s[h, 0:A_HEAD_DIM, :] / acc_s[h, A_HEAD_DIM:A_HEAD_DIM + 1, :]
        ms = jnp.mean(o * o, axis=0, keepdims=True)
        outs.append(o * lax.rsqrt(ms + RMS_EPS))
    res = (jnp.concatenate(outs, axis=0).T * ng_ref[...]).astype(BF16)

    out_ref[0] = res


def _alibi_query_features():
    rows = []
    for c in ALIBI_C:
        rest = jnp.float32(c)
        lanes = []
        for _ in range(N_SLOPE_PARTS):
            part = rest.astype(BF16).astype(F32)
            lanes += [part * CHUNK, part]
            rest = rest - part
        rows.append(jnp.stack(lanes + [jnp.float32(0.0)] * (LANES - len(lanes))))
    return jnp.stack(rows)


def _dsa(aq, iq, miscT, akx, avT, ik2, m, ng, topk, is_meta):
    nb, s, _ = aq.shape
    nq = s // QB
    nkt = s // KT
    vt = A_HEADS * VT_ROWS
    qidx = lambda b, i: (b, i, 0)
    const = lambda shp: pl.BlockSpec(shp, lambda b, i: (0,) * len(shp))
    in_specs = [
        pl.BlockSpec((1, QB, 512), qidx),
        pl.BlockSpec((1, QB, 256), qidx),
        pl.BlockSpec((1, 128, QB), lambda b, i: (b, 0, i)),
        pl.BlockSpec((1, nkt, KT, KX_COLS), lambda b, i: (b, 0, 0, 0)),
        pl.BlockSpec((1, nkt, vt, KT), lambda b, i: (b, 0, 0, 0)),
        pl.BlockSpec((1, nkt, KT, 128), lambda b, i: (b, 0, 0, 0)),
        const((META_ROWS, 512)), const((META_ROWS, 256)), const((128, META_ROWS)),
        const((META_ROWS, KX_COLS)), const((1, vt, KT)), const((META_ROWS, 128)),
        const((A_HEADS, LANES)), const((1, 512)),
    ]
    out_shape = jax.ShapeDtypeStruct((nb, s, 512), BF16)
    out_specs = pl.BlockSpec((1, QB, 512), qidx)
    scratch = [
        pltpu.VMEM((nkt + 1, KT, QB), F32),
        pltpu.VMEM((nkt + 1, KT, QB), BF16),
        pltpu.VMEM((A_HEADS, QB, 2 * LANES), BF16),
        pltpu.VMEM((IDX_HEADS, QB, LANES), BF16),
        pltpu.VMEM((8, QB), F32),
        pltpu.VMEM((A_HEADS, 1, QB), F32),
        pltpu.VMEM((A_HEADS, VT_ROWS, QB), F32),
        pltpu.VMEM((GROUP_ROWS, QB), F32),
        pltpu.VMEM((A_HEADS, GROUP_ROWS, QB), F32),
        pltpu.VMEM((A_HEADS, GROUP_ROWS, QB), BF16),
        pltpu.VMEM((1, QB), F32),
        pltpu.VMEM((A_HEADS, 1, QB), F32),
        pltpu.VMEM((A_HEADS, 1, QB), F32),
    ]
    return pl.pallas_call(
        functools.partial(_dsa_kernel, topk=topk, is_meta=is_meta),
        grid=(nb, nq),
        in_specs=in_specs,
        out_specs=out_specs,
        out_shape=out_shape,
        scratch_shapes=scratch,
        compiler_params=pltpu.CompilerParams(
            dimension_semantics=("arbitrary", "arbitrary"), vmem_limit_bytes=VMEM_LIMIT),
        name="dsa",
    )(aq, iq, miscT, akx.reshape(nb, nkt, KT, KX_COLS), avT, ik2.reshape(nb, nkt, KT, 128),
      m["aq"], m["iq"], m["miscT"], m["akx"], m["avT"], m["ik2"], _alibi_query_features(), ng)


def _chunk_cumsum_rows(x):
    row_in_chunk = lax.broadcasted_iota(I32, x.shape, 0) & (CHUNK - 1)
    sh = 1
    while sh < CHUNK:
        x = x + jnp.where(row_in_chunk >= sh, pltpu.roll(x, sh, 0), 0.0)
        sh *= 2
    return x


def _rows_of_chunk(rows_per_chunk):
    return jnp.concatenate([jnp.broadcast_to(r, (CHUNK, r.shape[1])) for r in rows_per_chunk], axis=0)


def _gla_block(q, k, v, lr, og, wg, bg, ng, st_s, valid_rows):
    nrow = q.shape[0]
    nch = nrow // CHUNK
    z = _dot(lr.astype(BF16), wg) + bg
    logg = (jnp.minimum(z, 0.0) - jnp.log(1.0 + jnp.exp(-jnp.abs(z)))) / B_GATE_TAU
    if valid_rows is not None:
        rows = lax.broadcasted_iota(I32, logg.shape, 0)
        logg = jnp.where(rows < valid_rows, logg, 0.0)
    b = _chunk_cumsum_rows(logg)
    b_last_rows = [b[(c + 1) * CHUNK - 1:(c + 1) * CHUNK, :] for c in range(nch)]
    b_mid = _rows_of_chunk([b[c * CHUNK + CHUNK // 2 - 1:c * CHUNK + CHUNK // 2, :] for c in range(nch)])
    b_last = _rows_of_chunk(b_last_rows)
    q_in = q * jnp.exp(b)
    q_ic = q * jnp.exp(b - b_mid)
    k_ic = k * jnp.exp(b_mid - b)
    k_st = k * jnp.exp(b_last - b)
    decay = [jnp.exp(r) for r in b_last_rows]

    lane_half = lax.broadcasted_iota(I32, (1, LANES), 1) // B_KEY_DIM
    ri = lax.broadcasted_iota(I32, (nrow, nrow), 0)
    ci = lax.broadcasted_iota(I32, (nrow, nrow), 1)
    causal = ((ri >> 6) == (ci >> 6)) & (ci <= ri)
    outs = []
    for p in range(B_HEADS // 2):
        sl = slice(p * LANES, (p + 1) * LANES)
        heads = (2 * p, 2 * p + 1)
        v_h = [v[:, h * B_VAL_DIM:(h + 1) * B_VAL_DIM] for h in heads]
        k_st_h = [jnp.where(lane_half == hh, k_st[:, sl], 0.0).astype(BF16) for hh in range(2)]
        states = [st_s[p]]
        for c in range(nch):
            rs = slice(c * CHUNK, (c + 1) * CHUNK)
            upd = _dot_tn(v_h[0][rs], k_st_h[0][rs]) + _dot_tn(v_h[1][rs], k_st_h[1][rs])
            states.append(states[c] * decay[c][:, sl] + upd)
        st_s[p] = states[nch]
        k_ic_p = k_ic[:, sl].astype(BF16)
        for hh, h in enumerate(heads):
            hm = lane_half == hh
            a = _dot_nt(jnp.where(hm, q_ic[:, sl], 0.0).astype(BF16), k_ic_p)
            o = _dot(jnp.where(causal, a, 0.0).astype(BF16), v_h[hh])
            q_in_h = jnp.where(hm, q_in[:, sl], 0.0).astype(BF16)
            o = o + jnp.concatenate(
                [_dot_nt(q_in_h[c * CHUNK:(c + 1) * CHUNK], states[c].astype(BF16)) for c in range(nch)], axis=0)
            ms = jnp.mean(o * o, axis=-1, keepdims=True)
            on = o * lax.rsqrt(ms + RMS_EPS) * ng[:, h * B_VAL_DIM:(h + 1) * B_VAL_DIM]
            og_h = og[:, h * B_VAL_DIM:(h + 1) * B_VAL_DIM]
            outs.append(on * (og_h * jax.nn.sigmoid(og_h)))
    return outs


def _gla_kernel(bq_ref, bk_ref, bv_ref, misc_ref, bog_ref,
                mbq_ref, mbk_ref, mbv_ref, mmisc_ref, mbog_ref,
                wg_ref, bg_ref, ng_ref, out_ref, mout_ref, st_s):
    j = pl.program_id(1)
    wg = wg_ref[...]
    bg = bg_ref[...]
    ng = ng_ref[...]

    @pl.when(j == 0)
    def _():
        st_s[...] = jnp.zeros(st_s.shape, F32)
        outs = _gla_block(mbq_ref[0:CHUNK, :].astype(F32), mbk_ref[0:CHUNK, :].astype(F32),
                          mbv_ref[0:CHUNK, :], mmisc_ref[0:CHUNK, MISC_LR:MISC_LR + B_GATE_RANK],
                          mbog_ref[0:CHUNK, :], wg, bg, ng, st_s, N_META)
        mout_ref[CHUNK:, :] = jnp.zeros((META_ROWS - CHUNK, B_WIDTH), BF16)
        for h, o in enumerate(outs):
            mout_ref[0:CHUNK, h * B_VAL_DIM:(h + 1) * B_VAL_DIM] = o.astype(BF16)

    outs = _gla_block(bq_ref[0].astype(F32), bk_ref[0].astype(F32), bv_ref[0],
                      misc_ref[0, :, MISC_LR:MISC_LR + B_GATE_RANK], bog_ref[0], wg, bg, ng, st_s, None)
    for h, o in enumerate(outs):
        out_ref[0, :, h * B_VAL_DIM:(h + 1) * B_VAL_DIM] = o.astype(BF16)


def _gla(bq, bk, bv, misc, bog, m, wg, bg, ng):
    nb, s, _ = bq.shape
    row = lambda c: pl.BlockSpec((1, GLA_BLOCK, c), lambda b, j: (b, j, 0))
    const = lambda shp: pl.BlockSpec(shp, lambda b, j: (0,) * len(shp))
    return pl.pallas_call(
        _gla_kernel,
        grid=(nb, s // GLA_BLOCK),
        in_specs=[row(256), row(256), row(512), row(128), row(512),
                  const((META_ROWS, 256)), const((META_ROWS, 256)), const((META_ROWS, 512)),
                  const((META_ROWS, 128)), const((META_ROWS, 512)),
                  const((B_GATE_RANK, 256)), const((1, 256)), const((1, 512))],
        out_specs=(row(512), const((META_ROWS, 512))),
        out_shape=(jax.ShapeDtypeStruct((nb, s, 512), BF16), jax.ShapeDtypeStruct((META_ROWS, 512), BF16)),
        scratch_shapes=[pltpu.VMEM((B_HEADS // 2, LANES, LANES), F32)],
        compiler_params=pltpu.CompilerParams(
            dimension_semantics=("arbitrary", "arbitrary"), vmem_limit_bytes=VMEM_LIMIT),
        name="gla",
    )(bq, bk, bv, misc, bog, m["bq"], m["bk"], m["bv"], m["misc"], m["bog"], wg, bg, ng)


def _outproj_kernel(x_ref, a_ref, b_ref, lng_ref, lnb_ref, wo_ref, g1_ref, b1_ref, out_ref):
    h = _layer_norm(x_ref[0], lng_ref[...], lnb_ref[...])
    mixed = _dot(a_ref[0], wo_ref[0:A_WIDTH, :]) + _dot(b_ref[0], wo_ref[A_WIDTH:, :])
    out_ref[0] = _layer_norm(ALPHA * h + mixed, g1_ref[...], b1_ref[...])


def _outproj(x, a, b, lng, lnb, wo, g1, b1, tm):
    nb, s, d = x.shape
    row = lambda c: pl.BlockSpec((1, tm, c), lambda bb, i: (bb, i, 0))
    const = lambda shp: pl.BlockSpec(shp, lambda bb, i: (0,) * len(shp))
    return pl.pallas_call(
        _outproj_kernel,
        grid=(nb, s // tm),
        in_specs=[row(d), row(512), row(512), const((1, d)), const((1, d)), const((d, d)),
                  const((1, d)), const((1, d))],
        out_specs=row(d),
        out_shape=jax.ShapeDtypeStruct((nb, s, d), F32),
        compiler_params=pltpu.CompilerParams(
            dimension_semantics=("arbitrary", "arbitrary"), vmem_limit_bytes=VMEM_LIMIT),
        name="outproj",
    )(x, a, b, lng, lnb, wo, g1, b1)


def _gelu_tanh(x):
    c = 0.7978845608028654
    return 0.5 * x * (1.0 + jnp.tanh(c * (x + 0.044715 * (x * x * x))))


def _ffn_kernel(x_ref, xh_ref, a_ref, ah_ref, b_ref, bh_ref, hm_ref, lng_ref, lnb_ref, wo_ref, g1_ref, b1_ref,
                wup_ref, cw_ref, cb_ref, wdn_ref, g2_ref, b2_ref, out_ref, y_s):
    j = pl.program_id(1)
    def mix(x, a_mix, b_mix):
        mixed = _dot(a_mix, wo_ref[0:A_WIDTH, :]) + _dot(b_mix, wo_ref[A_WIDTH:, :])
        return _layer_norm(ALPHA * _layer_norm(x, lng_ref[...], lnb_ref[...]) + mixed, g1_ref[...], b1_ref[...])

    half = x_ref.shape[1] // 2
    h1_lo = mix(jnp.concatenate([xh_ref[0], x_ref[0, 0:half, :]], axis=0),
                jnp.concatenate([ah_ref[0], a_ref[0, 0:half, :]], axis=0),
                jnp.concatenate([bh_ref[0], b_ref[0, 0:half, :]], axis=0))
    h1_hi = mix(x_ref[0, half:, :], a_ref[0, half:, :], b_ref[0, half:, :])
    h = jnp.concatenate([h1_lo[HALO:, :], h1_hi], axis=0)
    halo = jnp.where(j == 0, hm_ref[...], h1_lo[0:HALO, :])
    hb = jnp.concatenate([halo, h], axis=0).astype(BF16)
    for f in range(D_FF // FF_TILE):
        fs = slice(f * FF_TILE, (f + 1) * FF_TILE)
        a = _dot(hb, wup_ref[:, fs])
        gate = _dot(hb[HALO:, :], wup_ref[:, D_FF + f * FF_TILE:D_FF + (f + 1) * FF_TILE])
        cw = cw_ref[:, fs]
        conv = cb_ref[:, fs] + cw[CONV_W - 1:CONV_W, :] * a[HALO:, :]
        for back in range(1, CONV_W):
            tap = cw[CONV_W - 1 - back:CONV_W - back, :]
            conv = conv + tap * pltpu.roll(a, back, 0)[HALO:, :]
        y_s[:, fs] = (_gelu_tanh(conv) * gate).astype(BF16)
    ffn = _dot(y_s[...], wdn_ref[...])
    out_ref[0] = _layer_norm(ALPHA * h + ffn, g2_ref[...], b2_ref[...])


def _ffn(x, a, b, h1_meta, lng, lnb, wo, g1, b1, wup, cw, cb, wdn, g2, b2, tm):
    nb, s, d = x.shape
    per = tm // HALO
    const = lambda shp: pl.BlockSpec(shp, lambda bb, j: (0,) * len(shp), pipeline_mode=pl.Buffered(1))
    main = lambda c: pl.BlockSpec((1, tm, c), lambda bb, j: (bb, j, 0))
    halo = lambda c: pl.BlockSpec((1, HALO, c), lambda bb, j: (bb, jnp.maximum(j * per - 1, 0), 0))
    return pl.pallas_call(
        _ffn_kernel,
        grid=(nb, s // tm),
        in_specs=[main(d), halo(d), main(A_WIDTH), halo(A_WIDTH), main(B_WIDTH), halo(B_WIDTH),
                  const((HALO, d)), const((1, d)), const((1, d)), const((d, d)), const((1, d)), const((1, d)),
                  const((d, 2 * D_FF)), const((CONV_W, D_FF)), const((1, D_FF)),
                  const((D_FF, d)), const((1, d)), const((1, d))],
        out_specs=main(d),
        out_shape=jax.ShapeDtypeStruct((nb, s, d), F32),
        scratch_shapes=[pltpu.VMEM((tm, D_FF), BF16)],
        compiler_params=pltpu.CompilerParams(
            dimension_semantics=("arbitrary", "arbitrary"), vmem_limit_bytes=VMEM_LIMIT),
        name="ffn",
    )(x, x, a, a, b, b, h1_meta, lng, lnb, wo, g1, b1, wup, cw, cb, wdn, g2, b2)


def _permute_w_in(w):
    o = 0
    parts = {}
    for name, n in (("a_q", 512), ("a_k", 512), ("a_v", 512), ("i_q", 256), ("i_k", 64), ("i_w", 4),
                    ("b_q", 256), ("b_k", 256), ("b_v", 512), ("b_lr", 16), ("b_og", 512)):
        parts[name] = w[:, o:o + n]
        o += n
    zeros = jnp.zeros((w.shape[0], MISC_IK - B_GATE_RANK - IDX_HEADS), w.dtype)
    cols = [parts["a_q"] * (A_HEAD_DIM ** -0.5 * LOG2E), parts["a_k"], parts["a_v"], parts["i_q"],
            parts["b_q"] * (B_KEY_DIM ** -0.5), parts["b_k"], parts["b_v"], parts["b_og"],
            parts["b_lr"], parts["i_w"] * (IDX_HEADS ** -0.5) * (IDX_DIM ** -0.5), zeros, parts["i_k"]]
    return jnp.concatenate(cols, axis=1).astype(BF16)


def kernel(x, meta, ln_in_g, ln_in_b, w_in, w_gate_b, b_gate_b, attn_norm_g, gla_norm_g, w_out,
           ln1_g, ln1_b, w_up, conv_w, conv_b, w_down, ln2_g, ln2_b):
    nb, s, d = x.shape
    assert d == D_MODEL and w_in.shape[0] == DEPTH == 1
    assert s % ROW_TILE == 0 and s % KT == 0 and s % GLA_BLOCK == 0
    topk = min(TOPK_MAX, s // 4)
    l = 0
    r2 = lambda v: v.reshape(1, -1)
    lng, lnb = r2(ln_in_g), r2(ln_in_b)
    w_perm = _permute_w_in(w_in[l])
    wo = w_out[l].astype(BF16)
    wup = w_up[l].astype(BF16)
    wdn = w_down[l].astype(BF16)
    wg = w_gate_b[l].astype(BF16)
    bg = r2(b_gate_b[l])
    ng_a = r2(attn_norm_g[l])
    ng_b = r2(gla_norm_g[l])

    x_meta = jnp.zeros((1, META_ROWS, d), x.dtype).at[0, :N_META].set(meta.astype(x.dtype))
    names = ("aq", "akx", "avT", "iq", "bq", "bk", "bv", "bog", "ik2", "misc", "miscT")
    m = {n: v[0] for n, v in zip(names, _inproj(x_meta, lng, lnb, w_perm, META_ROWS, 0, valid_rows=N_META))}

    aq, akx, avT, iq, bq, bk, bv, bog, ik2, misc, miscT = _inproj(x, lng, lnb, w_perm, ROW_TILE, N_META)
    a_out = _dsa(aq, iq, miscT, akx, avT, ik2, m, ng_a, topk, False)
    a_out_m = _dsa(m["aq"][None], m["iq"][None], m["miscT"][None], m["akx"][None], m["avT"][None],
                   m["ik2"][None], m, ng_a, topk, True)[0]
    b_out, b_out_m = _gla(bq, bk, bv, misc, bog, m, wg, bg, ng_b)

    g1, b1 = r2(ln1_g[l]), r2(ln1_b[l])
    h1_m = _outproj(x_meta, a_out_m[None], b_out_m[None], lng, lnb, wo, g1, b1, META_ROWS)
    return _ffn(x, a_out, b_out, h1_m[0, N_META - HALO:N_META], lng, lnb, wo, g1, b1,
                wup, conv_w[l], r2(conv_b[l]), wdn, r2(ln2_g[l]), r2(ln2_b[l]), ROW_TILE)
```

```python
import functools

import jax
import jax.numpy as jnp
from jax import lax
from jax.experimental import pallas as pl
from jax.experimental.pallas import tpu as pltpu

F32 = jnp.float32
BF16 = jnp.bfloat16
I32 = jnp.int32

D_MODEL = 1024
CHUNK = 64
N_META = 16
A_HEADS = 8
A_WIDTH = 512
A_HEAD_DIM = 64
IDX_HEADS = 4
IDX_DIM = 64
TOPK_MAX = 256
B_HEADS = 4
B_WIDTH = 512
B_VAL_DIM = 128
B_KEY_DIM = 64
B_GATE_RANK = 16
B_GATE_TAU = 16.0
D_FF = 2816
CONV_W = 3
LN_EPS = 1e-5
RMS_EPS = 1e-6
DEPTH = 1
ALPHA = (2.0 * DEPTH) ** 0.25

LANES = 128
ROW_TILE = 512
QB = 256
KT = 256
MT = 64
GROUP_ROWS = MT + KT
META_ROWS = 256
GLA_BLOCK = 256
FF_TILE = 256
HALO = 16
U_NEG_INF, U_POS_INF = 0x007F, 0xFF80
N_SLOPE_PARTS = 3
VT_ROWS = 80
KX_COLS = 2 * A_WIDTH
LOG2E = 1.4426950408889634
ALIBI_C = tuple(2.0 ** (-8.0 * (h + 1) / A_HEADS) * LOG2E for h in range(A_HEADS))
VMEM_LIMIT = 56 * 1024 * 1024

_COLS = {}
_off = 0
for _name, _w in (("aq", 512), ("ak", 512), ("av", 512), ("iq", 256), ("bq", 256), ("bk", 256),
                  ("bv", 512), ("bog", 512), ("misc", 128)):
    _COLS[_name] = (_off, _w)
    _off += _w
PROJ_COLS = _off
MISC_LR = 0
MISC_IW = 16
MISC_IK = 64


def _layer_norm(x, g, b):
    mu = jnp.mean(x, axis=-1, keepdims=True)
    xc = x - mu
    var = jnp.mean(xc * xc, axis=-1, keepdims=True)
    return xc * lax.rsqrt(var + LN_EPS) * g + b


def _dot(a, b):
    return jnp.dot(a, b, preferred_element_type=F32)


def _dot_nt(a, b):
    return lax.dot_general(a, b, (((1,), (1,)), ((), ())), preferred_element_type=F32)


def _dot_tn(a, b):
    return lax.dot_general(a, b, (((0,), (0,)), ((), ())), preferred_element_type=F32)


def _inproj_kernel(x_ref, g_ref, b_ref, w_ref, aq_ref, akx_ref, avT_ref, iq_ref, bq_ref, bk_ref,
                   bv_ref, bog_ref, ik2_ref, misc_ref, miscT_ref, *, pos0, valid_rows):
    h = _layer_norm(x_ref[0], g_ref[...], b_ref[...])
    if valid_rows is not None:
        h = jnp.where(lax.broadcasted_iota(I32, h.shape, 0) < valid_rows, h, 0.0)
    hb = h.astype(BF16)
    tm = hb.shape[0]

    def proj(name):
        lo, w = _COLS[name]
        return _dot(hb, w_ref[:, lo:lo + w])

    aq_ref[0] = proj("aq").astype(BF16)
    ak = proj("ak")
    rows = lax.broadcasted_iota(I32, (tm, LANES), 0)
    lanes = lax.broadcasted_iota(I32, (tm, LANES), 1)
    pos = pos0 + pl.program_id(1) * tm + rows
    feat = jnp.where(lanes < 2 * N_SLOPE_PARTS, jnp.where(lanes % 2 == 0, pos >> 6, pos & 63), 0)
    feat = feat.astype(F32).astype(BF16)
    for p in range(A_HEADS // 2):
        akx_ref[0, :, 2 * p * LANES:(2 * p + 1) * LANES] = ak[:, p * LANES:(p + 1) * LANES].astype(BF16)
        akx_ref[0, :, (2 * p + 1) * LANES:(2 * p + 2) * LANES] = feat
    avT = proj("av").T
    ones = jnp.ones((VT_ROWS - A_HEAD_DIM, KT), BF16)
    for i in range(avT_ref.shape[1]):
        for hd in range(A_HEADS):
            avT_ref[0, i, hd * VT_ROWS:hd * VT_ROWS + A_HEAD_DIM, :] = (
                avT[hd * A_HEAD_DIM:(hd + 1) * A_HEAD_DIM, i * KT:(i + 1) * KT].astype(BF16))
            avT_ref[0, i, hd * VT_ROWS + A_HEAD_DIM:(hd + 1) * VT_ROWS, :] = ones
    iq_ref[0] = proj("iq").astype(BF16)
    bq_ref[0] = proj("bq").astype(BF16)
    bk_ref[0] = proj("bk").astype(BF16)
    bv_ref[0] = proj("bv").astype(BF16)
    bog_ref[0] = proj("bog")
    misc = proj("misc")
    misc_ref[0] = misc
    miscT_ref[0] = misc.T
    ik_lo = pltpu.roll(misc, LANES - MISC_IK, 1)
    ik2_ref[0] = jnp.where(lanes < MISC_IK, ik_lo, misc).astype(BF16)


def _inproj(x, ln_g, ln_b, w_perm, tm, pos0, valid_rows=None):
    nb, s, d = x.shape
    nt = s // tm
    row = lambda c: pl.BlockSpec((1, tm, c), lambda b, i: (b, i, 0))
    const = lambda shp: pl.BlockSpec(shp, lambda b, i: (0,) * len(shp))
    out_shape = (
        jax.ShapeDtypeStruct((nb, s, 512), BF16),
        jax.ShapeDtypeStruct((nb, s, KX_COLS), BF16),
        jax.ShapeDtypeStruct((nb, s // KT, A_HEADS * VT_ROWS, KT), BF16),
        jax.ShapeDtypeStruct((nb, s, 256), BF16),
        jax.ShapeDtypeStruct((nb, s, 256), BF16),
        jax.ShapeDtypeStruct((nb, s, 256), BF16),
        jax.ShapeDtypeStruct((nb, s, 512), BF16),
        jax.ShapeDtypeStruct((nb, s, 512), F32),
        jax.ShapeDtypeStruct((nb, s, 128), BF16),
        jax.ShapeDtypeStruct((nb, s, 128), F32),
        jax.ShapeDtypeStruct((nb, 128, s), F32),
    )
    out_specs = (
        row(512), row(KX_COLS),
        pl.BlockSpec((1, tm // KT, A_HEADS * VT_ROWS, KT), lambda b, i: (b, i, 0, 0)),
        row(256), row(256), row(256), row(512), row(512), row(128), row(128),
        pl.BlockSpec((1, 128, tm), lambda b, i: (b, 0, i)),
    )
    return pl.pallas_call(
        functools.partial(_inproj_kernel, pos0=pos0, valid_rows=valid_rows),
        grid=(nb, nt),
        in_specs=[row(d), const((1, d)), const((1, d)), const((d, PROJ_COLS))],
        out_specs=out_specs,
        out_shape=out_shape,
        compiler_params=pltpu.CompilerParams(
            dimension_semantics=("arbitrary", "arbitrary"), vmem_limit_bytes=VMEM_LIMIT),
        name="inproj",
    )(x, ln_g, ln_b, w_perm)


def _static_when(cond: bool):
    def deco(fn):
        if cond:
            fn()
    return deco


def _dsa_kernel(aq_ref, iq_ref, miscT_ref, akx_ref, avT_ref, ik2_ref,
                maq_ref, miq_ref, mmiscT_ref, makx_ref, mavT_ref, mik2_ref, qfeat_ref, ng_ref,
                out_ref,
                sc_s, scb_s, qm_s, iqm_s, iw_s, m_s, acc_s, bias_s, lg_s, p_s, neq_s, shift_s, corr_s,
                *, topk, is_meta):
    g = 0 if is_meta else pl.program_id(1) + 1
    n_past = 0 if is_meta else g - 1
    when_real = _static_when(not is_meta)
    when_meta = _static_when(is_meta)

    lane_half = lax.broadcasted_iota(I32, (1, LANES), 1) // A_HEAD_DIM

    def stage(aq, iq, miscT):
        for h in range(A_HEADS):
            p = h // 2
            qp = aq[:, p * LANES:(p + 1) * LANES]
            qm_s[h, :, 0:LANES] = jnp.where(lane_half == (h % 2), qp, jnp.zeros_like(qp))
            qm_s[h, :, LANES:2 * LANES] = jnp.broadcast_to(qfeat_ref[h:h + 1, :], (QB, LANES)).astype(BF16)
        for h in range(IDX_HEADS):
            p = h // 2
            qp = iq[:, p * LANES:(p + 1) * LANES]
            iqm_s[h] = jnp.where(lane_half == (h % 2), qp, jnp.zeros_like(qp))
        iw_s[...] = miscT[MISC_IW:MISC_IW + 8, :]

    lane = lax.broadcasted_iota(I32, (1, QB), 1)
    if is_meta:
        stage(maq_ref[...], miq_ref[...], mmiscT_ref[...])
        qpos = lane
    else:
        stage(aq_ref[0], iq_ref[0], miscT_ref[0])
        qpos = N_META + (g - 1) * QB + lane

    ninf = jnp.float32(-jnp.inf)

    def scores_for_tile(ik2_t, allowed):
        s = None
        for h in range(IDX_HEADS):
            sh = _dot_nt(ik2_t, iqm_s[h])
            term = iw_s[h:h + 1, :] * jnp.maximum(sh, 0.0)
            s = term if s is None else s + term
        return s if allowed is None else jnp.where(allowed, s, ninf)

    def store_scores(slot, r, s):
        sc_s[slot, 0:r, :] = s
        scb_s[slot, 0:r, :] = s.astype(BF16)

    mrow = lax.broadcasted_iota(I32, (MT, QB), 0)
    krow = lax.broadcasted_iota(I32, (KT, QB), 0)
    store_scores(0, MT, scores_for_tile(mik2_ref[0:MT, :], mrow < N_META))

    def fill_past(kt):
        store_scores(kt + 1, KT, scores_for_tile(ik2_ref[0, kt], None))

    def fill_pair(i, c):
        fill_past(2 * i)
        fill_past(2 * i + 1)
        return c

    @when_real
    def _():
        lax.fori_loop(0, n_past // 2, fill_pair, 0)

        @pl.when(n_past % 2 == 1)
        def _():
            fill_past(n_past - 1)

        store_scores(g, KT, scores_for_tile(ik2_ref[0, g - 1], (krow >> 6) <= (lane >> 6)))

    def count(arr_s, pred, pack):
        acc_t = jnp.int16 if pack == 16 else I32

        def cnt(tile):
            x = jnp.where(pred(tile), jnp.ones((), acc_t), jnp.zeros((), acc_t))
            x = x.reshape(tile.shape[0] // pack, pack, QB)
            parts = [x[j] for j in range(x.shape[0])]
            while len(parts) > 1:
                parts = [parts[j] + parts[j + 1] for j in range(0, len(parts), 2)]
            return parts[0]

        acc = cnt(arr_s[0, 0:MT, :])
        acc = lax.fori_loop(0, g, lambda kt, a: a + cnt(arr_s[kt + 1]), acc)
        return jnp.sum(acc.astype(I32), axis=0, keepdims=True)

    def ordered_to_bits(k, sign_bit, low_mask):
        return jnp.where(k >= sign_bit, k ^ sign_bit, k ^ low_mask)

    def coarse_body(i, u):
        cand_u = u | jnp.left_shift(jnp.int32(1), 15 - i)
        finite = (cand_u >= U_NEG_INF) & (cand_u <= U_POS_INF)
        bits = ordered_to_bits(jnp.clip(cand_u, U_NEG_INF, U_POS_INF), 0x8000, 0xFFFF)
        cand = lax.bitcast_convert_type(jnp.left_shift(bits, 16), F32).astype(BF16)
        cnt = count(scb_s, lambda tile: tile >= cand, 16)
        return jnp.where(finite & (cnt >= topk), cand_u, u)

    u1 = lax.fori_loop(0, 16, coarse_body, jnp.zeros((1, QB), I32))
    few = u1 == 0
    t1_bits = jnp.left_shift(ordered_to_bits(jnp.where(few, 0x8000, u1), 0x8000, 0xFFFF), 16)
    base = (t1_bits ^ ((t1_bits >> 31) & 0x7FFFFFFF)) - 0x8000

    def key_to_f32(k):
        return lax.bitcast_convert_type(k ^ ((k >> 31) & 0x7FFFFFFF), F32)

    def fine_body(i, o):
        cand_o = o | jnp.left_shift(jnp.int32(1), 16 - i)
        cand = key_to_f32(base + cand_o)
        cnt = count(sc_s, lambda tile: tile >= cand, 8)
        return jnp.where(cnt >= topk, cand_o, o)

    o2 = lax.fori_loop(0, 17, fine_body, jnp.zeros((1, QB), I32))
    thr = jnp.where(few, ninf, key_to_f32(base + o2))
    n_gt = count(sc_s, lambda tile: tile > thr, 8)
    need = jnp.where(few, 0, topk - n_gt).astype(F32)

    m_s[...] = jnp.full(m_s.shape, -jnp.inf, F32)
    acc_s[...] = jnp.zeros(acc_s.shape, F32)

    def offsets(subs):
        offs, row0 = [], 0
        for sub in subs:
            offs.append((row0, sub[0].shape[0]))
            row0 += sub[0].shape[0]
        return offs, row0

    def bias_stage(subs):
        offs, _ = offsets(subs)
        n_eq_after = None
        for (sc_t, _, _, _, n_eq_before), (o, r) in zip(subs, offs):
            if n_eq_before is None:
                n_eq_before = n_eq_after
            ri = lax.broadcasted_iota(I32, (r, r), 0)
            ci = lax.broadcasted_iota(I32, (r, r), 1)
            lower = jnp.where(ri > ci, 1.0, 0.0).astype(BF16)
            eq = sc_t == thr
            rank = _dot(lower, jnp.where(eq, 1.0, 0.0).astype(BF16)) + n_eq_before
            bias_s[o:o + r, :] = jnp.where(
                sc_t > thr, 0.0, jnp.where(eq, jnp.where(rank < need, 0.0, ninf), ninf))
            n_eq_after = rank[r - 1:r, :] + jnp.where(eq[r - 1:r, :], 1.0, 0.0)
        return n_eq_after

    def logits_head(subs, h):
        offs, _ = offsets(subs)
        m_old = m_s[h]
        m_new = m_old
        for (_, kx_fn, _, after, _), (o, r) in zip(subs, offs):
            lg = _dot_nt(kx_fn(h // 2), qm_s[h]) + bias_s[o:o + r, :]
            if after is not None:
                lg = lg + ALIBI_C[h] * after
            lg_s[h, o:o + r, :] = lg
            m_new = jnp.maximum(m_new, jnp.max(lg, axis=0, keepdims=True))
        m_s[h] = m_new
        m_safe = jnp.where(m_new == ninf, 0.0, m_new)
        shift_s[h] = m_safe
        corr_s[h] = jnp.exp2(m_old - m_safe)

    def softmax_pv_head(subs, h):
        offs, rows = offsets(subs)
        p_s[h, 0:rows, :] = jnp.exp2(lg_s[h, 0:rows, :] - shift_s[h]).astype(BF16)
        acc = acc_s[h] * corr_s[h]
        for (_, _, vT_fn, _, _), (o, r) in zip(subs, offs):
            acc = acc + _dot(vT_fn(h), p_s[h, o:o + r, :])
        acc_s[h] = acc

    def run_stages(done, nxt):
        n_eq_after = bias_stage(nxt) if nxt is not None else None
        for h in range(A_HEADS):
            if done is not None:
                softmax_pv_head(done, h)
            if nxt is not None:
                logits_head(nxt, h)
        return n_eq_after

    def after_term(kpos):
        return jnp.minimum(2 * (qpos - kpos), 0).astype(F32)

    def meta_sub():
        return (sc_s[0, 0:MT, :],
                lambda p: makx_ref[0:MT, 2 * p * LANES:(2 * p + 2) * LANES],
                lambda h: mavT_ref[0, h * VT_ROWS:(h + 1) * VT_ROWS, 0:MT],
                after_term(mrow), jnp.zeros((1, QB), F32))

    def real_sub(kt, after, n_eq_before):
        return (sc_s[kt + 1],
                lambda p: akx_ref[0, kt, :, 2 * p * LANES:(2 * p + 2) * LANES],
                lambda h: avT_ref[0, kt, h * VT_ROWS:(h + 1) * VT_ROWS, :],
                after, n_eq_before)

    neq_s[...] = jnp.sum(jnp.where(sc_s[0, 0:MT, :] == thr, 1.0, 0.0), axis=0, keepdims=True)

    def past(kt):
        return [real_sub(kt, None, neq_s[...])]

    def last_group():
        return [meta_sub(), real_sub(g - 1, after_term(N_META + (g - 1) * KT + krow), neq_s[...])]

    @when_real
    def _():
        @pl.when(n_past > 0)
        def _():
            neq_s[...] = run_stages(None, past(0))

        def pipe_body(i, c):
            neq_s[...] = run_stages(past(i), past(i + 1))
            return c

        lax.fori_loop(0, jnp.maximum(n_past - 1, 0), pipe_body, 0)

        @pl.when(n_past > 0)
        def _():
            run_stages(past(n_past - 1), last_group())

        @pl.when(n_past == 0)
        def _():
            run_stages(None, last_group())

        run_stages(last_group(), None)

    @when_meta
    def _():
        run_stages(None, [meta_sub()])
        run_stages([meta_sub()], None)

    outs = []
    for h in range(A_HEADS):
        o = acc_s[h, 0:A_HEAD_DIM, :] / acc_s[h, A_HEAD_DIM:A_HEAD_DIM + 1, :]
        ms = jnp.mean(o * o, axis=0, keepdims=True)
        outs.append(o * lax.rsqrt(ms + RMS_EPS))
    res = (jnp.concatenate(outs, axis=0).T * ng_ref[...]).astype(BF16)

    out_ref[0] = res


def _alibi_query_features():
    rows = []
    for c in ALIBI_C:
        rest = jnp.float32(c)
        lanes = []
        for _ in range(N_SLOPE_PARTS):
            part = rest.astype(BF16).astype(F32)
            lanes += [part * CHUNK, part]
            rest = rest - part
        rows.append(jnp.stack(lanes + [jnp.float32(0.0)] * (LANES - len(lanes))))
    return jnp.stack(rows)


def _dsa(aq, iq, miscT, akx, avT, ik2, m, ng, topk, is_meta):
    nb, s, _ = aq.shape
    nq = s // QB
    nkt = s // KT
    vt = A_HEADS * VT_ROWS
    qidx = lambda b, i: (b, i, 0)
    const = lambda shp: pl.BlockSpec(shp, lambda b, i: (0,) * len(shp))
    in_specs = [
        pl.BlockSpec((1, QB, 512), qidx),
        pl.BlockSpec((1, QB, 256), qidx),
        pl.BlockSpec((1, 128, QB), lambda b, i: (b, 0, i)),
        pl.BlockSpec((1, nkt, KT, KX_COLS), lambda b, i: (b, 0, 0, 0)),
        pl.BlockSpec((1, nkt, vt, KT), lambda b, i: (b, 0, 0, 0)),
        pl.BlockSpec((1, nkt, KT, 128), lambda b, i: (b, 0, 0, 0)),
        const((META_ROWS, 512)), const((META_ROWS, 256)), const((128, META_ROWS)),
        const((META_ROWS, KX_COLS)), const((1, vt, KT)), const((META_ROWS, 128)),
        const((A_HEADS, LANES)), const((1, 512)),
    ]
    out_shape = jax.ShapeDtypeStruct((nb, s, 512), BF16)
    out_specs = pl.BlockSpec((1, QB, 512), qidx)
    scratch = [
        pltpu.VMEM((nkt + 1, KT, QB), F32),
        pltpu.VMEM((nkt + 1, KT, QB), BF16),
        pltpu.VMEM((A_HEADS, QB, 2 * LANES), BF16),
        pltpu.VMEM((IDX_HEADS, QB, LANES), BF16),
        pltpu.VMEM((8, QB), F32),
        pltpu.VMEM((A_HEADS, 1, QB), F32),
        pltpu.VMEM((A_HEADS, VT_ROWS, QB), F32),
        pltpu.VMEM((GROUP_ROWS, QB), F32),
        pltpu.VMEM((A_HEADS, GROUP_ROWS, QB), F32),
        pltpu.VMEM((A_HEADS, GROUP_ROWS, QB), BF16),
        pltpu.VMEM((1, QB), F32),
        pltpu.VMEM((A_HEADS, 1, QB), F32),
        pltpu.VMEM((A_HEADS, 1, QB), F32),
    ]
    return pl.pallas_call(
        functools.partial(_dsa_kernel, topk=topk, is_meta=is_meta),
        grid=(nb, nq),
        in_specs=in_specs,
        out_specs=out_specs,
        out_shape=out_shape,
        scratch_shapes=scratch,
        compiler_params=pltpu.CompilerParams(
            dimension_semantics=("arbitrary", "arbitrary"), vmem_limit_bytes=VMEM_LIMIT),
        name="dsa",
    )(aq, iq, miscT, akx.reshape(nb, nkt, KT, KX_COLS), avT, ik2.reshape(nb, nkt, KT, 128),
      m["aq"], m["iq"], m["miscT"], m["akx"], m["avT"], m["ik2"], _alibi_query_features(), ng)


def _chunk_cumsum_rows(x):
    row_in_chunk = lax.broadcasted_iota(I32, x.shape, 0) & (CHUNK - 1)
    sh = 1
    while sh < CHUNK:
        x = x + jnp.where(row_in_chunk >= sh, pltpu.roll(x, sh, 0), 0.0)
        sh *= 2
    return x


def _rows_of_chunk(rows_per_chunk):
    return jnp.concatenate([jnp.broadcast_to(r, (CHUNK, r.shape[1])) for r in rows_per_chunk], axis=0)


def _gla_block(q, k, v, lr, og, wg, bg, ng, st_s, valid_rows):
    nrow = q.shape[0]
    nch = nrow // CHUNK
    z = _dot(lr.astype(BF16), wg) + bg
    logg = (jnp.minimum(z, 0.0) - jnp.log(1.0 + jnp.exp(-jnp.abs(z)))) / B_GATE_TAU
    if valid_rows is not None:
        rows = lax.broadcasted_iota(I32, logg.shape, 0)
        logg = jnp.where(rows < valid_rows, logg, 0.0)
    b = _chunk_cumsum_rows(logg)
    b_last_rows = [b[(c + 1) * CHUNK - 1:(c + 1) * CHUNK, :] for c in range(nch)]
    b_mid = _rows_of_chunk([b[c * CHUNK + CHUNK // 2 - 1:c * CHUNK + CHUNK // 2, :] for c in range(nch)])
    b_last = _rows_of_chunk(b_last_rows)
    q_in = q * jnp.exp(b)
    q_ic = q * jnp.exp(b - b_mid)
    k_ic = k * jnp.exp(b_mid - b)
    k_st = k * jnp.exp(b_last - b)
    decay = [jnp.exp(r) for r in b_last_rows]

    lane_half = lax.broadcasted_iota(I32, (1, LANES), 1) // B_KEY_DIM
    ri = lax.broadcasted_iota(I32, (nrow, nrow), 0)
    ci = lax.broadcasted_iota(I32, (nrow, nrow), 1)
    causal = ((ri >> 6) == (ci >> 6)) & (ci <= ri)
    outs = []
    for p in range(B_HEADS // 2):
        sl = slice(p * LANES, (p + 1) * LANES)
        heads = (2 * p, 2 * p + 1)
        v_h = [v[:, h * B_VAL_DIM:(h + 1) * B_VAL_DIM] for h in heads]
        k_st_h = [jnp.where(lane_half == hh, k_st[:, sl], 0.0).astype(BF16) for hh in range(2)]
        states = [st_s[p]]
        for c in range(nch):
            rs = slice(c * CHUNK, (c + 1) * CHUNK)
            upd = _dot_tn(v_h[0][rs], k_st_h[0][rs]) + _dot_tn(v_h[1][rs], k_st_h[1][rs])
            states.append(states[c] * decay[c][:, sl] + upd)
        st_s[p] = states[nch]
        k_ic_p = k_ic[:, sl].astype(BF16)
        for hh, h in enumerate(heads):
            hm = lane_half == hh
            a = _dot_nt(jnp.where(hm, q_ic[:, sl], 0.0).astype(BF16), k_ic_p)
            o = _dot(jnp.where(causal, a, 0.0).astype(BF16), v_h[hh])
            q_in_h = jnp.where(hm, q_in[:, sl], 0.0).astype(BF16)
            o = o + jnp.concatenate(
                [_dot_nt(q_in_h[c * CHUNK:(c + 1) * CHUNK], states[c].astype(BF16)) for c in range(nch)], axis=0)
            ms = jnp.mean(o * o, axis=-1, keepdims=True)
            on = o * lax.rsqrt(ms + RMS_EPS) * ng[:, h * B_VAL_DIM:(h + 1) * B_VAL_DIM]
            og_h = og[:, h * B_VAL_DIM:(h + 1) * B_VAL_DIM]
            outs.append(on * (og_h * jax.nn.sigmoid(og_h)))
    return outs


def _gla_kernel(bq_ref, bk_ref, bv_ref, misc_ref, bog_ref,
                mbq_ref, mbk_ref, mbv_ref, mmisc_ref, mbog_ref,
                wg_ref, bg_ref, ng_ref, out_ref, mout_ref, st_s):
    j = pl.program_id(1)
    wg = wg_ref[...]
    bg = bg_ref[...]
    ng = ng_ref[...]

    @pl.when(j == 0)
    def _():
        st_s[...] = jnp.zeros(st_s.shape, F32)
        outs = _gla_block(mbq_ref[0:CHUNK, :].astype(F32), mbk_ref[0:CHUNK, :].astype(F32),
                          mbv_ref[0:CHUNK, :], mmisc_ref[0:CHUNK, MISC_LR:MISC_LR + B_GATE_RANK],
                          mbog_ref[0:CHUNK, :], wg, bg, ng, st_s, N_META)
        mout_ref[CHUNK:, :] = jnp.zeros((META_ROWS - CHUNK, B_WIDTH), BF16)
        for h, o in enumerate(outs):
            mout_ref[0:CHUNK, h * B_VAL_DIM:(h + 1) * B_VAL_DIM] = o.astype(BF16)

    outs = _gla_block(bq_ref[0].astype(F32), bk_ref[0].astype(F32), bv_ref[0],
                      misc_ref[0, :, MISC_LR:MISC_LR + B_GATE_RANK], bog_ref[0], wg, bg, ng, st_s, None)
    for h, o in enumerate(outs):
        out_ref[0, :, h * B_VAL_DIM:(h + 1) * B_VAL_DIM] = o.astype(BF16)


def _gla(bq, bk, bv, misc, bog, m, wg, bg, ng):
    nb, s, _ = bq.shape
    row = lambda c: pl.BlockSpec((1, GLA_BLOCK, c), lambda b, j: (b, j, 0))
    const = lambda shp: pl.BlockSpec(shp, lambda b, j: (0,) * len(shp))
    return pl.pallas_call(
        _gla_kernel,
        grid=(nb, s // GLA_BLOCK),
        in_specs=[row(256), row(256), row(512), row(128), row(512),
                  const((META_ROWS, 256)), const((META_ROWS, 256)), const((META_ROWS, 512)),
                  const((META_ROWS, 128)), const((META_ROWS, 512)),
                  const((B_GATE_RANK, 256)), const((1, 256)), const((1, 512))],
        out_specs=(row(512), const((META_ROWS, 512))),
        out_shape=(jax.ShapeDtypeStruct((nb, s, 512), BF16), jax.ShapeDtypeStruct((META_ROWS, 512), BF16)),
        scratch_shapes=[pltpu.VMEM((B_HEADS // 2, LANES, LANES), F32)],
        compiler_params=pltpu.CompilerParams(
            dimension_semantics=("arbitrary", "arbitrary"), vmem_limit_bytes=VMEM_LIMIT),
        name="gla",
    )(bq, bk, bv, misc, bog, m["bq"], m["bk"], m["bv"], m["misc"], m["bog"], wg, bg, ng)


def _outproj_kernel(x_ref, a_ref, b_ref, lng_ref, lnb_ref, wo_ref, g1_ref, b1_ref, out_ref):
    h = _layer_norm(x_ref[0], lng_ref[...], lnb_ref[...])
    mixed = _dot(a_ref[0], wo_ref[0:A_WIDTH, :]) + _dot(b_ref[0], wo_ref[A_WIDTH:, :])
    out_ref[0] = _layer_norm(ALPHA * h + mixed, g1_ref[...], b1_ref[...])


def _outproj(x, a, b, lng, lnb, wo, g1, b1, tm):
    nb, s, d = x.shape
    row = lambda c: pl.BlockSpec((1, tm, c), lambda bb, i: (bb, i, 0))
    const = lambda shp: pl.BlockSpec(shp, lambda bb, i: (0,) * len(shp))
    return pl.pallas_call(
        _outproj_kernel,
        grid=(nb, s // tm),
        in_specs=[row(d), row(512), row(512), const((1, d)), const((1, d)), const((d, d)),
                  const((1, d)), const((1, d))],
        out_specs=row(d),
        out_shape=jax.ShapeDtypeStruct((nb, s, d), F32),
        compiler_params=pltpu.CompilerParams(
            dimension_semantics=("arbitrary", "arbitrary"), vmem_limit_bytes=VMEM_LIMIT),
        name="outproj",
    )(x, a, b, lng, lnb, wo, g1, b1)


def _gelu_tanh(x):
    c = 0.7978845608028654
    return 0.5 * x * (1.0 + jnp.tanh(c * (x + 0.044715 * (x * x * x))))


def _ffn_kernel(x_ref, xh_ref, a_ref, ah_ref, b_ref, bh_ref, hm_ref, lng_ref, lnb_ref, wo_ref, g1_ref, b1_ref,
                wup_ref, cw_ref, cb_ref, wdn_ref, g2_ref, b2_ref, out_ref, y_s):
    j = pl.program_id(1)
    def mix(x, a_mix, b_mix):
        mixed = _dot(a_mix, wo_ref[0:A_WIDTH, :]) + _dot(b_mix, wo_ref[A_WIDTH:, :])
        return _layer_norm(ALPHA * _layer_norm(x, lng_ref[...], lnb_ref[...]) + mixed, g1_ref[...], b1_ref[...])

    half = x_ref.shape[1] // 2
    h1_lo = mix(jnp.concatenate([xh_ref[0], x_ref[0, 0:half, :]], axis=0),
                jnp.concatenate([ah_ref[0], a_ref[0, 0:half, :]], axis=0),
                jnp.concatenate([bh_ref[0], b_ref[0, 0:half, :]], axis=0))
    h1_hi = mix(x_ref[0, half:, :], a_ref[0, half:, :], b_ref[0, half:, :])
    h = jnp.concatenate([h1_lo[HALO:, :], h1_hi], axis=0)
    halo = jnp.where(j == 0, hm_ref[...], h1_lo[0:HALO, :])
    hb = jnp.concatenate([halo, h], axis=0).astype(BF16)
    for f in range(D_FF // FF_TILE):
        fs = slice(f * FF_TILE, (f + 1) * FF_TILE)
        a = _dot(hb, wup_ref[:, fs])
        gate = _dot(hb[HALO:, :], wup_ref[:, D_FF + f * FF_TILE:D_FF + (f + 1) * FF_TILE])
        cw = cw_ref[:, fs]
        conv = cb_ref[:, fs] + cw[CONV_W - 1:CONV_W, :] * a[HALO:, :]
        for back in range(1, CONV_W):
            tap = cw[CONV_W - 1 - back:CONV_W - back, :]
            conv = conv + tap * pltpu.roll(a, back, 0)[HALO:, :]
        y_s[:, fs] = (_gelu_tanh(conv) * gate).astype(BF16)
    ffn = _dot(y_s[...], wdn_ref[...])
    out_ref[0] = _layer_norm(ALPHA * h + ffn, g2_ref[...], b2_ref[...])


def _ffn(x, a, b, h1_meta, lng, lnb, wo, g1, b1, wup, cw, cb, wdn, g2, b2, tm):
    nb, s, d = x.shape
    per = tm // HALO
    const = lambda shp: pl.BlockSpec(shp, lambda bb, j: (0,) * len(shp), pipeline_mode=pl.Buffered(1))
    main = lambda c: pl.BlockSpec((1, tm, c), lambda bb, j: (bb, j, 0))
    halo = lambda c: pl.BlockSpec((1, HALO, c), lambda bb, j: (bb, jnp.maximum(j * per - 1, 0), 0))
    return pl.pallas_call(
        _ffn_kernel,
        grid=(nb, s // tm),
        in_specs=[main(d), halo(d), main(A_WIDTH), halo(A_WIDTH), main(B_WIDTH), halo(B_WIDTH),
                  const((HALO, d)), const((1, d)), const((1, d)), const((d, d)), const((1, d)), const((1, d)),
                  const((d, 2 * D_FF)), const((CONV_W, D_FF)), const((1, D_FF)),
                  const((D_FF, d)), const((1, d)), const((1, d))],
        out_specs=main(d),
        out_shape=jax.ShapeDtypeStruct((nb, s, d), F32),
        scratch_shapes=[pltpu.VMEM((tm, D_FF), BF16)],
        compiler_params=pltpu.CompilerParams(
            dimension_semantics=("arbitrary", "arbitrary"), vmem_limit_bytes=VMEM_LIMIT),
        name="ffn",
    )(x, x, a, a, b, b, h1_meta, lng, lnb, wo, g1, b1, wup, cw, cb, wdn, g2, b2)


def _permute_w_in(w):
    o = 0
    parts = {}
    for name, n in (("a_q", 512), ("a_k", 512), ("a_v", 512), ("i_q", 256), ("i_k", 64), ("i_w", 4),
                    ("b_q", 256), ("b_k", 256), ("b_v", 512), ("b_lr", 16), ("b_og", 512)):
        parts[name] = w[:, o:o + n]
        o += n
    zeros = jnp.zeros((w.shape[0], MISC_IK - B_GATE_RANK - IDX_HEADS), w.dtype)
    cols = [parts["a_q"] * (A_HEAD_DIM ** -0.5 * LOG2E), parts["a_k"], parts["a_v"], parts["i_q"],
            parts["b_q"] * (B_KEY_DIM ** -0.5), parts["b_k"], parts["b_v"], parts["b_og"],
            parts["b_lr"], parts["i_w"] * (IDX_HEADS ** -0.5) * (IDX_DIM ** -0.5), zeros, parts["i_k"]]
    return jnp.concatenate(cols, axis=1).astype(BF16)


def kernel(x, meta, ln_in_g, ln_in_b, w_in, w_gate_b, b_gate_b, attn_norm_g, gla_norm_g, w_out,
           ln1_g, ln1_b, w_up, conv_w, conv_b, w_down, ln2_g, ln2_b):
    nb, s, d = x.shape
    assert d == D_MODEL and w_in.shape[0] == DEPTH == 1
    assert s % ROW_TILE == 0 and s % KT == 0 and s % GLA_BLOCK == 0
    topk = min(TOPK_MAX, s // 4)
    l = 0
    r2 = lambda v: v.reshape(1, -1)
    lng, lnb = r2(ln_in_g), r2(ln_in_b)
    w_perm = _permute_w_in(w_in[l])
    wo = w_out[l].astype(BF16)
    wup = w_up[l].astype(BF16)
    wdn = w_down[l].astype(BF16)
    wg = w_gate_b[l].astype(BF16)
    bg = r2(b_gate_b[l])
    ng_a = r2(attn_norm_g[l])
    ng_b = r2(gla_norm_g[l])

    x_meta = jnp.zeros((1, META_ROWS, d), x.dtype).at[0, :N_META].set(meta.astype(x.dtype))
    names = ("aq", "akx", "avT", "iq", "bq", "bk", "bv", "bog", "ik2", "misc", "miscT")
    m = {n: v[0] for n, v in zip(names, _inproj(x_meta, lng, lnb, w_perm, META_ROWS, 0, valid_rows=N_META))}

    aq, akx, avT, iq, bq, bk, bv, bog, ik2, misc, miscT = _inproj(x, lng, lnb, w_perm, ROW_TILE, N_META)
    a_out = _dsa(aq, iq, miscT, akx, avT, ik2, m, ng_a, topk, False)
    a_out_m = _dsa(m["aq"][None], m["iq"][None], m["miscT"][None], m["akx"][None], m["avT"][None],
                   m["ik2"][None], m, ng_a, topk, True)[0]
    b_out, b_out_m = _gla(bq, bk, bv, misc, bog, m, wg, bg, ng_b)

    g1, b1 = r2(ln1_g[l]), r2(ln1_b[l])
    h1_m = _outproj(x_meta, a_out_m[None], b_out_m[None], lng, lnb, wo, g1, b1, META_ROWS)
    return _ffn(x, a_out, b_out, h1_m[0, N_META - HALO:N_META], lng, lnb, wo, g1, b1,
                wup, conv_w[l], r2(conv_b[l]), wdn, r2(ln2_g[l]), r2(ln2_b[l]), ROW_TILE)
```

```python
import functools

import jax
import jax.numpy as jnp
from jax import lax
from jax.experimental import pallas as pl
from jax.experimental.pallas import tpu as pltpu

F32 = jnp.float32
BF16 = jnp.bfloat16
I32 = jnp.int32

D_MODEL = 1024
CHUNK = 64
N_META = 16
A_HEADS = 8
A_WIDTH = 512
A_HEAD_DIM = 64
IDX_HEADS = 4
IDX_DIM = 64
TOPK_MAX = 256
B_HEADS = 4
B_WIDTH = 512
B_VAL_DIM = 128
B_KEY_DIM = 64
B_GATE_RANK = 16
B_GATE_TAU = 16.0
D_FF = 2816
CONV_W = 3
LN_EPS = 1e-5
RMS_EPS = 1e-6
DEPTH = 1
ALPHA = (2.0 * DEPTH) ** 0.25

LANES = 128
ROW_TILE = 512
QB = 256
KT = 256
MT = 64
GROUP_ROWS = MT + KT
META_ROWS = 256
GLA_BLOCK = 256
FF_TILE = 256
HALO = 16
CHUNK_SHIFT = CHUNK.bit_length() - 1
BF16_BITS = 16
BF16_SIGN, BF16_ALL = 0x8000, 0xFFFF
F32_MAGNITUDE = 0x7FFFFFFF
FINE_BITS = 17
U_NEG_INF, U_POS_INF = 0x007F, 0xFF80
N_SLOPE_PARTS = 3
VT_ROWS = 80
KX_COLS = 2 * A_WIDTH
LOG2E = 1.4426950408889634
ALIBI_C = tuple(2.0 ** (-8.0 * (h + 1) / A_HEADS) * LOG2E for h in range(A_HEADS))
VMEM_LIMIT = 56 * 1024 * 1024

_COLS = {}
_off = 0
for _name, _w in (("aq", 512), ("ak", 512), ("av", 512), ("iq", 256), ("bq", 256), ("bk", 256),
                  ("bv", 512), ("bog", 512), ("misc", 128)):
    _COLS[_name] = (_off, _w)
    _off += _w
PROJ_COLS = _off
MISC_LR = 0
MISC_IW = 16
MISC_IK = 64


def _layer_norm(x, g, b):
    mu = jnp.mean(x, axis=-1, keepdims=True)
    xc = x - mu
    var = jnp.mean(xc * xc, axis=-1, keepdims=True)
    return xc * lax.rsqrt(var + LN_EPS) * g + b


def _dot(a, b):
    return jnp.dot(a, b, preferred_element_type=F32)


def _dot_nt(a, b):
    return lax.dot_general(a, b, (((1,), (1,)), ((), ())), preferred_element_type=F32)


def _dot_tn(a, b):
    return lax.dot_general(a, b, (((0,), (0,)), ((), ())), preferred_element_type=F32)


def _inproj_kernel(x_ref, g_ref, b_ref, w_ref, aq_ref, akx_ref, avT_ref, iq_ref, bq_ref, bk_ref,
                   bv_ref, bog_ref, ik2_ref, misc_ref, miscT_ref, *, pos0, valid_rows):
    h = _layer_norm(x_ref[0], g_ref[...], b_ref[...])
    if valid_rows is not None:
        h = jnp.where(lax.broadcasted_iota(I32, h.shape, 0) < valid_rows, h, 0.0)
    hb = h.astype(BF16)
    tm = hb.shape[0]

    def proj(name):
        lo, w = _COLS[name]
        return _dot(hb, w_ref[:, lo:lo + w])

    aq_ref[0] = proj("aq").astype(BF16)
    ak = proj("ak")
    rows = lax.broadcasted_iota(I32, (tm, LANES), 0)
    lanes = lax.broadcasted_iota(I32, (tm, LANES), 1)
    pos = pos0 + pl.program_id(1) * tm + rows
    feat = jnp.where(lanes < 2 * N_SLOPE_PARTS, jnp.where(lanes % 2 == 0, pos >> CHUNK_SHIFT, pos & (CHUNK - 1)), 0)
    feat = feat.astype(F32).astype(BF16)
    for p in range(A_HEADS // 2):
        akx_ref[0, :, 2 * p * LANES:(2 * p + 1) * LANES] = ak[:, p * LANES:(p + 1) * LANES].astype(BF16)
        akx_ref[0, :, (2 * p + 1) * LANES:(2 * p + 2) * LANES] = feat
    avT = proj("av").T
    ones = jnp.ones((VT_ROWS - A_HEAD_DIM, KT), BF16)
    for i in range(avT_ref.shape[1]):
        for hd in range(A_HEADS):
            avT_ref[0, i, hd * VT_ROWS:hd * VT_ROWS + A_HEAD_DIM, :] = (
                avT[hd * A_HEAD_DIM:(hd + 1) * A_HEAD_DIM, i * KT:(i + 1) * KT].astype(BF16))
            avT_ref[0, i, hd * VT_ROWS + A_HEAD_DIM:(hd + 1) * VT_ROWS, :] = ones
    iq_ref[0] = proj("iq").astype(BF16)
    bq_ref[0] = proj("bq").astype(BF16)
    bk_ref[0] = proj("bk").astype(BF16)
    bv_ref[0] = proj("bv").astype(BF16)
    bog_ref[0] = proj("bog")
    misc = proj("misc")
    misc_ref[0] = misc
    miscT_ref[0] = misc.T
    ik_lo = pltpu.roll(misc, LANES - MISC_IK, 1)
    ik2_ref[0] = jnp.where(lanes < MISC_IK, ik_lo, misc).astype(BF16)


def _inproj(x, ln_g, ln_b, w_perm, tm, pos0, valid_rows=None):
    nb, s, d = x.shape
    nt = s // tm
    row = lambda c: pl.BlockSpec((1, tm, c), lambda b, i: (b, i, 0))
    const = lambda shp: pl.BlockSpec(shp, lambda b, i: (0,) * len(shp))
    out_shape = (
        jax.ShapeDtypeStruct((nb, s, 512), BF16),
        jax.ShapeDtypeStruct((nb, s, KX_COLS), BF16),
        jax.ShapeDtypeStruct((nb, s // KT, A_HEADS * VT_ROWS, KT), BF16),
        jax.ShapeDtypeStruct((nb, s, 256), BF16),
        jax.ShapeDtypeStruct((nb, s, 256), BF16),
        jax.ShapeDtypeStruct((nb, s, 256), BF16),
        jax.ShapeDtypeStruct((nb, s, 512), BF16),
        jax.ShapeDtypeStruct((nb, s, 512), F32),
        jax.ShapeDtypeStruct((nb, s, 128), BF16),
        jax.ShapeDtypeStruct((nb, s, 128), F32),
        jax.ShapeDtypeStruct((nb, 128, s), F32),
    )
    out_specs = (
        row(512), row(KX_COLS),
        pl.BlockSpec((1, tm // KT, A_HEADS * VT_ROWS, KT), lambda b, i: (b, i, 0, 0)),
        row(256), row(256), row(256), row(512), row(512), row(128), row(128),
        pl.BlockSpec((1, 128, tm), lambda b, i: (b, 0, i)),
    )
    return pl.pallas_call(
        functools.partial(_inproj_kernel, pos0=pos0, valid_rows=valid_rows),
        grid=(nb, nt),
        in_specs=[row(d), const((1, d)), const((1, d)), const((d, PROJ_COLS))],
        out_specs=out_specs,
        out_shape=out_shape,
        compiler_params=pltpu.CompilerParams(
            dimension_semantics=("arbitrary", "arbitrary"), vmem_limit_bytes=VMEM_LIMIT),
        name="inproj",
    )(x, ln_g, ln_b, w_perm)


def _static_when(cond: bool):
    def deco(fn):
        if cond:
            fn()
    return deco


def _dsa_kernel(aq_ref, iq_ref, miscT_ref, akx_ref, avT_ref, ik2_ref,
                maq_ref, miq_ref, mmiscT_ref, makx_ref, mavT_ref, mik2_ref, qfeat_ref, ng_ref,
                out_ref,
                sc_s, scb_s, qm_s, iqm_s, iw_s, m_s, acc_s, bias_s, lg_s, p_s, neq_s, shift_s, corr_s,
                *, topk, is_meta):
    g = 0 if is_meta else pl.program_id(1) + 1
    n_past = 0 if is_meta else g - 1
    when_real = _static_when(not is_meta)
    when_meta = _static_when(is_meta)

    lane_half = lax.broadcasted_iota(I32, (1, LANES), 1) // A_HEAD_DIM

    def stage(aq, iq, miscT):
        for h in range(A_HEADS):
            p = h // 2
            qp = aq[:, p * LANES:(p + 1) * LANES]
            qm_s[h, :, 0:LANES] = jnp.where(lane_half == (h % 2), qp, jnp.zeros_like(qp))
            qm_s[h, :, LANES:2 * LANES] = jnp.broadcast_to(qfeat_ref[h:h + 1, :], (QB, LANES)).astype(BF16)
        for h in range(IDX_HEADS):
            p = h // 2
            qp = iq[:, p * LANES:(p + 1) * LANES]
            iqm_s[h] = jnp.where(lane_half == (h % 2), qp, jnp.zeros_like(qp))
        iw_s[...] = miscT[MISC_IW:MISC_IW + 8, :]

    lane = lax.broadcasted_iota(I32, (1, QB), 1)
    if is_meta:
        stage(maq_ref[...], miq_ref[...], mmiscT_ref[...])
        qpos = lane
    else:
        stage(aq_ref[0], iq_ref[0], miscT_ref[0])
        qpos = N_META + (g - 1) * QB + lane

    ninf = jnp.float32(-jnp.inf)

    def scores_for_tile(ik2_t, allowed):
        s = None
        for h in range(IDX_HEADS):
            sh = _dot_nt(ik2_t, iqm_s[h])
            term = iw_s[h:h + 1, :] * jnp.maximum(sh, 0.0)
            s = term if s is None else s + term
        return s if allowed is None else jnp.where(allowed, s, ninf)

    def store_scores(slot, r, s):
        sc_s[slot, 0:r, :] = s
        scb_s[slot, 0:r, :] = s.astype(BF16)

    mrow = lax.broadcasted_iota(I32, (MT, QB), 0)
    krow = lax.broadcasted_iota(I32, (KT, QB), 0)
    store_scores(0, MT, scores_for_tile(mik2_ref[0:MT, :], mrow < N_META))

    def fill_past(kt):
        store_scores(kt + 1, KT, scores_for_tile(ik2_ref[0, kt], None))

    def fill_pair(i, c):
        fill_past(2 * i)
        fill_past(2 * i + 1)
        return c

    @when_real
    def _():
        lax.fori_loop(0, n_past // 2, fill_pair, 0)

        @pl.when(n_past % 2 == 1)
        def _():
            fill_past(n_past - 1)

        store_scores(g, KT, scores_for_tile(ik2_ref[0, g - 1], (krow >> CHUNK_SHIFT) <= (lane >> CHUNK_SHIFT)))

    def count(arr_s, pred, pack):
        acc_t = jnp.int16 if pack == 16 else I32

        def cnt(tile):
            x = jnp.where(pred(tile), jnp.ones((), acc_t), jnp.zeros((), acc_t))
            x = x.reshape(tile.shape[0] // pack, pack, QB)
            parts = [x[j] for j in range(x.shape[0])]
            while len(parts) > 1:
                parts = [parts[j] + parts[j + 1] for j in range(0, len(parts), 2)]
            return parts[0]

        acc = cnt(arr_s[0, 0:MT, :])
        acc = lax.fori_loop(0, g, lambda kt, a: a + cnt(arr_s[kt + 1]), acc)
        return jnp.sum(acc.astype(I32), axis=0, keepdims=True)

    def ordered16_to_bits(k):
        return jnp.where(k >= BF16_SIGN, k ^ BF16_SIGN, k ^ BF16_ALL)

    def coarse_body(i, u):
        cand_u = u | jnp.left_shift(jnp.int32(1), BF16_BITS - 1 - i)
        finite = (cand_u >= U_NEG_INF) & (cand_u <= U_POS_INF)
        bits = ordered16_to_bits(jnp.clip(cand_u, U_NEG_INF, U_POS_INF))
        cand = lax.bitcast_convert_type(jnp.left_shift(bits, BF16_BITS), F32).astype(BF16)
        cnt = count(scb_s, lambda tile: tile >= cand, 16)
        return jnp.where(finite & (cnt >= topk), cand_u, u)

    u1 = lax.fori_loop(0, BF16_BITS, coarse_body, jnp.zeros((1, QB), I32))
    few = u1 == 0
    t1_bits = jnp.left_shift(ordered16_to_bits(jnp.where(few, BF16_SIGN, u1)), BF16_BITS)
    base = (t1_bits ^ ((t1_bits >> 31) & F32_MAGNITUDE)) - BF16_SIGN

    def key_to_f32(k):
        return lax.bitcast_convert_type(k ^ ((k >> 31) & F32_MAGNITUDE), F32)

    def fine_body(i, o):
        cand_o = o | jnp.left_shift(jnp.int32(1), FINE_BITS - 1 - i)
        cand = key_to_f32(base + cand_o)
        cnt = count(sc_s, lambda tile: tile >= cand, 8)
        return jnp.where(cnt >= topk, cand_o, o)

    o2 = lax.fori_loop(0, FINE_BITS, fine_body, jnp.zeros((1, QB), I32))
    thr = jnp.where(few, ninf, key_to_f32(base + o2))
    n_gt = count(sc_s, lambda tile: tile > thr, 8)
    need = jnp.where(few, 0, topk - n_gt).astype(F32)

    m_s[...] = jnp.full(m_s.shape, -jnp.inf, F32)
    acc_s[...] = jnp.zeros(acc_s.shape, F32)

    def offsets(subs):
        offs, row0 = [], 0
        for sub in subs:
            offs.append((row0, sub[0].shape[0]))
            row0 += sub[0].shape[0]
        return offs, row0

    def bias_stage(subs):
        offs, _ = offsets(subs)
        n_eq_after = None
        for (sc_t, _, _, _, n_eq_before), (o, r) in zip(subs, offs):
            if n_eq_before is None:
                n_eq_before = n_eq_after
            ri = lax.broadcasted_iota(I32, (r, r), 0)
            ci = lax.broadcasted_iota(I32, (r, r), 1)
            lower = jnp.where(ri > ci, 1.0, 0.0).astype(BF16)
            eq = sc_t == thr
            rank = _dot(lower, jnp.where(eq, 1.0, 0.0).astype(BF16)) + n_eq_before
            bias_s[o:o + r, :] = jnp.where(
                sc_t > thr, 0.0, jnp.where(eq, jnp.where(rank < need, 0.0, ninf), ninf))
            n_eq_after = rank[r - 1:r, :] + jnp.where(eq[r - 1:r, :], 1.0, 0.0)
        return n_eq_after

    def logits_head(subs, h):
        offs, _ = offsets(subs)
        m_old = m_s[h]
        m_new = m_old
        for (_, kx_fn, _, after, _), (o, r) in zip(subs, offs):
            lg = _dot_nt(kx_fn(h // 2), qm_s[h]) + bias_s[o:o + r, :]
            if after is not None:
                lg = lg + ALIBI_C[h] * after
            lg_s[h, o:o + r, :] = lg
            m_new = jnp.maximum(m_new, jnp.max(lg, axis=0, keepdims=True))
        m_s[h] = m_new
        m_safe = jnp.where(m_new == ninf, 0.0, m_new)
        shift_s[h] = m_safe
        corr_s[h] = jnp.exp2(m_old - m_safe)

    def softmax_pv_head(subs, h):
        offs, rows = offsets(subs)
        p_s[h, 0:rows, :] = jnp.exp2(lg_s[h, 0:rows, :] - shift_s[h]).astype(BF16)
        acc = acc_s[h] * corr_s[h]
        for (_, _, vT_fn, _, _), (o, r) in zip(subs, offs):
            acc = acc + _dot(vT_fn(h), p_s[h, o:o + r, :])
        acc_s[h] = acc

    def run_stages(done, nxt):
        n_eq_after = bias_stage(nxt) if nxt is not None else None
        for h in range(A_HEADS):
            if done is not None:
                softmax_pv_head(done, h)
            if nxt is not None:
                logits_head(nxt, h)
        return n_eq_after

    def after_term(kpos):
        return jnp.minimum(2 * (qpos - kpos), 0).astype(F32)

    def meta_sub():
        return (sc_s[0, 0:MT, :],
                lambda p: makx_ref[0:MT, 2 * p * LANES:(2 * p + 2) * LANES],
                lambda h: mavT_ref[0, h * VT_ROWS:(h + 1) * VT_ROWS, 0:MT],
                after_term(mrow), jnp.zeros((1, QB), F32))

    def real_sub(kt, after, n_eq_before):
        return (sc_s[kt + 1],
                lambda p: akx_ref[0, kt, :, 2 * p * LANES:(2 * p + 2) * LANES],
                lambda h: avT_ref[0, kt, h * VT_ROWS:(h + 1) * VT_ROWS, :],
                after, n_eq_before)

    neq_s[...] = jnp.sum(jnp.where(sc_s[0, 0:MT, :] == thr, 1.0, 0.0), axis=0, keepdims=True)

    def past(kt):
        return [real_sub(kt, None, neq_s[...])]

    def last_group():
        return [meta_sub(), real_sub(g - 1, after_term(N_META + (g - 1) * KT + krow), neq_s[...])]

    @when_real
    def _():
        @pl.when(n_past > 0)
        def _():
            neq_s[...] = run_stages(None, past(0))

        def pipe_body(i, c):
            neq_s[...] = run_stages(past(i), past(i + 1))
            return c

        lax.fori_loop(0, jnp.maximum(n_past - 1, 0), pipe_body, 0)

        @pl.when(n_past > 0)
        def _():
            run_stages(past(n_past - 1), last_group())

        @pl.when(n_past == 0)
        def _():
            run_stages(None, last_group())

        run_stages(last_group(), None)

    @when_meta
    def _():
        run_stages(None, [meta_sub()])
        run_stages([meta_sub()], None)

    outs = []
    for h in range(A_HEADS):
        o = acc_s[h, 0:A_HEAD_DIM, :] / acc_s[h, A_HEAD_DIM:A_HEAD_DIM + 1, :]
        ms = jnp.mean(o * o, axis=0, keepdims=True)
        outs.append(o * lax.rsqrt(ms + RMS_EPS))
    res = (jnp.concatenate(outs, axis=0).T * ng_ref[...]).astype(BF16)

    out_ref[0] = res


def _alibi_query_features():
    rows = []
    for c in ALIBI_C:
        rest = jnp.float32(c)
        lanes = []
        for _ in range(N_SLOPE_PARTS):
            part = rest.astype(BF16).astype(F32)
            lanes += [part * CHUNK, part]
            rest = rest - part
        rows.append(jnp.stack(lanes + [jnp.float32(0.0)] * (LANES - len(lanes))))
    return jnp.stack(rows)


def _dsa(aq, iq, miscT, akx, avT, ik2, m, ng, topk, is_meta):
    nb, s, _ = aq.shape
    nq = s // QB
    nkt = s // KT
    vt = A_HEADS * VT_ROWS
    qidx = lambda b, i: (b, i, 0)
    const = lambda shp: pl.BlockSpec(shp, lambda b, i: (0,) * len(shp))
    in_specs = [
        pl.BlockSpec((1, QB, 512), qidx),
        pl.BlockSpec((1, QB, 256), qidx),
        pl.BlockSpec((1, 128, QB), lambda b, i: (b, 0, i)),
        pl.BlockSpec((1, nkt, KT, KX_COLS), lambda b, i: (b, 0, 0, 0)),
        pl.BlockSpec((1, nkt, vt, KT), lambda b, i: (b, 0, 0, 0)),
        pl.BlockSpec((1, nkt, KT, 128), lambda b, i: (b, 0, 0, 0)),
        const((META_ROWS, 512)), const((META_ROWS, 256)), const((128, META_ROWS)),
        const((META_ROWS, KX_COLS)), const((1, vt, KT)), const((META_ROWS, 128)),
        const((A_HEADS, LANES)), const((1, 512)),
    ]
    out_shape = jax.ShapeDtypeStruct((nb, s, 512), BF16)
    out_specs = pl.BlockSpec((1, QB, 512), qidx)
    scratch = [
        pltpu.VMEM((nkt + 1, KT, QB), F32),
        pltpu.VMEM((nkt + 1, KT, QB), BF16),
        pltpu.VMEM((A_HEADS, QB, 2 * LANES), BF16),
        pltpu.VMEM((IDX_HEADS, QB, LANES), BF16),
        pltpu.VMEM((8, QB), F32),
        pltpu.VMEM((A_HEADS, 1, QB), F32),
        pltpu.VMEM((A_HEADS, VT_ROWS, QB), F32),
        pltpu.VMEM((GROUP_ROWS, QB), F32),
        pltpu.VMEM((A_HEADS, GROUP_ROWS, QB), F32),
        pltpu.VMEM((A_HEADS, GROUP_ROWS, QB), BF16),
        pltpu.VMEM((1, QB), F32),
        pltpu.VMEM((A_HEADS, 1, QB), F32),
        pltpu.VMEM((A_HEADS, 1, QB), F32),
    ]
    return pl.pallas_call(
        functools.partial(_dsa_kernel, topk=topk, is_meta=is_meta),
        grid=(nb, nq),
        in_specs=in_specs,
        out_specs=out_specs,
        out_shape=out_shape,
        scratch_shapes=scratch,
        compiler_params=pltpu.CompilerParams(
            dimension_semantics=("arbitrary", "arbitrary"), vmem_limit_bytes=VMEM_LIMIT),
        name="dsa",
    )(aq, iq, miscT, akx.reshape(nb, nkt, KT, KX_COLS), avT, ik2.reshape(nb, nkt, KT, 128),
      m["aq"], m["iq"], m["miscT"], m["akx"], m["avT"], m["ik2"], _alibi_query_features(), ng)


def _chunk_cumsum_rows(x):
    row_in_chunk = lax.broadcasted_iota(I32, x.shape, 0) & (CHUNK - 1)
    sh = 1
    while sh < CHUNK:
        x = x + jnp.where(row_in_chunk >= sh, pltpu.roll(x, sh, 0), 0.0)
        sh *= 2
    return x


def _rows_of_chunk(rows_per_chunk):
    return jnp.concatenate([jnp.broadcast_to(r, (CHUNK, r.shape[1])) for r in rows_per_chunk], axis=0)


def _gla_block(q, k, v, lr, og, wg, bg, ng, st_s, valid_rows):
    nrow = q.shape[0]
    nch = nrow // CHUNK
    z = _dot(lr.astype(BF16), wg) + bg
    logg = (jnp.minimum(z, 0.0) - jnp.log(1.0 + jnp.exp(-jnp.abs(z)))) / B_GATE_TAU
    if valid_rows is not None:
        rows = lax.broadcasted_iota(I32, logg.shape, 0)
        logg = jnp.where(rows < valid_rows, logg, 0.0)
    b = _chunk_cumsum_rows(logg)
    b_last_rows = [b[(c + 1) * CHUNK - 1:(c + 1) * CHUNK, :] for c in range(nch)]
    b_mid = _rows_of_chunk([b[c * CHUNK + CHUNK // 2 - 1:c * CHUNK + CHUNK // 2, :] for c in range(nch)])
    b_last = _rows_of_chunk(b_last_rows)
    q_in = q * jnp.exp(b)
    q_ic = q * jnp.exp(b - b_mid)
    k_ic = k * jnp.exp(b_mid - b)
    k_st = k * jnp.exp(b_last - b)
    decay = [jnp.exp(r) for r in b_last_rows]

    lane_half = lax.broadcasted_iota(I32, (1, LANES), 1) // B_KEY_DIM
    ri = lax.broadcasted_iota(I32, (nrow, nrow), 0)
    ci = lax.broadcasted_iota(I32, (nrow, nrow), 1)
    causal = ((ri >> CHUNK_SHIFT) == (ci >> CHUNK_SHIFT)) & (ci <= ri)
    outs = []
    for p in range(B_HEADS // 2):
        sl = slice(p * LANES, (p + 1) * LANES)
        heads = (2 * p, 2 * p + 1)
        v_h = [v[:, h * B_VAL_DIM:(h + 1) * B_VAL_DIM] for h in heads]
        k_st_h = [jnp.where(lane_half == hh, k_st[:, sl], 0.0).astype(BF16) for hh in range(2)]
        states = [st_s[p]]
        for c in range(nch):
            rs = slice(c * CHUNK, (c + 1) * CHUNK)
            upd = _dot_tn(v_h[0][rs], k_st_h[0][rs]) + _dot_tn(v_h[1][rs], k_st_h[1][rs])
            states.append(states[c] * decay[c][:, sl] + upd)
        st_s[p] = states[nch]
        k_ic_p = k_ic[:, sl].astype(BF16)
        for hh, h in enumerate(heads):
            hm = lane_half == hh
            a = _dot_nt(jnp.where(hm, q_ic[:, sl], 0.0).astype(BF16), k_ic_p)
            o = _dot(jnp.where(causal, a, 0.0).astype(BF16), v_h[hh])
            q_in_h = jnp.where(hm, q_in[:, sl], 0.0).astype(BF16)
            o = o + jnp.concatenate(
                [_dot_nt(q_in_h[c * CHUNK:(c + 1) * CHUNK], states[c].astype(BF16)) for c in range(nch)], axis=0)
            ms = jnp.mean(o * o, axis=-1, keepdims=True)
            on = o * lax.rsqrt(ms + RMS_EPS) * ng[:, h * B_VAL_DIM:(h + 1) * B_VAL_DIM]
            og_h = og[:, h * B_VAL_DIM:(h + 1) * B_VAL_DIM]
            outs.append(on * (og_h * jax.nn.sigmoid(og_h)))
    return outs


def _gla_kernel(bq_ref, bk_ref, bv_ref, misc_ref, bog_ref,
                mbq_ref, mbk_ref, mbv_ref, mmisc_ref, mbog_ref,
                wg_ref, bg_ref, ng_ref, out_ref, mout_ref, st_s):
    j = pl.program_id(1)
    wg = wg_ref[...]
    bg = bg_ref[...]
    ng = ng_ref[...]

    @pl.when(j == 0)
    def _():
        st_s[...] = jnp.zeros(st_s.shape, F32)
        outs = _gla_block(mbq_ref[0:CHUNK, :].astype(F32), mbk_ref[0:CHUNK, :].astype(F32),
                          mbv_ref[0:CHUNK, :], mmisc_ref[0:CHUNK, MISC_LR:MISC_LR + B_GATE_RANK],
                          mbog_ref[0:CHUNK, :], wg, bg, ng, st_s, N_META)
        mout_ref[CHUNK:, :] = jnp.zeros((META_ROWS - CHUNK, B_WIDTH), BF16)
        for h, o in enumerate(outs):
            mout_ref[0:CHUNK, h * B_VAL_DIM:(h + 1) * B_VAL_DIM] = o.astype(BF16)

    outs = _gla_block(bq_ref[0].astype(F32), bk_ref[0].astype(F32), bv_ref[0],
                      misc_ref[0, :, MISC_LR:MISC_LR + B_GATE_RANK], bog_ref[0], wg, bg, ng, st_s, None)
    for h, o in enumerate(outs):
        out_ref[0, :, h * B_VAL_DIM:(h + 1) * B_VAL_DIM] = o.astype(BF16)


def _gla(bq, bk, bv, misc, bog, m, wg, bg, ng):
    nb, s, _ = bq.shape
    row = lambda c: pl.BlockSpec((1, GLA_BLOCK, c), lambda b, j: (b, j, 0))
    const = lambda shp: pl.BlockSpec(shp, lambda b, j: (0,) * len(shp))
    return pl.pallas_call(
        _gla_kernel,
        grid=(nb, s // GLA_BLOCK),
        in_specs=[row(256), row(256), row(512), row(128), row(512),
                  const((META_ROWS, 256)), const((META_ROWS, 256)), const((META_ROWS, 512)),
                  const((META_ROWS, 128)), const((META_ROWS, 512)),
                  const((B_GATE_RANK, 256)), const((1, 256)), const((1, 512))],
        out_specs=(row(512), const((META_ROWS, 512))),
        out_shape=(jax.ShapeDtypeStruct((nb, s, 512), BF16), jax.ShapeDtypeStruct((META_ROWS, 512), BF16)),
        scratch_shapes=[pltpu.VMEM((B_HEADS // 2, LANES, LANES), F32)],
        compiler_params=pltpu.CompilerParams(
            dimension_semantics=("arbitrary", "arbitrary"), vmem_limit_bytes=VMEM_LIMIT),
        name="gla",
    )(bq, bk, bv, misc, bog, m["bq"], m["bk"], m["bv"], m["misc"], m["bog"], wg, bg, ng)


def _outproj_kernel(x_ref, a_ref, b_ref, lng_ref, lnb_ref, wo_ref, g1_ref, b1_ref, out_ref):
    h = _layer_norm(x_ref[0], lng_ref[...], lnb_ref[...])
    mixed = _dot(a_ref[0], wo_ref[0:A_WIDTH, :]) + _dot(b_ref[0], wo_ref[A_WIDTH:, :])
    out_ref[0] = _layer_norm(ALPHA * h + mixed, g1_ref[...], b1_ref[...])


def _outproj(x, a, b, lng, lnb, wo, g1, b1, tm):
    nb, s, d = x.shape
    row = lambda c: pl.BlockSpec((1, tm, c), lambda bb, i: (bb, i, 0))
    const = lambda shp: pl.BlockSpec(shp, lambda bb, i: (0,) * len(shp))
    return pl.pallas_call(
        _outproj_kernel,
        grid=(nb, s // tm),
        in_specs=[row(d), row(512), row(512), const((1, d)), const((1, d)), const((d, d)),
                  const((1, d)), const((1, d))],
        out_specs=row(d),
        out_shape=jax.ShapeDtypeStruct((nb, s, d), F32),
        compiler_params=pltpu.CompilerParams(
            dimension_semantics=("arbitrary", "arbitrary"), vmem_limit_bytes=VMEM_LIMIT),
        name="outproj",
    )(x, a, b, lng, lnb, wo, g1, b1)


def _gelu_tanh(x):
    c = 0.7978845608028654
    return 0.5 * x * (1.0 + jnp.tanh(c * (x + 0.044715 * (x * x * x))))


def _ffn_kernel(x_ref, xh_ref, a_ref, ah_ref, b_ref, bh_ref, hm_ref, lng_ref, lnb_ref, wo_ref, g1_ref, b1_ref,
                wup_ref, cw_ref, cb_ref, wdn_ref, g2_ref, b2_ref, out_ref, y_s):
    j = pl.program_id(1)
    def mix(x, a_mix, b_mix):
        mixed = _dot(a_mix, wo_ref[0:A_WIDTH, :]) + _dot(b_mix, wo_ref[A_WIDTH:, :])
        return _layer_norm(ALPHA * _layer_norm(x, lng_ref[...], lnb_ref[...]) + mixed, g1_ref[...], b1_ref[...])

    half = x_ref.shape[1] // 2
    h1_lo = mix(jnp.concatenate([xh_ref[0], x_ref[0, 0:half, :]], axis=0),
                jnp.concatenate([ah_ref[0], a_ref[0, 0:half, :]], axis=0),
                jnp.concatenate([bh_ref[0], b_ref[0, 0:half, :]], axis=0))
    h1_hi = mix(x_ref[0, half:, :], a_ref[0, half:, :], b_ref[0, half:, :])
    h = jnp.concatenate([h1_lo[HALO:, :], h1_hi], axis=0)
    halo = jnp.where(j == 0, hm_ref[...], h1_lo[0:HALO, :])
    hb = jnp.concatenate([halo, h], axis=0).astype(BF16)
    for f in range(D_FF // FF_TILE):
        fs = slice(f * FF_TILE, (f + 1) * FF_TILE)
        a = _dot(hb, wup_ref[:, fs])
        gate = _dot(hb[HALO:, :], wup_ref[:, D_FF + f * FF_TILE:D_FF + (f + 1) * FF_TILE])
        cw = cw_ref[:, fs]
        conv = cb_ref[:, fs] + cw[CONV_W - 1:CONV_W, :] * a[HALO:, :]
        for back in range(1, CONV_W):
            tap = cw[CONV_W - 1 - back:CONV_W - back, :]
            conv = conv + tap * pltpu.roll(a, back, 0)[HALO:, :]
        y_s[:, fs] = (_gelu_tanh(conv) * gate).astype(BF16)
    ffn = _dot(y_s[...], wdn_ref[...])
    out_ref[0] = _layer_norm(ALPHA * h + ffn, g2_ref[...], b2_ref[...])


def _ffn(x, a, b, h1_meta, lng, lnb, wo, g1, b1, wup, cw, cb, wdn, g2, b2, tm):
    nb, s, d = x.shape
    per = tm // HALO
    const = lambda shp: pl.BlockSpec(shp, lambda bb, j: (0,) * len(shp), pipeline_mode=pl.Buffered(1))
    main = lambda c: pl.BlockSpec((1, tm, c), lambda bb, j: (bb, j, 0))
    halo = lambda c: pl.BlockSpec((1, HALO, c), lambda bb, j: (bb, jnp.maximum(j * per - 1, 0), 0))
    return pl.pallas_call(
        _ffn_kernel,
        grid=(nb, s // tm),
        in_specs=[main(d), halo(d), main(A_WIDTH), halo(A_WIDTH), main(B_WIDTH), halo(B_WIDTH),
                  const((HALO, d)), const((1, d)), const((1, d)), const((d, d)), const((1, d)), const((1, d)),
                  const((d, 2 * D_FF)), const((CONV_W, D_FF)), const((1, D_FF)),
                  const((D_FF, d)), const((1, d)), const((1, d))],
        out_specs=main(d),
        out_shape=jax.ShapeDtypeStruct((nb, s, d), F32),
        scratch_shapes=[pltpu.VMEM((tm, D_FF), BF16)],
        compiler_params=pltpu.CompilerParams(
            dimension_semantics=("arbitrary", "arbitrary"), vmem_limit_bytes=VMEM_LIMIT),
        name="ffn",
    )(x, x, a, a, b, b, h1_meta, lng, lnb, wo, g1, b1, wup, cw, cb, wdn, g2, b2)


def _permute_w_in(w):
    o = 0
    parts = {}
    for name, n in (("a_q", 512), ("a_k", 512), ("a_v", 512), ("i_q", 256), ("i_k", 64), ("i_w", 4),
                    ("b_q", 256), ("b_k", 256), ("b_v", 512), ("b_lr", 16), ("b_og", 512)):
        parts[name] = w[:, o:o + n]
        o += n
    zeros = jnp.zeros((w.shape[0], MISC_IK - B_GATE_RANK - IDX_HEADS), w.dtype)
    cols = [parts["a_q"] * (A_HEAD_DIM ** -0.5 * LOG2E), parts["a_k"], parts["a_v"], parts["i_q"],
            parts["b_q"] * (B_KEY_DIM ** -0.5), parts["b_k"], parts["b_v"], parts["b_og"],
            parts["b_lr"], parts["i_w"] * (IDX_HEADS ** -0.5) * (IDX_DIM ** -0.5), zeros, parts["i_k"]]
    return jnp.concatenate(cols, axis=1).astype(BF16)


def kernel(x, meta, ln_in_g, ln_in_b, w_in, w_gate_b, b_gate_b, attn_norm_g, gla_norm_g, w_out,
           ln1_g, ln1_b, w_up, conv_w, conv_b, w_down, ln2_g, ln2_b):
    nb, s, d = x.shape
    assert d == D_MODEL and w_in.shape[0] == DEPTH == 1
    assert s % ROW_TILE == 0 and s % KT == 0 and s % GLA_BLOCK == 0
    topk = min(TOPK_MAX, s // 4)
    l = 0
    r2 = lambda v: v.reshape(1, -1)
    lng, lnb = r2(ln_in_g), r2(ln_in_b)
    w_perm = _permute_w_in(w_in[l])
    wo = w_out[l].astype(BF16)
    wup = w_up[l].astype(BF16)
    wdn = w_down[l].astype(BF16)
    wg = w_gate_b[l].astype(BF16)
    bg = r2(b_gate_b[l])
    ng_a = r2(attn_norm_g[l])
    ng_b = r2(gla_norm_g[l])

    x_meta = jnp.zeros((1, META_ROWS, d), x.dtype).at[0, :N_META].set(meta.astype(x.dtype))
    names = ("aq", "akx", "avT", "iq", "bq", "bk", "bv", "bog", "ik2", "misc", "miscT")
    m = {n: v[0] for n, v in zip(names, _inproj(x_meta, lng, lnb, w_perm, META_ROWS, 0, valid_rows=N_META))}

    aq, akx, avT, iq, bq, bk, bv, bog, ik2, misc, miscT = _inproj(x, lng, lnb, w_perm, ROW_TILE, N_META)
    a_out = _dsa(aq, iq, miscT, akx, avT, ik2, m, ng_a, topk, False)
    a_out_m = _dsa(m["aq"][None], m["iq"][None], m["miscT"][None], m["akx"][None], m["avT"][None],
                   m["ik2"][None], m, ng_a, topk, True)[0]
    b_out, b_out_m = _gla(bq, bk, bv, misc, bog, m, wg, bg, ng_b)

    g1, b1 = r2(ln1_g[l]), r2(ln1_b[l])
    h1_m = _outproj(x_meta, a_out_m[None], b_out_m[None], lng, lnb, wo, g1, b1, META_ROWS)
    return _ffn(x, a_out, b_out, h1_m[0, N_META - HALO:N_META], lng, lnb, wo, g1, b1,
                wup, conv_w[l], r2(conv_b[l]), wdn, r2(ln2_g[l]), r2(ln2_b[l]), ROW_TILE)
```

```python
import functools

import jax
import jax.numpy as jnp
from jax import lax
from jax.experimental import pallas as pl
from jax.experimental.pallas import tpu as pltpu

F32 = jnp.float32
BF16 = jnp.bfloat16
I32 = jnp.int32

D_MODEL = 1024
CHUNK = 64
N_META = 16
A_HEADS = 8
A_WIDTH = 512
A_HEAD_DIM = 64
IDX_HEADS = 4
IDX_DIM = 64
TOPK_MAX = 256
B_HEADS = 4
B_WIDTH = 512
B_VAL_DIM = 128
B_KEY_DIM = 64
B_GATE_RANK = 16
B_GATE_TAU = 16.0
D_FF = 2816
CONV_W = 3
LN_EPS = 1e-5
RMS_EPS = 1e-6
DEPTH = 1
ALPHA = (2.0 * DEPTH) ** 0.25

LANES = 128
ROW_TILE = 512
QB = 256
KT = 256
MT = 64
GROUP_ROWS = MT + KT
META_ROWS = 256
GLA_BLOCK = 512
FF_TILE = 256
HALO = 16
CHUNK_SHIFT = CHUNK.bit_length() - 1
BF16_BITS = 16
BF16_SIGN, BF16_ALL = 0x8000, 0xFFFF
F32_MAGNITUDE = 0x7FFFFFFF
FINE_BITS = 17
U_NEG_INF, U_POS_INF = 0x007F, 0xFF80
N_SLOPE_PARTS = 3
VT_ROWS = 80
KX_COLS = 2 * A_WIDTH
LOG2E = 1.4426950408889634
ALIBI_C = tuple(2.0 ** (-8.0 * (h + 1) / A_HEADS) * LOG2E for h in range(A_HEADS))
VMEM_LIMIT = 56 * 1024 * 1024

_COLS = {}
_off = 0
for _name, _w in (("aq", 512), ("ak", 512), ("av", 512), ("iq", 256), ("bq", 256), ("bk", 256),
                  ("bv", 512), ("bog", 512), ("misc", 128)):
    _COLS[_name] = (_off, _w)
    _off += _w
PROJ_COLS = _off
MISC_LR = 0
MISC_IW = 16
MISC_IK = 64


def _layer_norm(x, g, b):
    mu = jnp.mean(x, axis=-1, keepdims=True)
    xc = x - mu
    var = jnp.mean(xc * xc, axis=-1, keepdims=True)
    return xc * lax.rsqrt(var + LN_EPS) * g + b


def _dot(a, b):
    return jnp.dot(a, b, preferred_element_type=F32)


def _dot_nt(a, b):
    return lax.dot_general(a, b, (((1,), (1,)), ((), ())), preferred_element_type=F32)


def _dot_tn(a, b):
    return lax.dot_general(a, b, (((0,), (0,)), ((), ())), preferred_element_type=F32)


def _inproj_kernel(x_ref, g_ref, b_ref, w_ref, aq_ref, akx_ref, avT_ref, iq_ref, bq_ref, bk_ref,
                   bv_ref, bog_ref, ik2_ref, misc_ref, miscT_ref, *, pos0, valid_rows):
    h = _layer_norm(x_ref[0], g_ref[...], b_ref[...])
    if valid_rows is not None:
        h = jnp.where(lax.broadcasted_iota(I32, h.shape, 0) < valid_rows, h, 0.0)
    hb = h.astype(BF16)
    tm = hb.shape[0]

    def proj(name):
        lo, w = _COLS[name]
        return _dot(hb, w_ref[:, lo:lo + w])

    aq_ref[0] = proj("aq").astype(BF16)
    ak = proj("ak")
    rows = lax.broadcasted_iota(I32, (tm, LANES), 0)
    lanes = lax.broadcasted_iota(I32, (tm, LANES), 1)
    pos = pos0 + pl.program_id(1) * tm + rows
    feat = jnp.where(lanes < 2 * N_SLOPE_PARTS, jnp.where(lanes % 2 == 0, pos >> CHUNK_SHIFT, pos & (CHUNK - 1)), 0)
    feat = feat.astype(F32).astype(BF16)
    for p in range(A_HEADS // 2):
        akx_ref[0, :, 2 * p * LANES:(2 * p + 1) * LANES] = ak[:, p * LANES:(p + 1) * LANES].astype(BF16)
        akx_ref[0, :, (2 * p + 1) * LANES:(2 * p + 2) * LANES] = feat
    avT = proj("av").T
    ones = jnp.ones((VT_ROWS - A_HEAD_DIM, KT), BF16)
    for i in range(avT_ref.shape[1]):
        for hd in range(A_HEADS):
            avT_ref[0, i, hd * VT_ROWS:hd * VT_ROWS + A_HEAD_DIM, :] = (
                avT[hd * A_HEAD_DIM:(hd + 1) * A_HEAD_DIM, i * KT:(i + 1) * KT].astype(BF16))
            avT_ref[0, i, hd * VT_ROWS + A_HEAD_DIM:(hd + 1) * VT_ROWS, :] = ones
    iq_ref[0] = proj("iq").astype(BF16)
    bq_ref[0] = proj("bq").astype(BF16)
    bk_ref[0] = proj("bk").astype(BF16)
    bv_ref[0] = proj("bv").astype(BF16)
    bog_ref[0] = proj("bog")
    misc = proj("misc")
    misc_ref[0] = misc
    miscT_ref[0] = misc.T
    ik_lo = pltpu.roll(misc, LANES - MISC_IK, 1)
    ik2_ref[0] = jnp.where(lanes < MISC_IK, ik_lo, misc).astype(BF16)


def _inproj(x, ln_g, ln_b, w_perm, tm, pos0, valid_rows=None):
    nb, s, d = x.shape
    nt = s // tm
    row = lambda c: pl.BlockSpec((1, tm, c), lambda b, i: (b, i, 0))
    const = lambda shp: pl.BlockSpec(shp, lambda b, i: (0,) * len(shp))
    out_shape = (
        jax.ShapeDtypeStruct((nb, s, 512), BF16),
        jax.ShapeDtypeStruct((nb, s, KX_COLS), BF16),
        jax.ShapeDtypeStruct((nb, s // KT, A_HEADS * VT_ROWS, KT), BF16),
        jax.ShapeDtypeStruct((nb, s, 256), BF16),
        jax.ShapeDtypeStruct((nb, s, 256), BF16),
        jax.ShapeDtypeStruct((nb, s, 256), BF16),
        jax.ShapeDtypeStruct((nb, s, 512), BF16),
        jax.ShapeDtypeStruct((nb, s, 512), F32),
        jax.ShapeDtypeStruct((nb, s, 128), BF16),
        jax.ShapeDtypeStruct((nb, s, 128), F32),
        jax.ShapeDtypeStruct((nb, 128, s), F32),
    )
    out_specs = (
        row(512), row(KX_COLS),
        pl.BlockSpec((1, tm // KT, A_HEADS * VT_ROWS, KT), lambda b, i: (b, i, 0, 0)),
        row(256), row(256), row(256), row(512), row(512), row(128), row(128),
        pl.BlockSpec((1, 128, tm), lambda b, i: (b, 0, i)),
    )
    return pl.pallas_call(
        functools.partial(_inproj_kernel, pos0=pos0, valid_rows=valid_rows),
        grid=(nb, nt),
        in_specs=[row(d), const((1, d)), const((1, d)), const((d, PROJ_COLS))],
        out_specs=out_specs,
        out_shape=out_shape,
        compiler_params=pltpu.CompilerParams(
            dimension_semantics=("arbitrary", "arbitrary"), vmem_limit_bytes=VMEM_LIMIT),
        name="inproj",
    )(x, ln_g, ln_b, w_perm)


def _static_when(cond: bool):
    def deco(fn):
        if cond:
            fn()
    return deco


def _dsa_kernel(aq_ref, iq_ref, miscT_ref, akx_ref, avT_ref, ik2_ref,
                maq_ref, miq_ref, mmiscT_ref, makx_ref, mavT_ref, mik2_ref, qfeat_ref, ng_ref,
                out_ref,
                sc_s, scb_s, qm_s, iqm_s, iw_s, m_s, acc_s, bias_s, lg_s, p_s, neq_s, shift_s, corr_s,
                *, topk, is_meta):
    g = 0 if is_meta else pl.program_id(1) + 1
    n_past = 0 if is_meta else g - 1
    when_real = _static_when(not is_meta)
    when_meta = _static_when(is_meta)

    lane_half = lax.broadcasted_iota(I32, (1, LANES), 1) // A_HEAD_DIM

    def stage(aq, iq, miscT):
        for h in range(A_HEADS):
            p = h // 2
            qp = aq[:, p * LANES:(p + 1) * LANES]
            qm_s[h, :, 0:LANES] = jnp.where(lane_half == (h % 2), qp, jnp.zeros_like(qp))
            qm_s[h, :, LANES:2 * LANES] = jnp.broadcast_to(qfeat_ref[h:h + 1, :], (QB, LANES)).astype(BF16)
        for h in range(IDX_HEADS):
            p = h // 2
            qp = iq[:, p * LANES:(p + 1) * LANES]
            iqm_s[h] = jnp.where(lane_half == (h % 2), qp, jnp.zeros_like(qp))
        iw_s[...] = miscT[MISC_IW:MISC_IW + 8, :]

    lane = lax.broadcasted_iota(I32, (1, QB), 1)
    if is_meta:
        stage(maq_ref[...], miq_ref[...], mmiscT_ref[...])
        qpos = lane
    else:
        stage(aq_ref[0], iq_ref[0], miscT_ref[0])
        qpos = N_META + (g - 1) * QB + lane

    ninf = jnp.float32(-jnp.inf)

    def scores_for_tile(ik2_t, allowed):
        s = None
        for h in range(IDX_HEADS):
            sh = _dot_nt(ik2_t, iqm_s[h])
            term = iw_s[h:h + 1, :] * jnp.maximum(sh, 0.0)
            s = term if s is None else s + term
        return s if allowed is None else jnp.where(allowed, s, ninf)

    def store_scores(slot, r, s):
        sc_s[slot, 0:r, :] = s
        scb_s[slot, 0:r, :] = s.astype(BF16)

    mrow = lax.broadcasted_iota(I32, (MT, QB), 0)
    krow = lax.broadcasted_iota(I32, (KT, QB), 0)
    store_scores(0, MT, scores_for_tile(mik2_ref[0:MT, :], mrow < N_META))

    def fill_past(kt):
        store_scores(kt + 1, KT, scores_for_tile(ik2_ref[0, kt], None))

    def fill_pair(i, c):
        fill_past(2 * i)
        fill_past(2 * i + 1)
        return c

    @when_real
    def _():
        lax.fori_loop(0, n_past // 2, fill_pair, 0)

        @pl.when(n_past % 2 == 1)
        def _():
            fill_past(n_past - 1)

        store_scores(g, KT, scores_for_tile(ik2_ref[0, g - 1], (krow >> CHUNK_SHIFT) <= (lane >> CHUNK_SHIFT)))

    def count(arr_s, pred, pack):
        acc_t = jnp.int16 if pack == 16 else I32

        def cnt(tile):
            x = jnp.where(pred(tile), jnp.ones((), acc_t), jnp.zeros((), acc_t))
            x = x.reshape(tile.shape[0] // pack, pack, QB)
            parts = [x[j] for j in range(x.shape[0])]
            while len(parts) > 1:
                parts = [parts[j] + parts[j + 1] for j in range(0, len(parts), 2)]
            return parts[0]

        acc = cnt(arr_s[0, 0:MT, :])
        acc = lax.fori_loop(0, g, lambda kt, a: a + cnt(arr_s[kt + 1]), acc)
        return jnp.sum(acc.astype(I32), axis=0, keepdims=True)

    def ordered16_to_bits(k):
        return jnp.where(k >= BF16_SIGN, k ^ BF16_SIGN, k ^ BF16_ALL)

    def coarse_body(i, u):
        cand_u = u | jnp.left_shift(jnp.int32(1), BF16_BITS - 1 - i)
        finite = (cand_u >= U_NEG_INF) & (cand_u <= U_POS_INF)
        bits = ordered16_to_bits(jnp.clip(cand_u, U_NEG_INF, U_POS_INF))
        cand = lax.bitcast_convert_type(jnp.left_shift(bits, BF16_BITS), F32).astype(BF16)
        cnt = count(scb_s, lambda tile: tile >= cand, 16)
        return jnp.where(finite & (cnt >= topk), cand_u, u)

    u1 = lax.fori_loop(0, BF16_BITS, coarse_body, jnp.zeros((1, QB), I32))
    few = u1 == 0
    t1_bits = jnp.left_shift(ordered16_to_bits(jnp.where(few, BF16_SIGN, u1)), BF16_BITS)
    base = (t1_bits ^ ((t1_bits >> 31) & F32_MAGNITUDE)) - BF16_SIGN

    def key_to_f32(k):
        return lax.bitcast_convert_type(k ^ ((k >> 31) & F32_MAGNITUDE), F32)

    def fine_body(i, o):
        cand_o = o | jnp.left_shift(jnp.int32(1), FINE_BITS - 1 - i)
        cand = key_to_f32(base + cand_o)
        cnt = count(sc_s, lambda tile: tile >= cand, 8)
        return jnp.where(cnt >= topk, cand_o, o)

    o2 = lax.fori_loop(0, FINE_BITS, fine_body, jnp.zeros((1, QB), I32))
    thr = jnp.where(few, ninf, key_to_f32(base + o2))
    n_gt = count(sc_s, lambda tile: tile > thr, 8)
    need = jnp.where(few, 0, topk - n_gt).astype(F32)

    m_s[...] = jnp.full(m_s.shape, -jnp.inf, F32)
    acc_s[...] = jnp.zeros(acc_s.shape, F32)

    def offsets(subs):
        offs, row0 = [], 0
        for sub in subs:
            offs.append((row0, sub[0].shape[0]))
            row0 += sub[0].shape[0]
        return offs, row0

    def bias_stage(subs):
        offs, _ = offsets(subs)
        n_eq_after = None
        for (sc_t, _, _, _, n_eq_before), (o, r) in zip(subs, offs):
            if n_eq_before is None:
                n_eq_before = n_eq_after
            ri = lax.broadcasted_iota(I32, (r, r), 0)
            ci = lax.broadcasted_iota(I32, (r, r), 1)
            lower = jnp.where(ri > ci, 1.0, 0.0).astype(BF16)
            eq = sc_t == thr
            rank = _dot(lower, jnp.where(eq, 1.0, 0.0).astype(BF16)) + n_eq_before
            bias_s[o:o + r, :] = jnp.where(
                sc_t > thr, 0.0, jnp.where(eq, jnp.where(rank < need, 0.0, ninf), ninf))
            n_eq_after = rank[r - 1:r, :] + jnp.where(eq[r - 1:r, :], 1.0, 0.0)
        return n_eq_after

    def logits_head(subs, h):
        offs, _ = offsets(subs)
        m_old = m_s[h]
        m_new = m_old
        for (_, kx_fn, _, after, _), (o, r) in zip(subs, offs):
            lg = _dot_nt(kx_fn(h // 2), qm_s[h]) + bias_s[o:o + r, :]
            if after is not None:
                lg = lg + ALIBI_C[h] * after
            lg_s[h, o:o + r, :] = lg
            m_new = jnp.maximum(m_new, jnp.max(lg, axis=0, keepdims=True))
        m_s[h] = m_new
        m_safe = jnp.where(m_new == ninf, 0.0, m_new)
        shift_s[h] = m_safe
        corr_s[h] = jnp.exp2(m_old - m_safe)

    def softmax_pv_head(subs, h):
        offs, rows = offsets(subs)
        p_s[h, 0:rows, :] = jnp.exp2(lg_s[h, 0:rows, :] - shift_s[h]).astype(BF16)
        acc = acc_s[h] * corr_s[h]
        for (_, _, vT_fn, _, _), (o, r) in zip(subs, offs):
            acc = acc + _dot(vT_fn(h), p_s[h, o:o + r, :])
        acc_s[h] = acc

    def run_stages(done, nxt):
        n_eq_after = bias_stage(nxt) if nxt is not None else None
        for h in range(A_HEADS):
            if done is not None:
                softmax_pv_head(done, h)
            if nxt is not None:
                logits_head(nxt, h)
        return n_eq_after

    def after_term(kpos):
        return jnp.minimum(2 * (qpos - kpos), 0).astype(F32)

    def meta_sub():
        return (sc_s[0, 0:MT, :],
                lambda p: makx_ref[0:MT, 2 * p * LANES:(2 * p + 2) * LANES],
                lambda h: mavT_ref[0, h * VT_ROWS:(h + 1) * VT_ROWS, 0:MT],
                after_term(mrow), jnp.zeros((1, QB), F32))

    def real_sub(kt, after, n_eq_before):
        return (sc_s[kt + 1],
                lambda p: akx_ref[0, kt, :, 2 * p * LANES:(2 * p + 2) * LANES],
                lambda h: avT_ref[0, kt, h * VT_ROWS:(h + 1) * VT_ROWS, :],
                after, n_eq_before)

    neq_s[...] = jnp.sum(jnp.where(sc_s[0, 0:MT, :] == thr, 1.0, 0.0), axis=0, keepdims=True)

    def past(kt):
        return [real_sub(kt, None, neq_s[...])]

    def last_group():
        return [meta_sub(), real_sub(g - 1, after_term(N_META + (g - 1) * KT + krow), neq_s[...])]

    @when_real
    def _():
        @pl.when(n_past > 0)
        def _():
            neq_s[...] = run_stages(None, past(0))

        def pipe_body(i, c):
            neq_s[...] = run_stages(past(i), past(i + 1))
            return c

        lax.fori_loop(0, jnp.maximum(n_past - 1, 0), pipe_body, 0)

        @pl.when(n_past > 0)
        def _():
            run_stages(past(n_past - 1), last_group())

        @pl.when(n_past == 0)
        def _():
            run_stages(None, last_group())

        run_stages(last_group(), None)

    @when_meta
    def _():
        run_stages(None, [meta_sub()])
        run_stages([meta_sub()], None)

    outs = []
    for h in range(A_HEADS):
        o = acc_s[h, 0:A_HEAD_DIM, :] / acc_s[h, A_HEAD_DIM:A_HEAD_DIM + 1, :]
        ms = jnp.mean(o * o, axis=0, keepdims=True)
        outs.append(o * lax.rsqrt(ms + RMS_EPS))
    res = (jnp.concatenate(outs, axis=0).T * ng_ref[...]).astype(BF16)

    out_ref[0] = res


def _alibi_query_features():
    rows = []
    for c in ALIBI_C:
        rest = jnp.float32(c)
        lanes = []
        for _ in range(N_SLOPE_PARTS):
            part = rest.astype(BF16).astype(F32)
            lanes += [part * CHUNK, part]
            rest = rest - part
        rows.append(jnp.stack(lanes + [jnp.float32(0.0)] * (LANES - len(lanes))))
    return jnp.stack(rows)


def _dsa(aq, iq, miscT, akx, avT, ik2, m, ng, topk, is_meta):
    nb, s, _ = aq.shape
    nq = s // QB
    nkt = s // KT
    vt = A_HEADS * VT_ROWS
    qidx = lambda b, i: (b, i, 0)
    const = lambda shp: pl.BlockSpec(shp, lambda b, i: (0,) * len(shp))
    in_specs = [
        pl.BlockSpec((1, QB, 512), qidx),
        pl.BlockSpec((1, QB, 256), qidx),
        pl.BlockSpec((1, 128, QB), lambda b, i: (b, 0, i)),
        pl.BlockSpec((1, nkt, KT, KX_COLS), lambda b, i: (b, 0, 0, 0)),
        pl.BlockSpec((1, nkt, vt, KT), lambda b, i: (b, 0, 0, 0)),
        pl.BlockSpec((1, nkt, KT, 128), lambda b, i: (b, 0, 0, 0)),
        const((META_ROWS, 512)), const((META_ROWS, 256)), const((128, META_ROWS)),
        const((META_ROWS, KX_COLS)), const((1, vt, KT)), const((META_ROWS, 128)),
        const((A_HEADS, LANES)), const((1, 512)),
    ]
    out_shape = jax.ShapeDtypeStruct((nb, s, 512), BF16)
    out_specs = pl.BlockSpec((1, QB, 512), qidx)
    scratch = [
        pltpu.VMEM((nkt + 1, KT, QB), F32),
        pltpu.VMEM((nkt + 1, KT, QB), BF16),
        pltpu.VMEM((A_HEADS, QB, 2 * LANES), BF16),
        pltpu.VMEM((IDX_HEADS, QB, LANES), BF16),
        pltpu.VMEM((8, QB), F32),
        pltpu.VMEM((A_HEADS, 1, QB), F32),
        pltpu.VMEM((A_HEADS, VT_ROWS, QB), F32),
        pltpu.VMEM((GROUP_ROWS, QB), F32),
        pltpu.VMEM((A_HEADS, GROUP_ROWS, QB), F32),
        pltpu.VMEM((A_HEADS, GROUP_ROWS, QB), BF16),
        pltpu.VMEM((1, QB), F32),
        pltpu.VMEM((A_HEADS, 1, QB), F32),
        pltpu.VMEM((A_HEADS, 1, QB), F32),
    ]
    return pl.pallas_call(
        functools.partial(_dsa_kernel, topk=topk, is_meta=is_meta),
        grid=(nb, nq),
        in_specs=in_specs,
        out_specs=out_specs,
        out_shape=out_shape,
        scratch_shapes=scratch,
        compiler_params=pltpu.CompilerParams(
            dimension_semantics=("arbitrary", "arbitrary"), vmem_limit_bytes=VMEM_LIMIT),
        name="dsa",
    )(aq, iq, miscT, akx.reshape(nb, nkt, KT, KX_COLS), avT, ik2.reshape(nb, nkt, KT, 128),
      m["aq"], m["iq"], m["miscT"], m["akx"], m["avT"], m["ik2"], _alibi_query_features(), ng)


def _chunk_cumsum_rows(x):
    row_in_chunk = lax.broadcasted_iota(I32, x.shape, 0) & (CHUNK - 1)
    sh = 1
    while sh < CHUNK:
        x = x + jnp.where(row_in_chunk >= sh, pltpu.roll(x, sh, 0), 0.0)
        sh *= 2
    return x


def _rows_of_chunk(rows_per_chunk):
    return jnp.concatenate([jnp.broadcast_to(r, (CHUNK, r.shape[1])) for r in rows_per_chunk], axis=0)


def _gla_block(q, k, v, lr, og, wg, bg, ng, st_s, valid_rows):
    nrow = q.shape[0]
    nch = nrow // CHUNK
    z = _dot(lr.astype(BF16), wg) + bg
    logg = (jnp.minimum(z, 0.0) - jnp.log(1.0 + jnp.exp(-jnp.abs(z)))) / B_GATE_TAU
    if valid_rows is not None:
        rows = lax.broadcasted_iota(I32, logg.shape, 0)
        logg = jnp.where(rows < valid_rows, logg, 0.0)
    b = _chunk_cumsum_rows(logg)
    b_last_rows = [b[(c + 1) * CHUNK - 1:(c + 1) * CHUNK, :] for c in range(nch)]
    b_mid = _rows_of_chunk([b[c * CHUNK + CHUNK // 2 - 1:c * CHUNK + CHUNK // 2, :] for c in range(nch)])
    b_last = _rows_of_chunk(b_last_rows)
    q_in = q * jnp.exp(b)
    q_ic = q * jnp.exp(b - b_mid)
    k_ic = k * jnp.exp(b_mid - b)
    k_st = k * jnp.exp(b_last - b)
    decay = [jnp.exp(r) for r in b_last_rows]

    lane_half = lax.broadcasted_iota(I32, (1, LANES), 1) // B_KEY_DIM
    ri = lax.broadcasted_iota(I32, (nrow, nrow), 0)
    ci = lax.broadcasted_iota(I32, (nrow, nrow), 1)
    causal = ((ri >> CHUNK_SHIFT) == (ci >> CHUNK_SHIFT)) & (ci <= ri)
    outs = []
    for p in range(B_HEADS // 2):
        sl = slice(p * LANES, (p + 1) * LANES)
        heads = (2 * p, 2 * p + 1)
        v_h = [v[:, h * B_VAL_DIM:(h + 1) * B_VAL_DIM] for h in heads]
        k_st_h = [jnp.where(lane_half == hh, k_st[:, sl], 0.0).astype(BF16) for hh in range(2)]
        states = [st_s[p]]
        for c in range(nch):
            rs = slice(c * CHUNK, (c + 1) * CHUNK)
            upd = _dot_tn(v_h[0][rs], k_st_h[0][rs]) + _dot_tn(v_h[1][rs], k_st_h[1][rs])
            states.append(states[c] * decay[c][:, sl] + upd)
        st_s[p] = states[nch]
        k_ic_p = k_ic[:, sl].astype(BF16)
        for hh, h in enumerate(heads):
            hm = lane_half == hh
            a = _dot_nt(jnp.where(hm, q_ic[:, sl], 0.0).astype(BF16), k_ic_p)
            o = _dot(jnp.where(causal, a, 0.0).astype(BF16), v_h[hh])
            q_in_h = jnp.where(hm, q_in[:, sl], 0.0).astype(BF16)
            o = o + jnp.concatenate(
                [_dot_nt(q_in_h[c * CHUNK:(c + 1) * CHUNK], states[c].astype(BF16)) for c in range(nch)], axis=0)
            ms = jnp.mean(o * o, axis=-1, keepdims=True)
            on = o * lax.rsqrt(ms + RMS_EPS) * ng[:, h * B_VAL_DIM:(h + 1) * B_VAL_DIM]
            og_h = og[:, h * B_VAL_DIM:(h + 1) * B_VAL_DIM]
            outs.append(on * (og_h * jax.nn.sigmoid(og_h)))
    return outs


def _gla_kernel(bq_ref, bk_ref, bv_ref, misc_ref, bog_ref,
                mbq_ref, mbk_ref, mbv_ref, mmisc_ref, mbog_ref,
                wg_ref, bg_ref, ng_ref, out_ref, mout_ref, st_s):
    j = pl.program_id(1)
    wg = wg_ref[...]
    bg = bg_ref[...]
    ng = ng_ref[...]

    @pl.when(j == 0)
    def _():
        st_s[...] = jnp.zeros(st_s.shape, F32)
        outs = _gla_block(mbq_ref[0:CHUNK, :].astype(F32), mbk_ref[0:CHUNK, :].astype(F32),
                          mbv_ref[0:CHUNK, :], mmisc_ref[0:CHUNK, MISC_LR:MISC_LR + B_GATE_RANK],
                          mbog_ref[0:CHUNK, :], wg, bg, ng, st_s, N_META)
        mout_ref[CHUNK:, :] = jnp.zeros((META_ROWS - CHUNK, B_WIDTH), BF16)
        for h, o in enumerate(outs):
            mout_ref[0:CHUNK, h * B_VAL_DIM:(h + 1) * B_VAL_DIM] = o.astype(BF16)

    outs = _gla_block(bq_ref[0].astype(F32), bk_ref[0].astype(F32), bv_ref[0],
                      misc_ref[0, :, MISC_LR:MISC_LR + B_GATE_RANK], bog_ref[0], wg, bg, ng, st_s, None)
    for h, o in enumerate(outs):
        out_ref[0, :, h * B_VAL_DIM:(h + 1) * B_VAL_DIM] = o.astype(BF16)


def _gla(bq, bk, bv, misc, bog, m, wg, bg, ng):
    nb, s, _ = bq.shape
    row = lambda c: pl.BlockSpec((1, GLA_BLOCK, c), lambda b, j: (b, j, 0))
    const = lambda shp: pl.BlockSpec(shp, lambda b, j: (0,) * len(shp))
    return pl.pallas_call(
        _gla_kernel,
        grid=(nb, s // GLA_BLOCK),
        in_specs=[row(256), row(256), row(512), row(128), row(512),
                  const((META_ROWS, 256)), const((META_ROWS, 256)), const((META_ROWS, 512)),
                  const((META_ROWS, 128)), const((META_ROWS, 512)),
                  const((B_GATE_RANK, 256)), const((1, 256)), const((1, 512))],
        out_specs=(row(512), const((META_ROWS, 512))),
        out_shape=(jax.ShapeDtypeStruct((nb, s, 512), BF16), jax.ShapeDtypeStruct((META_ROWS, 512), BF16)),
        scratch_shapes=[pltpu.VMEM((B_HEADS // 2, LANES, LANES), F32)],
        compiler_params=pltpu.CompilerParams(
            dimension_semantics=("arbitrary", "arbitrary"), vmem_limit_bytes=VMEM_LIMIT),
        name="gla",
    )(bq, bk, bv, misc, bog, m["bq"], m["bk"], m["bv"], m["misc"], m["bog"], wg, bg, ng)


def _outproj_kernel(x_ref, a_ref, b_ref, lng_ref, lnb_ref, wo_ref, g1_ref, b1_ref, out_ref):
    h = _layer_norm(x_ref[0], lng_ref[...], lnb_ref[...])
    mixed = _dot(a_ref[0], wo_ref[0:A_WIDTH, :]) + _dot(b_ref[0], wo_ref[A_WIDTH:, :])
    out_ref[0] = _layer_norm(ALPHA * h + mixed, g1_ref[...], b1_ref[...])


def _outproj(x, a, b, lng, lnb, wo, g1, b1, tm):
    nb, s, d = x.shape
    row = lambda c: pl.BlockSpec((1, tm, c), lambda bb, i: (bb, i, 0))
    const = lambda shp: pl.BlockSpec(shp, lambda bb, i: (0,) * len(shp))
    return pl.pallas_call(
        _outproj_kernel,
        grid=(nb, s // tm),
        in_specs=[row(d), row(512), row(512), const((1, d)), const((1, d)), const((d, d)),
                  const((1, d)), const((1, d))],
        out_specs=row(d),
        out_shape=jax.ShapeDtypeStruct((nb, s, d), F32),
        compiler_params=pltpu.CompilerParams(
            dimension_semantics=("arbitrary", "arbitrary"), vmem_limit_bytes=VMEM_LIMIT),
        name="outproj",
    )(x, a, b, lng, lnb, wo, g1, b1)


def _gelu_tanh(x):
    c = 0.7978845608028654
    return 0.5 * x * (1.0 + jnp.tanh(c * (x + 0.044715 * (x * x * x))))


def _ffn_kernel(x_ref, xh_ref, a_ref, ah_ref, b_ref, bh_ref, hm_ref, lng_ref, lnb_ref, wo_ref, g1_ref, b1_ref,
                wup_ref, cw_ref, cb_ref, wdn_ref, g2_ref, b2_ref, out_ref, y_s):
    j = pl.program_id(1)
    def mix(x, a_mix, b_mix):
        mixed = _dot(a_mix, wo_ref[0:A_WIDTH, :]) + _dot(b_mix, wo_ref[A_WIDTH:, :])
        return _layer_norm(ALPHA * _layer_norm(x, lng_ref[...], lnb_ref[...]) + mixed, g1_ref[...], b1_ref[...])

    half = x_ref.shape[1] // 2
    h1_lo = mix(jnp.concatenate([xh_ref[0], x_ref[0, 0:half, :]], axis=0),
                jnp.concatenate([ah_ref[0], a_ref[0, 0:half, :]], axis=0),
                jnp.concatenate([bh_ref[0], b_ref[0, 0:half, :]], axis=0))
    h1_hi = mix(x_ref[0, half:, :], a_ref[0, half:, :], b_ref[0, half:, :])
    h = jnp.concatenate([h1_lo[HALO:, :], h1_hi], axis=0)
    halo = jnp.where(j == 0, hm_ref[...], h1_lo[0:HALO, :])
    hb = jnp.concatenate([halo, h], axis=0).astype(BF16)
    for f in range(D_FF // FF_TILE):
        fs = slice(f * FF_TILE, (f + 1) * FF_TILE)
        a = _dot(hb, wup_ref[:, fs])
        gate = _dot(hb[HALO:, :], wup_ref[:, D_FF + f * FF_TILE:D_FF + (f + 1) * FF_TILE])
        cw = cw_ref[:, fs]
        conv = cb_ref[:, fs] + cw[CONV_W - 1:CONV_W, :] * a[HALO:, :]
        for back in range(1, CONV_W):
            tap = cw[CONV_W - 1 - back:CONV_W - back, :]
            conv = conv + tap * pltpu.roll(a, back, 0)[HALO:, :]
        y_s[:, fs] = (_gelu_tanh(conv) * gate).astype(BF16)
    ffn = _dot(y_s[...], wdn_ref[...])
    out_ref[0] = _layer_norm(ALPHA * h + ffn, g2_ref[...], b2_ref[...])


def _ffn(x, a, b, h1_meta, lng, lnb, wo, g1, b1, wup, cw, cb, wdn, g2, b2, tm):
    nb, s, d = x.shape
    per = tm // HALO
    const = lambda shp: pl.BlockSpec(shp, lambda bb, j: (0,) * len(shp), pipeline_mode=pl.Buffered(1))
    main = lambda c: pl.BlockSpec((1, tm, c), lambda bb, j: (bb, j, 0))
    halo = lambda c: pl.BlockSpec((1, HALO, c), lambda bb, j: (bb, jnp.maximum(j * per - 1, 0), 0))
    return pl.pallas_call(
        _ffn_kernel,
        grid=(nb, s // tm),
        in_specs=[main(d), halo(d), main(A_WIDTH), halo(A_WIDTH), main(B_WIDTH), halo(B_WIDTH),
                  const((HALO, d)), const((1, d)), const((1, d)), const((d, d)), const((1, d)), const((1, d)),
                  const((d, 2 * D_FF)), const((CONV_W, D_FF)), const((1, D_FF)),
                  const((D_FF, d)), const((1, d)), const((1, d))],
        out_specs=main(d),
        out_shape=jax.ShapeDtypeStruct((nb, s, d), F32),
        scratch_shapes=[pltpu.VMEM((tm, D_FF), BF16)],
        compiler_params=pltpu.CompilerParams(
            dimension_semantics=("arbitrary", "arbitrary"), vmem_limit_bytes=VMEM_LIMIT),
        name="ffn",
    )(x, x, a, a, b, b, h1_meta, lng, lnb, wo, g1, b1, wup, cw, cb, wdn, g2, b2)


def _permute_w_in(w):
    o = 0
    parts = {}
    for name, n in (("a_q", 512), ("a_k", 512), ("a_v", 512), ("i_q", 256), ("i_k", 64), ("i_w", 4),
                    ("b_q", 256), ("b_k", 256), ("b_v", 512), ("b_lr", 16), ("b_og", 512)):
        parts[name] = w[:, o:o + n]
        o += n
    zeros = jnp.zeros((w.shape[0], MISC_IK - B_GATE_RANK - IDX_HEADS), w.dtype)
    cols = [parts["a_q"] * (A_HEAD_DIM ** -0.5 * LOG2E), parts["a_k"], parts["a_v"], parts["i_q"],
            parts["b_q"] * (B_KEY_DIM ** -0.5), parts["b_k"], parts["b_v"], parts["b_og"],
            parts["b_lr"], parts["i_w"] * (IDX_HEADS ** -0.5) * (IDX_DIM ** -0.5), zeros, parts["i_k"]]
    return jnp.concatenate(cols, axis=1).astype(BF16)


def kernel(x, meta, ln_in_g, ln_in_b, w_in, w_gate_b, b_gate_b, attn_norm_g, gla_norm_g, w_out,
           ln1_g, ln1_b, w_up, conv_w, conv_b, w_down, ln2_g, ln2_b):
    nb, s, d = x.shape
    assert d == D_MODEL and w_in.shape[0] == DEPTH == 1
    assert s % ROW_TILE == 0 and s % KT == 0 and s % GLA_BLOCK == 0
    topk = min(TOPK_MAX, s // 4)
    l = 0
    r2 = lambda v: v.reshape(1, -1)
    lng, lnb = r2(ln_in_g), r2(ln_in_b)
    w_perm = _permute_w_in(w_in[l])
    wo = w_out[l].astype(BF16)
    wup = w_up[l].astype(BF16)
    wdn = w_down[l].astype(BF16)
    wg = w_gate_b[l].astype(BF16)
    bg = r2(b_gate_b[l])
    ng_a = r2(attn_norm_g[l])
    ng_b = r2(gla_norm_g[l])

    x_meta = jnp.zeros((1, META_ROWS, d), x.dtype).at[0, :N_META].set(meta.astype(x.dtype))
    names = ("aq", "akx", "avT", "iq", "bq", "bk", "bv", "bog", "ik2", "misc", "miscT")
    m = {n: v[0] for n, v in zip(names, _inproj(x_meta, lng, lnb, w_perm, META_ROWS, 0, valid_rows=N_META))}

    aq, akx, avT, iq, bq, bk, bv, bog, ik2, misc, miscT = _inproj(x, lng, lnb, w_perm, ROW_TILE, N_META)
    a_out = _dsa(aq, iq, miscT, akx, avT, ik2, m, ng_a, topk, False)
    a_out_m = _dsa(m["aq"][None], m["iq"][None], m["miscT"][None], m["akx"][None], m["avT"][None],
                   m["ik2"][None], m, ng_a, topk, True)[0]
    b_out, b_out_m = _gla(bq, bk, bv, misc, bog, m, wg, bg, ng_b)

    g1, b1 = r2(ln1_g[l]), r2(ln1_b[l])
    h1_m = _outproj(x_meta, a_out_m[None], b_out_m[None], lng, lnb, wo, g1, b1, META_ROWS)
    return _ffn(x, a_out, b_out, h1_m[0, N_META - HALO:N_META], lng, lnb, wo, g1, b1,
                wup, conv_w[l], r2(conv_b[l]), wdn, r2(ln2_g[l]), r2(ln2_b[l]), ROW_TILE)
```

```python
import functools

import jax
import jax.numpy as jnp
from jax import lax
from jax.experimental import pallas as pl
from jax.experimental.pallas import tpu as pltpu

F32 = jnp.float32
BF16 = jnp.bfloat16
I32 = jnp.int32

D_MODEL = 1024
CHUNK = 64
N_META = 16
A_HEADS = 8
A_WIDTH = 512
A_HEAD_DIM = 64
IDX_HEADS = 4
IDX_DIM = 64
TOPK_MAX = 256
B_HEADS = 4
B_WIDTH = 512
B_VAL_DIM = 128
B_KEY_DIM = 64
B_GATE_RANK = 16
B_GATE_TAU = 16.0
D_FF = 2816
CONV_W = 3
LN_EPS = 1e-5
RMS_EPS = 1e-6
DEPTH = 1
ALPHA = (2.0 * DEPTH) ** 0.25

LANES = 128
ROW_TILE = 512
INPROJ_TILE = 1024
QB = 256
KT = 256
MT = 64
GROUP_ROWS = MT + KT
META_ROWS = 256
GLA_BLOCK = 512
FF_TILE = 256
HALO = 16
CHUNK_SHIFT = CHUNK.bit_length() - 1
BF16_BITS = 16
BF16_SIGN, BF16_ALL = 0x8000, 0xFFFF
F32_MAGNITUDE = 0x7FFFFFFF
FINE_BITS = 17
U_NEG_INF, U_POS_INF = 0x007F, 0xFF80
N_SLOPE_PARTS = 3
VT_ROWS = 80
KX_COLS = 2 * A_WIDTH
LOG2E = 1.4426950408889634
ALIBI_C = tuple(2.0 ** (-8.0 * (h + 1) / A_HEADS) * LOG2E for h in range(A_HEADS))
VMEM_LIMIT = 56 * 1024 * 1024

_COLS = {}
_off = 0
for _name, _w in (("aq", 512), ("ak", 512), ("av", 512), ("iq", 256), ("bq", 256), ("bk", 256),
                  ("bv", 512), ("bog", 512), ("misc", 128)):
    _COLS[_name] = (_off, _w)
    _off += _w
PROJ_COLS = _off
MISC_LR = 0
MISC_IW = 16
MISC_IK = 64


def _layer_norm(x, g, b):
    mu = jnp.mean(x, axis=-1, keepdims=True)
    xc = x - mu
    var = jnp.mean(xc * xc, axis=-1, keepdims=True)
    return xc * lax.rsqrt(var + LN_EPS) * g + b


def _dot(a, b):
    return jnp.dot(a, b, preferred_element_type=F32)


def _dot_nt(a, b):
    return lax.dot_general(a, b, (((1,), (1,)), ((), ())), preferred_element_type=F32)


def _dot_tn(a, b):
    return lax.dot_general(a, b, (((0,), (0,)), ((), ())), preferred_element_type=F32)


def _inproj_kernel(x_ref, g_ref, b_ref, w_ref, aq_ref, akx_ref, avT_ref, iq_ref, bq_ref, bk_ref,
                   bv_ref, bog_ref, ik2_ref, misc_ref, miscT_ref, *, pos0, valid_rows):
    h = _layer_norm(x_ref[0], g_ref[...], b_ref[...])
    if valid_rows is not None:
        h = jnp.where(lax.broadcasted_iota(I32, h.shape, 0) < valid_rows, h, 0.0)
    hb = h.astype(BF16)
    tm = hb.shape[0]

    def proj(name):
        lo, w = _COLS[name]
        return _dot(hb, w_ref[:, lo:lo + w])

    aq_ref[0] = proj("aq").astype(BF16)
    ak = proj("ak")
    rows = lax.broadcasted_iota(I32, (tm, LANES), 0)
    lanes = lax.broadcasted_iota(I32, (tm, LANES), 1)
    pos = pos0 + pl.program_id(1) * tm + rows
    feat = jnp.where(lanes < 2 * N_SLOPE_PARTS, jnp.where(lanes % 2 == 0, pos >> CHUNK_SHIFT, pos & (CHUNK - 1)), 0)
    feat = feat.astype(F32).astype(BF16)
    for p in range(A_HEADS // 2):
        akx_ref[0, :, 2 * p * LANES:(2 * p + 1) * LANES] = ak[:, p * LANES:(p + 1) * LANES].astype(BF16)
        akx_ref[0, :, (2 * p + 1) * LANES:(2 * p + 2) * LANES] = feat
    avT = proj("av").T
    ones = jnp.ones((VT_ROWS - A_HEAD_DIM, KT), BF16)
    for i in range(avT_ref.shape[1]):
        for hd in range(A_HEADS):
            avT_ref[0, i, hd * VT_ROWS:hd * VT_ROWS + A_HEAD_DIM, :] = (
                avT[hd * A_HEAD_DIM:(hd + 1) * A_HEAD_DIM, i * KT:(i + 1) * KT].astype(BF16))
            avT_ref[0, i, hd * VT_ROWS + A_HEAD_DIM:(hd + 1) * VT_ROWS, :] = ones
    iq_ref[0] = proj("iq").astype(BF16)
    bq_ref[0] = proj("bq").astype(BF16)
    bk_ref[0] = proj("bk").astype(BF16)
    bv_ref[0] = proj("bv").astype(BF16)
    bog_ref[0] = proj("bog")
    misc = proj("misc")
    misc_ref[0] = misc
    miscT_ref[0] = misc.T
    ik_lo = pltpu.roll(misc, LANES - MISC_IK, 1)
    ik2_ref[0] = jnp.where(lanes < MISC_IK, ik_lo, misc).astype(BF16)


def _inproj(x, ln_g, ln_b, w_perm, tm, pos0, valid_rows=None):
    nb, s, d = x.shape
    nt = s // tm
    row = lambda c: pl.BlockSpec((1, tm, c), lambda b, i: (b, i, 0))
    const = lambda shp: pl.BlockSpec(shp, lambda b, i: (0,) * len(shp))
    out_shape = (
        jax.ShapeDtypeStruct((nb, s, 512), BF16),
        jax.ShapeDtypeStruct((nb, s, KX_COLS), BF16),
        jax.ShapeDtypeStruct((nb, s // KT, A_HEADS * VT_ROWS, KT), BF16),
        jax.ShapeDtypeStruct((nb, s, 256), BF16),
        jax.ShapeDtypeStruct((nb, s, 256), BF16),
        jax.ShapeDtypeStruct((nb, s, 256), BF16),
        jax.ShapeDtypeStruct((nb, s, 512), BF16),
        jax.ShapeDtypeStruct((nb, s, 512), F32),
        jax.ShapeDtypeStruct((nb, s, 128), BF16),
        jax.ShapeDtypeStruct((nb, s, 128), F32),
        jax.ShapeDtypeStruct((nb, 128, s), F32),
    )
    out_specs = (
        row(512), row(KX_COLS),
        pl.BlockSpec((1, tm // KT, A_HEADS * VT_ROWS, KT), lambda b, i: (b, i, 0, 0)),
        row(256), row(256), row(256), row(512), row(512), row(128), row(128),
        pl.BlockSpec((1, 128, tm), lambda b, i: (b, 0, i)),
    )
    return pl.pallas_call(
        functools.partial(_inproj_kernel, pos0=pos0, valid_rows=valid_rows),
        grid=(nb, nt),
        in_specs=[row(d), const((1, d)), const((1, d)), const((d, PROJ_COLS))],
        out_specs=out_specs,
        out_shape=out_shape,
        compiler_params=pltpu.CompilerParams(
            dimension_semantics=("arbitrary", "arbitrary"), vmem_limit_bytes=VMEM_LIMIT),
        name="inproj",
    )(x, ln_g, ln_b, w_perm)


def _static_when(cond: bool):
    def deco(fn):
        if cond:
            fn()
    return deco


def _dsa_kernel(aq_ref, iq_ref, miscT_ref, akx_ref, avT_ref, ik2_ref,
                maq_ref, miq_ref, mmiscT_ref, makx_ref, mavT_ref, mik2_ref, qfeat_ref, ng_ref,
                out_ref,
                sc_s, scb_s, qm_s, iqm_s, iw_s, m_s, acc_s, bias_s, lg_s, p_s, neq_s, shift_s, corr_s,
                *, topk, is_meta):
    g = 0 if is_meta else pl.program_id(1) + 1
    n_past = 0 if is_meta else g - 1
    when_real = _static_when(not is_meta)
    when_meta = _static_when(is_meta)

    lane_half = lax.broadcasted_iota(I32, (1, LANES), 1) // A_HEAD_DIM

    def stage(aq, iq, miscT):
        for h in range(A_HEADS):
            p = h // 2
            qp = aq[:, p * LANES:(p + 1) * LANES]
            qm_s[h, :, 0:LANES] = jnp.where(lane_half == (h % 2), qp, jnp.zeros_like(qp))
            qm_s[h, :, LANES:2 * LANES] = jnp.broadcast_to(qfeat_ref[h:h + 1, :], (QB, LANES)).astype(BF16)
        for h in range(IDX_HEADS):
            p = h // 2
            qp = iq[:, p * LANES:(p + 1) * LANES]
            iqm_s[h] = jnp.where(lane_half == (h % 2), qp, jnp.zeros_like(qp))
        iw_s[...] = miscT[MISC_IW:MISC_IW + 8, :]

    lane = lax.broadcasted_iota(I32, (1, QB), 1)
    if is_meta:
        stage(maq_ref[...], miq_ref[...], mmiscT_ref[...])
        qpos = lane
    else:
        stage(aq_ref[0], iq_ref[0], miscT_ref[0])
        qpos = N_META + (g - 1) * QB + lane

    ninf = jnp.float32(-jnp.inf)

    def scores_for_tile(ik2_t, allowed):
        s = None
        for h in range(IDX_HEADS):
            sh = _dot_nt(ik2_t, iqm_s[h])
            term = iw_s[h:h + 1, :] * jnp.maximum(sh, 0.0)
            s = term if s is None else s + term
        return s if allowed is None else jnp.where(allowed, s, ninf)

    def store_scores(slot, r, s):
        sc_s[slot, 0:r, :] = s
        scb_s[slot, 0:r, :] = s.astype(BF16)

    mrow = lax.broadcasted_iota(I32, (MT, QB), 0)
    krow = lax.broadcasted_iota(I32, (KT, QB), 0)
    store_scores(0, MT, scores_for_tile(mik2_ref[0:MT, :], mrow < N_META))

    def fill_past(kt):
        store_scores(kt + 1, KT, scores_for_tile(ik2_ref[0, kt], None))

    def fill_pair(i, c):
        fill_past(2 * i)
        fill_past(2 * i + 1)
        return c

    @when_real
    def _():
        lax.fori_loop(0, n_past // 2, fill_pair, 0)

        @pl.when(n_past % 2 == 1)
        def _():
            fill_past(n_past - 1)

        store_scores(g, KT, scores_for_tile(ik2_ref[0, g - 1], (krow >> CHUNK_SHIFT) <= (lane >> CHUNK_SHIFT)))

    def count(arr_s, pred, pack):
        acc_t = jnp.int16 if pack == 16 else I32

        def cnt(tile):
            x = jnp.where(pred(tile), jnp.ones((), acc_t), jnp.zeros((), acc_t))
            x = x.reshape(tile.shape[0] // pack, pack, QB)
            parts = [x[j] for j in range(x.shape[0])]
            while len(parts) > 1:
                parts = [parts[j] + parts[j + 1] for j in range(0, len(parts), 2)]
            return parts[0]

        acc = cnt(arr_s[0, 0:MT, :])
        acc = lax.fori_loop(0, g, lambda kt, a: a + cnt(arr_s[kt + 1]), acc)
        return jnp.sum(acc.astype(I32), axis=0, keepdims=True)

    def ordered16_to_bits(k):
        return jnp.where(k >= BF16_SIGN, k ^ BF16_SIGN, k ^ BF16_ALL)

    def coarse_body(i, u):
        cand_u = u | jnp.left_shift(jnp.int32(1), BF16_BITS - 1 - i)
        finite = (cand_u >= U_NEG_INF) & (cand_u <= U_POS_INF)
        bits = ordered16_to_bits(jnp.clip(cand_u, U_NEG_INF, U_POS_INF))
        cand = lax.bitcast_convert_type(jnp.left_shift(bits, BF16_BITS), F32).astype(BF16)
        cnt = count(scb_s, lambda tile: tile >= cand, 16)
        return jnp.where(finite & (cnt >= topk), cand_u, u)

    u1 = lax.fori_loop(0, BF16_BITS, coarse_body, jnp.zeros((1, QB), I32))
    few = u1 == 0
    t1_bits = jnp.left_shift(ordered16_to_bits(jnp.where(few, BF16_SIGN, u1)), BF16_BITS)
    base = (t1_bits ^ ((t1_bits >> 31) & F32_MAGNITUDE)) - BF16_SIGN

    def key_to_f32(k):
        return lax.bitcast_convert_type(k ^ ((k >> 31) & F32_MAGNITUDE), F32)

    def fine_body(i, o):
        cand_o = o | jnp.left_shift(jnp.int32(1), FINE_BITS - 1 - i)
        cand = key_to_f32(base + cand_o)
        cnt = count(sc_s, lambda tile: tile >= cand, 8)
        return jnp.where(cnt >= topk, cand_o, o)

    o2 = lax.fori_loop(0, FINE_BITS, fine_body, jnp.zeros((1, QB), I32))
    thr = jnp.where(few, ninf, key_to_f32(base + o2))
    n_gt = count(sc_s, lambda tile: tile > thr, 8)
    need = jnp.where(few, 0, topk - n_gt).astype(F32)

    m_s[...] = jnp.full(m_s.shape, -jnp.inf, F32)
    acc_s[...] = jnp.zeros(acc_s.shape, F32)

    def offsets(subs):
        offs, row0 = [], 0
        for sub in subs:
            offs.append((row0, sub[0].shape[0]))
            row0 += sub[0].shape[0]
        return offs, row0

    def bias_stage(subs):
        offs, _ = offsets(subs)
        n_eq_after = None
        for (sc_t, _, _, _, n_eq_before), (o, r) in zip(subs, offs):
            if n_eq_before is None:
                n_eq_before = n_eq_after
            ri = lax.broadcasted_iota(I32, (r, r), 0)
            ci = lax.broadcasted_iota(I32, (r, r), 1)
            lower = jnp.where(ri > ci, 1.0, 0.0).astype(BF16)
            eq = sc_t == thr
            rank = _dot(lower, jnp.where(eq, 1.0, 0.0).astype(BF16)) + n_eq_before
            bias_s[o:o + r, :] = jnp.where(
                sc_t > thr, 0.0, jnp.where(eq, jnp.where(rank < need, 0.0, ninf), ninf))
            n_eq_after = rank[r - 1:r, :] + jnp.where(eq[r - 1:r, :], 1.0, 0.0)
        return n_eq_after

    def logits_head(subs, h):
        offs, _ = offsets(subs)
        m_old = m_s[h]
        m_new = m_old
        for (_, kx_fn, _, after, _), (o, r) in zip(subs, offs):
            lg = _dot_nt(kx_fn(h // 2), qm_s[h]) + bias_s[o:o + r, :]
            if after is not None:
                lg = lg + ALIBI_C[h] * after
            lg_s[h, o:o + r, :] = lg
            m_new = jnp.maximum(m_new, jnp.max(lg, axis=0, keepdims=True))
        m_s[h] = m_new
        m_safe = jnp.where(m_new == ninf, 0.0, m_new)
        shift_s[h] = m_safe
        corr_s[h] = jnp.exp2(m_old - m_safe)

    def softmax_pv_head(subs, h):
        offs, rows = offsets(subs)
        p_s[h, 0:rows, :] = jnp.exp2(lg_s[h, 0:rows, :] - shift_s[h]).astype(BF16)
        acc = acc_s[h] * corr_s[h]
        for (_, _, vT_fn, _, _), (o, r) in zip(subs, offs):
            acc = acc + _dot(vT_fn(h), p_s[h, o:o + r, :])
        acc_s[h] = acc

    def run_stages(done, nxt):
        n_eq_after = bias_stage(nxt) if nxt is not None else None
        for h in range(A_HEADS):
            if done is not None:
                softmax_pv_head(done, h)
            if nxt is not None:
                logits_head(nxt, h)
        return n_eq_after

    def after_term(kpos):
        return jnp.minimum(2 * (qpos - kpos), 0).astype(F32)

    def meta_sub():
        return (sc_s[0, 0:MT, :],
                lambda p: makx_ref[0:MT, 2 * p * LANES:(2 * p + 2) * LANES],
                lambda h: mavT_ref[0, h * VT_ROWS:(h + 1) * VT_ROWS, 0:MT],
                after_term(mrow), jnp.zeros((1, QB), F32))

    def real_sub(kt, after, n_eq_before):
        return (sc_s[kt + 1],
                lambda p: akx_ref[0, kt, :, 2 * p * LANES:(2 * p + 2) * LANES],
                lambda h: avT_ref[0, kt, h * VT_ROWS:(h + 1) * VT_ROWS, :],
                after, n_eq_before)

    neq_s[...] = jnp.sum(jnp.where(sc_s[0, 0:MT, :] == thr, 1.0, 0.0), axis=0, keepdims=True)

    def past(kt):
        return [real_sub(kt, None, neq_s[...])]

    def last_group():
        return [meta_sub(), real_sub(g - 1, after_term(N_META + (g - 1) * KT + krow), neq_s[...])]

    @when_real
    def _():
        @pl.when(n_past > 0)
        def _():
            neq_s[...] = run_stages(None, past(0))

        def pipe_body(i, c):
            neq_s[...] = run_stages(past(i), past(i + 1))
            return c

        lax.fori_loop(0, jnp.maximum(n_past - 1, 0), pipe_body, 0)

        @pl.when(n_past > 0)
        def _():
            run_stages(past(n_past - 1), last_group())

        @pl.when(n_past == 0)
        def _():
            run_stages(None, last_group())

        run_stages(last_group(), None)

    @when_meta
    def _():
        run_stages(None, [meta_sub()])
        run_stages([meta_sub()], None)

    outs = []
    for h in range(A_HEADS):
        o = acc_s[h, 0:A_HEAD_DIM, :] / acc_s[h, A_HEAD_DIM:A_HEAD_DIM + 1, :]
        ms = jnp.mean(o * o, axis=0, keepdims=True)
        outs.append(o * lax.rsqrt(ms + RMS_EPS))
    res = (jnp.concatenate(outs, axis=0).T * ng_ref[...]).astype(BF16)

    out_ref[0] = res


def _alibi_query_features():
    rows = []
    for c in ALIBI_C:
        rest = jnp.float32(c)
        lanes = []
        for _ in range(N_SLOPE_PARTS):
            part = rest.astype(BF16).astype(F32)
            lanes += [part * CHUNK, part]
            rest = rest - part
        rows.append(jnp.stack(lanes + [jnp.float32(0.0)] * (LANES - len(lanes))))
    return jnp.stack(rows)


def _dsa(aq, iq, miscT, akx, avT, ik2, m, ng, topk, is_meta):
    nb, s, _ = aq.shape
    nq = s // QB
    nkt = s // KT
    vt = A_HEADS * VT_ROWS
    qidx = lambda b, i: (b, i, 0)
    const = lambda shp: pl.BlockSpec(shp, lambda b, i: (0,) * len(shp))
    in_specs = [
        pl.BlockSpec((1, QB, 512), qidx),
        pl.BlockSpec((1, QB, 256), qidx),
        pl.BlockSpec((1, 128, QB), lambda b, i: (b, 0, i)),
        pl.BlockSpec((1, nkt, KT, KX_COLS), lambda b, i: (b, 0, 0, 0)),
        pl.BlockSpec((1, nkt, vt, KT), lambda b, i: (b, 0, 0, 0)),
        pl.BlockSpec((1, nkt, KT, 128), lambda b, i: (b, 0, 0, 0)),
        const((META_ROWS, 512)), const((META_ROWS, 256)), const((128, META_ROWS)),
        const((META_ROWS, KX_COLS)), const((1, vt, KT)), const((META_ROWS, 128)),
        const((A_HEADS, LANES)), const((1, 512)),
    ]
    out_shape = jax.ShapeDtypeStruct((nb, s, 512), BF16)
    out_specs = pl.BlockSpec((1, QB, 512), qidx)
    scratch = [
        pltpu.VMEM((nkt + 1, KT, QB), F32),
        pltpu.VMEM((nkt + 1, KT, QB), BF16),
        pltpu.VMEM((A_HEADS, QB, 2 * LANES), BF16),
        pltpu.VMEM((IDX_HEADS, QB, LANES), BF16),
        pltpu.VMEM((8, QB), F32),
        pltpu.VMEM((A_HEADS, 1, QB), F32),
        pltpu.VMEM((A_HEADS, VT_ROWS, QB), F32),
        pltpu.VMEM((GROUP_ROWS, QB), F32),
        pltpu.VMEM((A_HEADS, GROUP_ROWS, QB), F32),
        pltpu.VMEM((A_HEADS, GROUP_ROWS, QB), BF16),
        pltpu.VMEM((1, QB), F32),
        pltpu.VMEM((A_HEADS, 1, QB), F32),
        pltpu.VMEM((A_HEADS, 1, QB), F32),
    ]
    return pl.pallas_call(
        functools.partial(_dsa_kernel, topk=topk, is_meta=is_meta),
        grid=(nb, nq),
        in_specs=in_specs,
        out_specs=out_specs,
        out_shape=out_shape,
        scratch_shapes=scratch,
        compiler_params=pltpu.CompilerParams(
            dimension_semantics=("arbitrary", "arbitrary"), vmem_limit_bytes=VMEM_LIMIT),
        name="dsa",
    )(aq, iq, miscT, akx.reshape(nb, nkt, KT, KX_COLS), avT, ik2.reshape(nb, nkt, KT, 128),
      m["aq"], m["iq"], m["miscT"], m["akx"], m["avT"], m["ik2"], _alibi_query_features(), ng)


def _chunk_cumsum_rows(x):
    row_in_chunk = lax.broadcasted_iota(I32, x.shape, 0) & (CHUNK - 1)
    sh = 1
    while sh < CHUNK:
        x = x + jnp.where(row_in_chunk >= sh, pltpu.roll(x, sh, 0), 0.0)
        sh *= 2
    return x


def _rows_of_chunk(rows_per_chunk):
    return jnp.concatenate([jnp.broadcast_to(r, (CHUNK, r.shape[1])) for r in rows_per_chunk], axis=0)


def _gla_block(q, k, v, lr, og, wg, bg, ng, st_s, valid_rows):
    nrow = q.shape[0]
    nch = nrow // CHUNK
    z = _dot(lr.astype(BF16), wg) + bg
    logg = (jnp.minimum(z, 0.0) - jnp.log(1.0 + jnp.exp(-jnp.abs(z)))) / B_GATE_TAU
    if valid_rows is not None:
        rows = lax.broadcasted_iota(I32, logg.shape, 0)
        logg = jnp.where(rows < valid_rows, logg, 0.0)
    b = _chunk_cumsum_rows(logg)
    b_last_rows = [b[(c + 1) * CHUNK - 1:(c + 1) * CHUNK, :] for c in range(nch)]
    b_mid = _rows_of_chunk([b[c * CHUNK + CHUNK // 2 - 1:c * CHUNK + CHUNK // 2, :] for c in range(nch)])
    b_last = _rows_of_chunk(b_last_rows)
    q_in = q * jnp.exp(b)
    q_ic = q * jnp.exp(b - b_mid)
    k_ic = k * jnp.exp(b_mid - b)
    k_st = k * jnp.exp(b_last - b)
    decay = [jnp.exp(r) for r in b_last_rows]

    lane_half = lax.broadcasted_iota(I32, (1, LANES), 1) // B_KEY_DIM
    ri = lax.broadcasted_iota(I32, (nrow, nrow), 0)
    ci = lax.broadcasted_iota(I32, (nrow, nrow), 1)
    causal = ((ri >> CHUNK_SHIFT) == (ci >> CHUNK_SHIFT)) & (ci <= ri)
    outs = []
    for p in range(B_HEADS // 2):
        sl = slice(p * LANES, (p + 1) * LANES)
        heads = (2 * p, 2 * p + 1)
        v_h = [v[:, h * B_VAL_DIM:(h + 1) * B_VAL_DIM] for h in heads]
        k_st_h = [jnp.where(lane_half == hh, k_st[:, sl], 0.0).astype(BF16) for hh in range(2)]
        states = [st_s[p]]
        for c in range(nch):
            rs = slice(c * CHUNK, (c + 1) * CHUNK)
            upd = _dot_tn(v_h[0][rs], k_st_h[0][rs]) + _dot_tn(v_h[1][rs], k_st_h[1][rs])
            states.append(states[c] * decay[c][:, sl] + upd)
        st_s[p] = states[nch]
        k_ic_p = k_ic[:, sl].astype(BF16)
        for hh, h in enumerate(heads):
            hm = lane_half == hh
            a = _dot_nt(jnp.where(hm, q_ic[:, sl], 0.0).astype(BF16), k_ic_p)
            o = _dot(jnp.where(causal, a, 0.0).astype(BF16), v_h[hh])
            q_in_h = jnp.where(hm, q_in[:, sl], 0.0).astype(BF16)
            o = o + jnp.concatenate(
                [_dot_nt(q_in_h[c * CHUNK:(c + 1) * CHUNK], states[c].astype(BF16)) for c in range(nch)], axis=0)
            ms = jnp.mean(o * o, axis=-1, keepdims=True)
            on = o * lax.rsqrt(ms + RMS_EPS) * ng[:, h * B_VAL_DIM:(h + 1) * B_VAL_DIM]
            og_h = og[:, h * B_VAL_DIM:(h + 1) * B_VAL_DIM]
            outs.append(on * (og_h * jax.nn.sigmoid(og_h)))
    return outs


def _gla_kernel(bq_ref, bk_ref, bv_ref, misc_ref, bog_ref,
                mbq_ref, mbk_ref, mbv_ref, mmisc_ref, mbog_ref,
                wg_ref, bg_ref, ng_ref, out_ref, mout_ref, st_s):
    j = pl.program_id(1)
    wg = wg_ref[...]
    bg = bg_ref[...]
    ng = ng_ref[...]

    @pl.when(j == 0)
    def _():
        st_s[...] = jnp.zeros(st_s.shape, F32)
        outs = _gla_block(mbq_ref[0:CHUNK, :].astype(F32), mbk_ref[0:CHUNK, :].astype(F32),
                          mbv_ref[0:CHUNK, :], mmisc_ref[0:CHUNK, MISC_LR:MISC_LR + B_GATE_RANK],
                          mbog_ref[0:CHUNK, :], wg, bg, ng, st_s, N_META)
        mout_ref[CHUNK:, :] = jnp.zeros((META_ROWS - CHUNK, B_WIDTH), BF16)
        for h, o in enumerate(outs):
            mout_ref[0:CHUNK, h * B_VAL_DIM:(h + 1) * B_VAL_DIM] = o.astype(BF16)

    outs = _gla_block(bq_ref[0].astype(F32), bk_ref[0].astype(F32), bv_ref[0],
                      misc_ref[0, :, MISC_LR:MISC_LR + B_GATE_RANK], bog_ref[0], wg, bg, ng, st_s, None)
    for h, o in enumerate(outs):
        out_ref[0, :, h * B_VAL_DIM:(h + 1) * B_VAL_DIM] = o.astype(BF16)


def _gla(bq, bk, bv, misc, bog, m, wg, bg, ng):
    nb, s, _ = bq.shape
    row = lambda c: pl.BlockSpec((1, GLA_BLOCK, c), lambda b, j: (b, j, 0))
    const = lambda shp: pl.BlockSpec(shp, lambda b, j: (0,) * len(shp))
    return pl.pallas_call(
        _gla_kernel,
        grid=(nb, s // GLA_BLOCK),
        in_specs=[row(256), row(256), row(512), row(128), row(512),
                  const((META_ROWS, 256)), const((META_ROWS, 256)), const((META_ROWS, 512)),
                  const((META_ROWS, 128)), const((META_ROWS, 512)),
                  const((B_GATE_RANK, 256)), const((1, 256)), const((1, 512))],
        out_specs=(row(512), const((META_ROWS, 512))),
        out_shape=(jax.ShapeDtypeStruct((nb, s, 512), BF16), jax.ShapeDtypeStruct((META_ROWS, 512), BF16)),
        scratch_shapes=[pltpu.VMEM((B_HEADS // 2, LANES, LANES), F32)],
        compiler_params=pltpu.CompilerParams(
            dimension_semantics=("arbitrary", "arbitrary"), vmem_limit_bytes=VMEM_LIMIT),
        name="gla",
    )(bq, bk, bv, misc, bog, m["bq"], m["bk"], m["bv"], m["misc"], m["bog"], wg, bg, ng)


def _outproj_kernel(x_ref, a_ref, b_ref, lng_ref, lnb_ref, wo_ref, g1_ref, b1_ref, out_ref):
    h = _layer_norm(x_ref[0], lng_ref[...], lnb_ref[...])
    mixed = _dot(a_ref[0], wo_ref[0:A_WIDTH, :]) + _dot(b_ref[0], wo_ref[A_WIDTH:, :])
    out_ref[0] = _layer_norm(ALPHA * h + mixed, g1_ref[...], b1_ref[...])


def _outproj(x, a, b, lng, lnb, wo, g1, b1, tm):
    nb, s, d = x.shape
    row = lambda c: pl.BlockSpec((1, tm, c), lambda bb, i: (bb, i, 0))
    const = lambda shp: pl.BlockSpec(shp, lambda bb, i: (0,) * len(shp))
    return pl.pallas_call(
        _outproj_kernel,
        grid=(nb, s // tm),
        in_specs=[row(d), row(512), row(512), const((1, d)), const((1, d)), const((d, d)),
                  const((1, d)), const((1, d))],
        out_specs=row(d),
        out_shape=jax.ShapeDtypeStruct((nb, s, d), F32),
        compiler_params=pltpu.CompilerParams(
            dimension_semantics=("arbitrary", "arbitrary"), vmem_limit_bytes=VMEM_LIMIT),
        name="outproj",
    )(x, a, b, lng, lnb, wo, g1, b1)


def _gelu_tanh(x):
    c = 0.7978845608028654
    return 0.5 * x * (1.0 + jnp.tanh(c * (x + 0.044715 * (x * x * x))))


def _ffn_kernel(x_ref, xh_ref, a_ref, ah_ref, b_ref, bh_ref, hm_ref, lng_ref, lnb_ref, wo_ref, g1_ref, b1_ref,
                wup_ref, cw_ref, cb_ref, wdn_ref, g2_ref, b2_ref, out_ref, y_s):
    j = pl.program_id(1)
    def mix(x, a_mix, b_mix):
        mixed = _dot(a_mix, wo_ref[0:A_WIDTH, :]) + _dot(b_mix, wo_ref[A_WIDTH:, :])
        return _layer_norm(ALPHA * _layer_norm(x, lng_ref[...], lnb_ref[...]) + mixed, g1_ref[...], b1_ref[...])

    half = x_ref.shape[1] // 2
    h1_lo = mix(jnp.concatenate([xh_ref[0], x_ref[0, 0:half, :]], axis=0),
                jnp.concatenate([ah_ref[0], a_ref[0, 0:half, :]], axis=0),
                jnp.concatenate([bh_ref[0], b_ref[0, 0:half, :]], axis=0))
    h1_hi = mix(x_ref[0, half:, :], a_ref[0, half:, :], b_ref[0, half:, :])
    h = jnp.concatenate([h1_lo[HALO:, :], h1_hi], axis=0)
    halo = jnp.where(j == 0, hm_ref[...], h1_lo[0:HALO, :])
    hb = jnp.concatenate([halo, h], axis=0).astype(BF16)
    for f in range(D_FF // FF_TILE):
        fs = slice(f * FF_TILE, (f + 1) * FF_TILE)
        a = _dot(hb, wup_ref[:, fs])
        gate = _dot(hb[HALO:, :], wup_ref[:, D_FF + f * FF_TILE:D_FF + (f + 1) * FF_TILE])
        cw = cw_ref[:, fs]
        conv = cb_ref[:, fs] + cw[CONV_W - 1:CONV_W, :] * a[HALO:, :]
        for back in range(1, CONV_W):
            tap = cw[CONV_W - 1 - back:CONV_W - back, :]
            conv = conv + tap * pltpu.roll(a, back, 0)[HALO:, :]
        y_s[:, fs] = (_gelu_tanh(conv) * gate).astype(BF16)
    ffn = _dot(y_s[...], wdn_ref[...])
    out_ref[0] = _layer_norm(ALPHA * h + ffn, g2_ref[...], b2_ref[...])


def _ffn(x, a, b, h1_meta, lng, lnb, wo, g1, b1, wup, cw, cb, wdn, g2, b2, tm):
    nb, s, d = x.shape
    per = tm // HALO
    const = lambda shp: pl.BlockSpec(shp, lambda bb, j: (0,) * len(shp), pipeline_mode=pl.Buffered(1))
    main = lambda c: pl.BlockSpec((1, tm, c), lambda bb, j: (bb, j, 0))
    halo = lambda c: pl.BlockSpec((1, HALO, c), lambda bb, j: (bb, jnp.maximum(j * per - 1, 0), 0))
    return pl.pallas_call(
        _ffn_kernel,
        grid=(nb, s // tm),
        in_specs=[main(d), halo(d), main(A_WIDTH), halo(A_WIDTH), main(B_WIDTH), halo(B_WIDTH),
                  const((HALO, d)), const((1, d)), const((1, d)), const((d, d)), const((1, d)), const((1, d)),
                  const((d, 2 * D_FF)), const((CONV_W, D_FF)), const((1, D_FF)),
                  const((D_FF, d)), const((1, d)), const((1, d))],
        out_specs=main(d),
        out_shape=jax.ShapeDtypeStruct((nb, s, d), F32),
        scratch_shapes=[pltpu.VMEM((tm, D_FF), BF16)],
        compiler_params=pltpu.CompilerParams(
            dimension_semantics=("arbitrary", "arbitrary"), vmem_limit_bytes=VMEM_LIMIT),
        name="ffn",
    )(x, x, a, a, b, b, h1_meta, lng, lnb, wo, g1, b1, wup, cw, cb, wdn, g2, b2)


def _permute_w_in(w):
    o = 0
    parts = {}
    for name, n in (("a_q", 512), ("a_k", 512), ("a_v", 512), ("i_q", 256), ("i_k", 64), ("i_w", 4),
                    ("b_q", 256), ("b_k", 256), ("b_v", 512), ("b_lr", 16), ("b_og", 512)):
        parts[name] = w[:, o:o + n]
        o += n
    zeros = jnp.zeros((w.shape[0], MISC_IK - B_GATE_RANK - IDX_HEADS), w.dtype)
    cols = [parts["a_q"] * (A_HEAD_DIM ** -0.5 * LOG2E), parts["a_k"], parts["a_v"], parts["i_q"],
            parts["b_q"] * (B_KEY_DIM ** -0.5), parts["b_k"], parts["b_v"], parts["b_og"],
            parts["b_lr"], parts["i_w"] * (IDX_HEADS ** -0.5) * (IDX_DIM ** -0.5), zeros, parts["i_k"]]
    return jnp.concatenate(cols, axis=1).astype(BF16)


def kernel(x, meta, ln_in_g, ln_in_b, w_in, w_gate_b, b_gate_b, attn_norm_g, gla_norm_g, w_out,
           ln1_g, ln1_b, w_up, conv_w, conv_b, w_down, ln2_g, ln2_b):
    nb, s, d = x.shape
    assert d == D_MODEL and w_in.shape[0] == DEPTH == 1
    assert s % ROW_TILE == 0 and s % INPROJ_TILE == 0 and s % KT == 0 and s % GLA_BLOCK == 0
    topk = min(TOPK_MAX, s // 4)
    l = 0
    r2 = lambda v: v.reshape(1, -1)
    lng, lnb = r2(ln_in_g), r2(ln_in_b)
    w_perm = _permute_w_in(w_in[l])
    wo = w_out[l].astype(BF16)
    wup = w_up[l].astype(BF16)
    wdn = w_down[l].astype(BF16)
    wg = w_gate_b[l].astype(BF16)
    bg = r2(b_gate_b[l])
    ng_a = r2(attn_norm_g[l])
    ng_b = r2(gla_norm_g[l])

    x_meta = jnp.zeros((1, META_ROWS, d), x.dtype).at[0, :N_META].set(meta.astype(x.dtype))
    names = ("aq", "akx", "avT", "iq", "bq", "bk", "bv", "bog", "ik2", "misc", "miscT")
    m = {n: v[0] for n, v in zip(names, _inproj(x_meta, lng, lnb, w_perm, META_ROWS, 0, valid_rows=N_META))}

    aq, akx, avT, iq, bq, bk, bv, bog, ik2, misc, miscT = _inproj(x, lng, lnb, w_perm, INPROJ_TILE, N_META)
    a_out = _dsa(aq, iq, miscT, akx, avT, ik2, m, ng_a, topk, False)
    a_out_m = _dsa(m["aq"][None], m["iq"][None], m["miscT"][None], m["akx"][None], m["avT"][None],
                   m["ik2"][None], m, ng_a, topk, True)[0]
    b_out, b_out_m = _gla(bq, bk, bv, misc, bog, m, wg, bg, ng_b)

    g1, b1 = r2(ln1_g[l]), r2(ln1_b[l])
    h1_m = _outproj(x_meta, a_out_m[None], b_out_m[None], lng, lnb, wo, g1, b1, META_ROWS)
    return _ffn(x, a_out, b_out, h1_m[0, N_META - HALO:N_META], lng, lnb, wo, g1, b1,
                wup, conv_w[l], r2(conv_b[l]), wdn, r2(ln2_g[l]), r2(ln2_b[l]), ROW_TILE)
```
